```python
import jax, jax.numpy as jnp
from jax import lax
import numpy as np

D_MODEL = 2048
BATCH = 8
SEQ = 4096
DEPTH = 4

D_MIX = D_MODEL
D_CONV = D_MIX // 2
CONV_HEADS = 16
D_POOL = D_MIX - D_CONV
POOL_WINDOWS = (2, 4, 8, 16)
N_POOL_GROUPS = len(POOL_WINDOWS)
POOL_GROUP_DIM = D_POOL // N_POOL_GROUPS
CONV_WIDTH = 3
D_IN_PROJ = 3 * D_CONV + D_POOL
D_FF = 4 * D_MODEL
EPS = 1e-6

kernel_name = "hybrid_shortconv_pool_sqrelu_trunk"


def _rmsnorm(x, g):
    xf = x.astype(jnp.float32)
    r = lax.rsqrt(jnp.mean(xf * xf, axis=-1, keepdims=True) + EPS)
    return (xf * r).astype(x.dtype) * g


def _short_conv_mixer(b, c, xt, conv_w):
    u = c * xt
    s = u.shape[1]
    u_pad = jnp.pad(u, ((0, 0), (CONV_WIDTH - 1, 0), (0, 0)))
    conv = conv_w[0] * u_pad[:, 0:s] + conv_w[1] * u_pad[:, 1:s + 1] + conv_w[2] * u_pad[:, 2:s + 2]
    return b * conv


def _pool_mixer(v, pool_w, pool_scale):
    bsz, s, _ = v.shape
    vg = v.reshape(bsz, s, N_POOL_GROUPS, POOL_GROUP_DIM)
    csum = jnp.cumsum(vg.astype(jnp.float32), axis=1)
    pos = jnp.arange(s, dtype=jnp.float32)
    outs = []
    for g, w in enumerate(POOL_WINDOWS):
        cs = csum[:, :, g]
        lagged = jnp.pad(cs[:, :s - w], ((0, 0), (w, 0), (0, 0)))
        count = jnp.minimum(pos + 1.0, float(w))[None, :, None]
        mean = (cs - lagged) / count
        outs.append(mean.astype(v.dtype) - vg[:, :, g])
    d = jnp.stack(outs, axis=2)
    y = jnp.einsum('bsgc,gcd->bsgd', d, pool_w).reshape(bsz, s, D_POOL)
    return y * pool_scale


def _fwd_setup_inputs(seed: int = 0) -> dict:
    key = jax.random.key(seed)
    ks = jax.random.split(key, 12)
    f32 = jnp.float32
    x = jax.random.normal(ks[0], (BATCH, SEQ, D_MODEL), f32)
    w_in = jax.random.normal(ks[1], (DEPTH, D_MODEL, D_IN_PROJ), f32) * D_MODEL ** -0.5
    conv_w = jax.random.normal(ks[2], (DEPTH, CONV_WIDTH, D_CONV), f32) * CONV_WIDTH ** -0.5
    pool_w = jax.random.normal(ks[3], (DEPTH, N_POOL_GROUPS, POOL_GROUP_DIM, POOL_GROUP_DIM), f32) * POOL_GROUP_DIM ** -0.5
    pool_scale = 1.0 + 0.1 * jax.random.normal(ks[4], (DEPTH, D_POOL), f32)
    w_out = jax.random.normal(ks[5], (DEPTH, D_MIX, D_MODEL), f32) * D_MIX ** -0.5
    norm_mix = 1.0 + 0.05 * jax.random.normal(ks[6], (DEPTH, D_MODEL), f32)
    norm_mlp = 1.0 + 0.05 * jax.random.normal(ks[7], (DEPTH, D_MODEL), f32)
    w_up = jax.random.normal(ks[8], (DEPTH, D_MODEL, D_FF), f32) * D_MODEL ** -0.5
    w_down = jax.random.normal(ks[9], (DEPTH, D_FF, D_MODEL), f32) * D_FF ** -0.5
    norm_final = 1.0 + 0.05 * jax.random.normal(ks[10], (D_MODEL,), f32)
    return {"x": x, "w_in": w_in, "conv_w": conv_w, "pool_w": pool_w, "pool_scale": pool_scale,
            "w_out": w_out, "norm_mix": norm_mix, "norm_mlp": norm_mlp, "w_up": w_up,
            "w_down": w_down, "norm_final": norm_final}


def _fwd_reference(x, w_in, conv_w, pool_w, pool_scale, w_out, norm_mix, norm_mlp, w_up, w_down, norm_final):
    for l in range(DEPTH):
        h = _rmsnorm(x, norm_mix[l])
        proj = jnp.einsum('bsd,de->bse', h, w_in[l])
        b = proj[..., 0:D_CONV]
        c = proj[..., D_CONV:2 * D_CONV]
        xt = proj[..., 2 * D_CONV:3 * D_CONV]
        v = proj[..., 3 * D_CONV:]
        y_conv = _short_conv_mixer(b, c, xt, conv_w[l])
        y_pool = _pool_mixer(v, pool_w[l], pool_scale[l])
        y = jnp.concatenate([y_conv, y_pool], axis=-1)
        x = x + jnp.einsum('bse,ed->bsd', y, w_out[l])
        h = _rmsnorm(x, norm_mlp[l])
        u = jax.nn.relu(jnp.einsum('bsd,df->bsf', h, w_up[l]))
        x = x + jnp.einsum('bsf,fd->bsd', u * u, w_down[l])
    return _rmsnorm(x, norm_final)


import jax as _jax
import jax.numpy as _jnp

TWIN_FORMAT = 'train_step'
FWD_PARAMS = ['x', 'w_in', 'conv_w', 'pool_w', 'pool_scale', 'w_out', 'norm_mix', 'norm_mlp', 'w_up', 'w_down', 'norm_final']
TWIN_WEIGHTS = ['w_in', 'conv_w', 'pool_w', 'pool_scale', 'w_out', 'norm_mix', 'norm_mlp', 'w_up', 'w_down', 'norm_final']
TWIN_DIFF_INPUT = 'x'
TWIN_INPUTS = ['x', 'w_in', 'conv_w', 'pool_w', 'pool_scale', 'w_out', 'norm_mix', 'norm_mlp', 'w_up', 'w_down', 'norm_final', 'loss_target', 'm_w_in', 'm_conv_w', 'm_pool_w', 'm_pool_scale', 'm_w_out', 'm_norm_mix', 'm_norm_mlp', 'm_w_up', 'm_w_down', 'm_norm_final', 'v_w_in', 'v_conv_w', 'v_pool_w', 'v_pool_scale', 'v_w_out', 'v_norm_mix', 'v_norm_mlp', 'v_w_up', 'v_w_down', 'v_norm_final']
TWIN_OUTPUTS = ['loss', 'grad_x', 'grad_w_in', 'grad_conv_w', 'grad_pool_w', 'grad_pool_scale', 'grad_w_out', 'grad_norm_mix', 'grad_norm_mlp', 'grad_w_up', 'grad_w_down', 'grad_norm_final', 'delta_w_in', 'delta_conv_w', 'delta_pool_w', 'delta_pool_scale', 'delta_w_out', 'delta_norm_mix', 'delta_norm_mlp', 'delta_w_up', 'delta_w_down', 'delta_norm_final', 'new_m_w_in', 'new_m_conv_w', 'new_m_pool_w', 'new_m_pool_scale', 'new_m_w_out', 'new_m_norm_mix', 'new_m_norm_mlp', 'new_m_w_up', 'new_m_w_down', 'new_m_norm_final', 'new_v_w_in', 'new_v_conv_w', 'new_v_pool_w', 'new_v_pool_scale', 'new_v_w_out', 'new_v_norm_mix', 'new_v_norm_mlp', 'new_v_w_up', 'new_v_w_down', 'new_v_norm_final']
TWIN_LEAF_KINDS = {'loss': 'loss', 'grad_x': 'grad_x', 'grad_w_in': 'grad_w', 'grad_conv_w': 'grad_w', 'grad_pool_w': 'grad_w', 'grad_pool_scale': 'grad_w', 'grad_w_out': 'grad_w', 'grad_norm_mix': 'grad_w', 'grad_norm_mlp': 'grad_w', 'grad_w_up': 'grad_w', 'grad_w_down': 'grad_w', 'grad_norm_final': 'grad_w', 'delta_w_in': 'delta_w', 'delta_conv_w': 'delta_w', 'delta_pool_w': 'delta_w', 'delta_pool_scale': 'delta_w', 'delta_w_out': 'delta_w', 'delta_norm_mix': 'delta_w', 'delta_norm_mlp': 'delta_w', 'delta_w_up': 'delta_w', 'delta_w_down': 'delta_w', 'delta_norm_final': 'delta_w', 'new_m_w_in': 'new_m', 'new_m_conv_w': 'new_m', 'new_m_pool_w': 'new_m', 'new_m_pool_scale': 'new_m', 'new_m_w_out': 'new_m', 'new_m_norm_mix': 'new_m', 'new_m_norm_mlp': 'new_m', 'new_m_w_up': 'new_m', 'new_m_w_down': 'new_m', 'new_m_norm_final': 'new_m', 'new_v_w_in': 'new_v', 'new_v_conv_w': 'new_v', 'new_v_pool_w': 'new_v', 'new_v_pool_scale': 'new_v', 'new_v_w_out': 'new_v', 'new_v_norm_mix': 'new_v', 'new_v_norm_mlp': 'new_v', 'new_v_w_up': 'new_v', 'new_v_w_down': 'new_v', 'new_v_norm_final': 'new_v'}


def _forward(args):
    return _fwd_reference(*[args[k] for k in FWD_PARAMS])


def _output_shape():
    out = _jax.eval_shape(lambda: _forward(_fwd_setup_inputs(0)))
    return out.shape, out.dtype

N_MICROBATCH = 1
ADAM_LR = 0.001
ADAM_B1 = 0.9
ADAM_B2 = 0.999
ADAM_EPS = 1e-08
ADAM_WD = 0.01
ADAM_STEP = 10
PER_EXAMPLE_BATCH_AXIS = {'x': 0, 'loss_target': 0}
SHARED_INPUTS = []
_WEIGHT_DTYPES = {'w_in': _jnp.float32, 'conv_w': _jnp.float32, 'pool_w': _jnp.float32, 'pool_scale': _jnp.float32, 'w_out': _jnp.float32, 'norm_mix': _jnp.float32, 'norm_mlp': _jnp.float32, 'w_up': _jnp.float32, 'w_down': _jnp.float32, 'norm_final': _jnp.float32}
MOMENT_SCALE = {'w_in': 5.931662e-02, 'conv_w': 6.102598e-02, 'pool_w': 5.322460e-02, 'pool_scale': 5.562003e-02, 'w_out': 5.736983e-02, 'norm_mix': 8.858274e-02, 'norm_mlp': 6.734380e-02, 'w_up': 3.398450e-02, 'w_down': 7.993900e-02, 'norm_final': 1.622971e+01}


def _to_microbatches(a, axis):
    t = _jnp.moveaxis(a, axis, 0)
    t = t.reshape((N_MICROBATCH, t.shape[0] // N_MICROBATCH) + t.shape[1:])
    return _jnp.moveaxis(t, 1, axis + 1)


def setup_inputs(seed: int = 0) -> dict:
    inp = _fwd_setup_inputs(seed)
    key = _jax.random.fold_in(_jax.random.key(seed), 7919)
    shape, _ = _output_shape()
    out = dict(inp)
    out["loss_target"] = _jax.random.normal(_jax.random.fold_in(key, 0), shape, _jnp.float32)
    for i, name in enumerate(TWIN_WEIGHTS):
        w = inp[name].astype(_jnp.float32)
        if MOMENT_SCALE is None:
            s = _jnp.sqrt(_jnp.mean(_jnp.square(w)) + 1e-30)
        else:
            s = MOMENT_SCALE[name]
        km, kv = _jax.random.split(_jax.random.fold_in(key, i + 1))
        out[name] = w
        out["m_" + name] = s * _jax.random.normal(km, w.shape, _jnp.float32)
        out["v_" + name] = (s * s) * _jax.random.uniform(kv, w.shape, _jnp.float32, 0.5, 1.5)
    if N_MICROBATCH > 1:
        for name, axis in PER_EXAMPLE_BATCH_AXIS.items():
            out[name] = _to_microbatches(out[name], axis)
    return {'x': out['x'], 'w_in': out['w_in'], 'conv_w': out['conv_w'], 'pool_w': out['pool_w'], 'pool_scale': out['pool_scale'], 'w_out': out['w_out'], 'norm_mix': out['norm_mix'], 'norm_mlp': out['norm_mlp'], 'w_up': out['w_up'], 'w_down': out['w_down'], 'norm_final': out['norm_final'], 'loss_target': out['loss_target'], 'm_w_in': out['m_w_in'], 'm_conv_w': out['m_conv_w'], 'm_pool_w': out['m_pool_w'], 'm_pool_scale': out['m_pool_scale'], 'm_w_out': out['m_w_out'], 'm_norm_mix': out['m_norm_mix'], 'm_norm_mlp': out['m_norm_mlp'], 'm_w_up': out['m_w_up'], 'm_w_down': out['m_w_down'], 'm_norm_final': out['m_norm_final'], 'v_w_in': out['v_w_in'], 'v_conv_w': out['v_conv_w'], 'v_pool_w': out['v_pool_w'], 'v_pool_scale': out['v_pool_scale'], 'v_w_out': out['v_w_out'], 'v_norm_mix': out['v_norm_mix'], 'v_norm_mlp': out['v_norm_mlp'], 'v_w_up': out['v_w_up'], 'v_w_down': out['v_w_down'], 'v_norm_final': out['v_norm_final']}


def _loss(weights, diff, rest, loss_target):
    with _jax.named_scope("forward"):
        args = {**rest, TWIN_DIFF_INPUT: diff, **{k: w.astype(_WEIGHT_DTYPES[k]) for k, w in weights.items()}}
        y = _forward(args)
    with _jax.named_scope("loss_head"):
        err = _jnp.square(y.astype(_jnp.float32) - loss_target)
        return 0.5 * _jnp.sum(_jnp.mean(err, axis=-1)) if err.ndim else 0.5 * err


def _adamw(w, g, m, v):
    m = ADAM_B1 * m + (1.0 - ADAM_B1) * g
    v = ADAM_B2 * v + (1.0 - ADAM_B2) * _jnp.square(g)
    m_hat = m / (1.0 - ADAM_B1 ** ADAM_STEP)
    v_hat = v / (1.0 - ADAM_B2 ** ADAM_STEP)
    delta = -ADAM_LR * (m_hat / (_jnp.sqrt(v_hat) + ADAM_EPS) + ADAM_WD * w)
    return delta, m, v


def reference(x, w_in, conv_w, pool_w, pool_scale, w_out, norm_mix, norm_mlp, w_up, w_down, norm_final, loss_target, m_w_in, m_conv_w, m_pool_w, m_pool_scale, m_w_out, m_norm_mix, m_norm_mlp, m_w_up, m_w_down, m_norm_final, v_w_in, v_conv_w, v_pool_w, v_pool_scale, v_w_out, v_norm_mix, v_norm_mlp, v_w_up, v_w_down, v_norm_final):
    given = dict(x=x, w_in=w_in, conv_w=conv_w, pool_w=pool_w, pool_scale=pool_scale, w_out=w_out, norm_mix=norm_mix, norm_mlp=norm_mlp, w_up=w_up, w_down=w_down, norm_final=norm_final, loss_target=loss_target, m_w_in=m_w_in, m_conv_w=m_conv_w, m_pool_w=m_pool_w, m_pool_scale=m_pool_scale, m_w_out=m_w_out, m_norm_mix=m_norm_mix, m_norm_mlp=m_norm_mlp, m_w_up=m_w_up, m_w_down=m_w_down, m_norm_final=m_norm_final, v_w_in=v_w_in, v_conv_w=v_conv_w, v_pool_w=v_pool_w, v_pool_scale=v_pool_scale, v_w_out=v_w_out, v_norm_mix=v_norm_mix, v_norm_mlp=v_norm_mlp, v_w_up=v_w_up, v_w_down=v_w_down, v_norm_final=v_norm_final)
    weights = {n: given[n] for n in TWIN_WEIGHTS}
    shared = {n: given[n] for n in SHARED_INPUTS}
    per_example = {n: given[n] for n in ['x']}
    grad_fn = _jax.value_and_grad(_loss, argnums=(0, 1))

    def one_microbatch(ex, loss_target):
        ex = dict(ex)
        diff = ex.pop(TWIN_DIFF_INPUT)
        return grad_fn(weights, diff, {**shared, **ex}, loss_target)

    if N_MICROBATCH == 1:
        loss, (grad_w, grad_x) = one_microbatch(per_example, given["loss_target"])
    else:
        def body(carry, xs):
            loss_sum, grad_sum = carry
            l_k, (gw_k, gx_k) = one_microbatch(xs[0], xs[1])
            with _jax.named_scope("update"):
                return (loss_sum + l_k, _jax.tree.map(_jnp.add, grad_sum, gw_k)), gx_k

        init = (_jnp.zeros((), _jnp.float32), _jax.tree.map(_jnp.zeros_like, weights))
        (loss, grad_w), grad_x = _jax.lax.scan(body, init, (per_example, given["loss_target"]))
    with _jax.named_scope("update"):
        delta_w, new_m, new_v = {}, {}, {}
        for n in TWIN_WEIGHTS:
            delta_w[n], new_m[n], new_v[n] = _adamw(weights[n], grad_w[n], given["m_" + n], given["v_" + n])
    return (loss, grad_x, *[grad_w[n] for n in TWIN_WEIGHTS], *[delta_w[n] for n in TWIN_WEIGHTS],
            *[new_m[n] for n in TWIN_WEIGHTS], *[new_v[n] for n in TWIN_WEIGHTS])
```

```python
import functools

import jax
import jax.numpy as jnp
from jax import lax
from jax.experimental import pallas as pl
from jax.experimental.pallas import tpu as pltpu

F32 = jnp.float32
BF16 = jnp.bfloat16

N_DEV = 8
MESH_AXES = ("x", "y", "c")
NORM_EPS = 1e-6
POOL_WINDOWS = (2, 4, 8, 16)
HALO_ROWS = 16

ADAM_LR = 0.001
ADAM_B1 = 0.9
ADAM_B2 = 0.999
ADAM_EPS = 1e-08
ADAM_WD = 0.01
ADAM_STEP = 10

VMEM_BYTES_V7X = 64 * 1024 * 1024
VMEM_LIMIT = (VMEM_BYTES_V7X * 3) // 4

MM_TILE = 1024
MM_TILE_K = 2048
ROW_TILE = 256
MIXER_TILE = 512
ADAM_BLOCK_ELEMS = 64 * 1024


def _params(*semantics):
    return pltpu.CompilerParams(dimension_semantics=semantics, vmem_limit_bytes=VMEM_LIMIT)


def _tile(dim, pref):
    t = min(dim, pref)
    assert dim % t == 0, (dim, pref)
    return t


def _mesh_position():
    return lax.axis_index("x"), lax.axis_index("y"), lax.axis_index("c")


def _slot(p):
    return 4 * p[0] + 2 * p[1] + p[2]


def _all_gather(arrs, name):
    n = len(arrs)

    def body(*refs):
        ins, outs = refs[:n], refs[n:2 * n]
        send_sems, recv_sems, local_sems = refs[2 * n:]
        x, y, c = _mesh_position()
        me, sibling = (x, y, c), (x, y, 1 - c)
        chips = [(1 - x, y), (x, 1 - y), (1 - x, 1 - y)]

        def copy(a, k, block, to, src=None):
            dst = outs[a].at[_slot(block)]
            return pltpu.make_async_remote_copy(
                src_ref=dst if src is None else src, dst_ref=dst,
                send_sem=send_sems.at[a, k], recv_sem=recv_sems.at[a, k],
                device_id=to, device_id_type=pl.DeviceIdType.MESH)

        mine = [pltpu.make_async_copy(ins[a], outs[a].at[_slot(me)], local_sems.at[a]) for a in range(n)]
        for cp in mine:
            cp.start()
        first = []
        for a in range(n):
            first.append(copy(a, 0, me, sibling, src=ins[a]))
            first += [copy(a, 1 + j, me, (*chip, c), src=ins[a]) for j, chip in enumerate(chips)]
        for cp in first:
            cp.start()
        passed = []
        for j, chip in enumerate(chips):
            for a in range(n):
                copy(a, 1 + j, (*chip, c), me).wait_recv()
                cp = copy(a, 4 + j, (*chip, c), sibling)
                cp.start()
                passed.append(cp)
        for a in range(n):
            copy(a, 0, sibling, me).wait_recv()
            for j, chip in enumerate(chips):
                copy(a, 4 + j, (*chip, 1 - c), me).wait_recv()
        for cp in first + passed:
            cp.wait_send()
        for cp in mine:
            cp.wait()

    any_spec = pl.BlockSpec(memory_space=pl.ANY)
    return pl.pallas_call(
        body, name=name,
        out_shape=[jax.ShapeDtypeStruct((N_DEV, *a.shape), a.dtype) for a in arrs],
        in_specs=[any_spec] * n, out_specs=[any_spec] * n,
        scratch_shapes=[pltpu.SemaphoreType.DMA((n, 7)), pltpu.SemaphoreType.DMA((n, 7)),
                        pltpu.SemaphoreType.DMA((n,))],
    )(*arrs)


def _all_to_all(arrs, name):
    n = len(arrs)

    def body(*refs):
        ins, outs = refs[:n], refs[n:2 * n]
        send_sems, recv_sems, local_sems = refs[2 * n:]
        x, y, c = _mesh_position()
        me = (x, y, c)

        def peer(k):
            return (1 - x if k & 4 else x, 1 - y if k & 2 else y, 1 - c if k & 1 else c)

        def copy(a, k):
            return pltpu.make_async_remote_copy(
                src_ref=ins[a].at[_slot(peer(k))], dst_ref=outs[a].at[_slot(me)],
                send_sem=send_sems.at[a, k - 1], recv_sem=recv_sems.at[a, k - 1],
                device_id=peer(k), device_id_type=pl.DeviceIdType.MESH)

        def landing(a, k):
            return pltpu.make_async_remote_copy(
                src_ref=ins[a].at[_slot(peer(k))], dst_ref=outs[a].at[_slot(peer(k))],
                send_sem=send_sems.at[a, k - 1], recv_sem=recv_sems.at[a, k - 1],
                device_id=peer(k), device_id_type=pl.DeviceIdType.MESH)

        mine = [pltpu.make_async_copy(ins[a].at[_slot(me)], outs[a].at[_slot(me)], local_sems.at[a])
                for a in range(n)]
        for cp in mine:
            cp.start()
        sends = [copy(a, k) for k in range(1, N_DEV) for a in range(n)]
        for cp in sends:
            cp.start()
        for k in range(1, N_DEV):
            for a in range(n):
                landing(a, k).wait_recv()
        for cp in sends:
            cp.wait_send()
        for cp in mine:
            cp.wait()

    any_spec = pl.BlockSpec(memory_space=pl.ANY)
    return pl.pallas_call(
        body, name=name,
        out_shape=[jax.ShapeDtypeStruct(a.shape, a.dtype) for a in arrs],
        in_specs=[any_spec] * n, out_specs=[any_spec] * n,
        scratch_shapes=[pltpu.SemaphoreType.DMA((n, 7)), pltpu.SemaphoreType.DMA((n, 7)),
                        pltpu.SemaphoreType.DMA((n,))],
    )(*arrs)


def _relu2(a):
    r = jnp.maximum(a.astype(F32), 0.0)
    return (r * r).astype(BF16)


def _accumulate(acc_ref, p, k, nk, finish):
    if nk == 1:
        finish(p)
        return

    @pl.when(k == 0)
    def _():
        acc_ref[...] = p

    @pl.when(k > 0)
    def _():
        acc_ref[...] += p

    @pl.when(k == nk - 1)
    def _():
        finish(acc_ref[...])


def _mm_nn(a, b3, *, out_dtype, name, relu2_lhs=False, res=None):
    m, kdim = a.shape
    nb, kb_, nw = b3.shape
    assert kb_ == kdim
    tm, tn, tk = _tile(m, MM_TILE), _tile(nw, MM_TILE), _tile(kdim, MM_TILE_K if not relu2_lhs else MM_TILE)
    per_block = nw // tn
    nk = kdim // tk

    def body(*refs):
        if res is None:
            a_ref, b_ref, o_ref = refs[:3]
            r_ref = None
        else:
            a_ref, b_ref, r_ref, o_ref = refs[:4]
        acc_ref = refs[-1] if nk > 1 else None
        av = a_ref[...]
        if relu2_lhs:
            av = _relu2(av)
        p = jnp.dot(av, b_ref[...], preferred_element_type=F32)

        def finish(total):
            if r_ref is not None:
                total = total + r_ref[...]
            o_ref[...] = total.astype(out_dtype)

        _accumulate(acc_ref, p, pl.program_id(2), nk, finish)

    in_specs = [pl.BlockSpec((tm, tk), lambda i, j, k: (i, k)),
                pl.BlockSpec((None, tk, tn), lambda i, j, k: (j // per_block, k, j % per_block))]
    operands = [a, b3]
    if res is not None:
        in_specs.append(pl.BlockSpec((tm, tn), lambda i, j, k: (i, j)))
        operands.append(res)
    return pl.pallas_call(
        body, name=name, grid=(m // tm, (nb * nw) // tn, nk),
        out_shape=jax.ShapeDtypeStruct((m, nb * nw), out_dtype),
        in_specs=in_specs, out_specs=pl.BlockSpec((tm, tn), lambda i, j, k: (i, j)),
        scratch_shapes=[pltpu.VMEM((tm, tn), F32)] if nk > 1 else [],
        compiler_params=_params("parallel", "parallel", "arbitrary"),
    )(*operands)


def _mm_nt(a, b3, *, out_dtype, name, relu2_grad_of=None):
    m, kdim = a.shape
    kb, n, kw = b3.shape
    assert kb * kw == kdim
    tm, tn, tk = _tile(m, MM_TILE), _tile(n, MM_TILE), _tile(kw, MM_TILE_K)
    per_block = kw // tk
    nk = kdim // tk

    def body(*refs):
        if relu2_grad_of is None:
            a_ref, b_ref, o_ref = refs[:3]
            g_ref = None
        else:
            a_ref, b_ref, g_ref, o_ref = refs[:4]
        acc_ref = refs[-1] if nk > 1 else None
        p = lax.dot_general(a_ref[...], b_ref[...], (((1,), (1,)), ((), ())), preferred_element_type=F32)

        def finish(total):
            if g_ref is not None:
                total = total * (2.0 * jnp.maximum(g_ref[...].astype(F32), 0.0))
            o_ref[...] = total.astype(out_dtype)

        _accumulate(acc_ref, p, pl.program_id(2), nk, finish)

    in_specs = [pl.BlockSpec((tm, tk), lambda i, j, k: (i, k)),
                pl.BlockSpec((None, tn, tk), lambda i, j, k: (k // per_block, j, k % per_block))]
    operands = [a, b3]
    if relu2_grad_of is not None:
        in_specs.append(pl.BlockSpec((tm, tn), lambda i, j, k: (i, j)))
        operands.append(relu2_grad_of)
    return pl.pallas_call(
        body, name=name, grid=(m // tm, n // tn, nk),
        out_shape=jax.ShapeDtypeStruct((m, n), out_dtype),
        in_specs=in_specs, out_specs=pl.BlockSpec((tm, tn), lambda i, j, k: (i, j)),
        scratch_shapes=[pltpu.VMEM((tm, tn), F32)] if nk > 1 else [],
        compiler_params=_params("parallel", "parallel", "arbitrary"),
    )(*operands)


def _mm_tn(a, b, *, n_blocks, name, relu2_lhs=False):
    t, m = a.shape
    t2, n = b.shape
    assert t == t2 and n % n_blocks == 0
    nw = n // n_blocks
    tm, tn, tk = _tile(m, MM_TILE), _tile(nw, MM_TILE), _tile(t, MM_TILE)
    per_block = nw // tn
    nk = t // tk

    def body(a_ref, b_ref, o_ref, *scratch):
        acc_ref = scratch[0] if nk > 1 else None
        av = a_ref[...]
        if relu2_lhs:
            av = _relu2(av)
        p = lax.dot_general(av, b_ref[...], (((0,), (0,)), ((), ())), preferred_element_type=F32)

        def finish(total):
            o_ref[...] = total.astype(BF16)

        _accumulate(acc_ref, p, pl.program_id(2), nk, finish)

    return pl.pallas_call(
        body, name=name, grid=(m // tm, n // tn, nk),
        out_shape=jax.ShapeDtypeStruct((n_blocks, m, nw), BF16),
        in_specs=[pl.BlockSpec((tk, tm), lambda i, j, k: (k, i)),
                  pl.BlockSpec((tk, tn), lambda i, j, k: (k, j))],
        out_specs=pl.BlockSpec((None, tm, tn), lambda i, j, k: (j // per_block, i, j % per_block)),
        scratch_shapes=[pltpu.VMEM((tm, tn), F32)] if nk > 1 else [],
        compiler_params=_params("parallel", "parallel", "arbitrary"),
    )(a, b)


def _normalise(x):
    r = lax.rsqrt(jnp.mean(x * x, axis=-1, keepdims=True) + NORM_EPS)
    return x * r, r


def _rmsnorm_backward(dh, xhat, r, gain):
    dxhat = dh * gain
    return r * (dxhat - xhat * jnp.mean(dxhat * xhat, axis=-1, keepdims=True))


def _rmsnorm(x, gain, name):
    t, d = x.shape
    tr = _tile(t, ROW_TILE)

    def body(x_ref, g_ref, o_ref):
        xhat, _ = _normalise(x_ref[...])
        o_ref[...] = (xhat * g_ref[...]).astype(BF16)

    return pl.pallas_call(
        body, name=name, grid=(t // tr,),
        out_shape=jax.ShapeDtypeStruct((t, d), BF16),
        in_specs=[pl.BlockSpec((tr, d), lambda i: (i, 0)), pl.BlockSpec((1, d), lambda i: (0, 0))],
        out_specs=pl.BlockSpec((tr, d), lambda i: (i, 0)),
        compiler_params=_params("parallel"),
    )(x, gain.reshape(1, d))


def _rmsnorm_bwd(dh, x, gain, dres, name):
    t, d = x.shape
    tr = _tile(t, ROW_TILE)

    def body(dh_ref, x_ref, g_ref, r_ref, dx_ref, dxb_ref, dg_ref):
        xhat, r = _normalise(x_ref[...])
        dh_v = dh_ref[...]
        dx = r_ref[...] + _rmsnorm_backward(dh_v, xhat, r, g_ref[...])
        dx_ref[...] = dx
        dxb_ref[...] = dx.astype(BF16)
        part = jnp.sum(dh_v * xhat, axis=0, keepdims=True)

        @pl.when(pl.program_id(0) == 0)
        def _():
            dg_ref[...] = part

        @pl.when(pl.program_id(0) > 0)
        def _():
            dg_ref[...] += part

    row = pl.BlockSpec((tr, d), lambda i: (i, 0))
    vec = pl.BlockSpec((1, d), lambda i: (0, 0))
    return pl.pallas_call(
        body, name=name, grid=(t // tr,),
        out_shape=[jax.ShapeDtypeStruct((t, d), F32), jax.ShapeDtypeStruct((t, d), BF16),
                   jax.ShapeDtypeStruct((1, d), F32)],
        in_specs=[row, row, vec, row], out_specs=[row, row, vec],
        compiler_params=_params("arbitrary"),
    )(dh, x, gain.reshape(1, d), dres)


def _loss_head(x, gain, target, name):
    t, d = x.shape
    tr = _tile(t, ROW_TILE)
    steps = t // tr

    def body(x_ref, g_ref, t_ref, loss_ref, dx_ref, dxb_ref, dg_ref, sq_ref):
        i = pl.program_id(0)
        xhat, r = _normalise(x_ref[...])
        gain_v = g_ref[...]
        diff = xhat * gain_v - t_ref[...]
        dy = diff / float(d)
        dx = _rmsnorm_backward(dy, xhat, r, gain_v)
        dx_ref[...] = dx
        dxb_ref[...] = dx.astype(BF16)
        dg_part = jnp.sum(dy * xhat, axis=0, keepdims=True)
        sq_part = jnp.sum(diff * diff, axis=0, keepdims=True)

        @pl.when(i == 0)
        def _():
            dg_ref[...] = dg_part
            sq_ref[...] = sq_part

        @pl.when(i > 0)
        def _():
            dg_ref[...] += dg_part
            sq_ref[...] += sq_part

        @pl.when(i == steps - 1)
        def _():
            loss_ref[...] = (0.5 / float(d)) * jnp.sum(sq_ref[...], axis=1, keepdims=True)

    row = pl.BlockSpec((tr, d), lambda i: (i, 0))
    vec = pl.BlockSpec((1, d), lambda i: (0, 0))
    return pl.pallas_call(
        body, name=name, grid=(steps,),
        out_shape=[jax.ShapeDtypeStruct((1, 1), F32), jax.ShapeDtypeStruct((t, d), F32),
                   jax.ShapeDtypeStruct((t, d), BF16), jax.ShapeDtypeStruct((1, d), F32)],
        in_specs=[row, vec, row],
        out_specs=[pl.BlockSpec((1, 1), lambda i: (0, 0)), row, row, vec],
        scratch_shapes=[pltpu.VMEM((1, d), F32)],
        compiler_params=_params("arbitrary"),
    )(x, gain.reshape(1, d), target)


def _shift_down(ext, s):
    return pltpu.roll(ext, s, 0)


def _shift_up(ext, s):
    return pltpu.roll(ext, ext.shape[0] - s, 0)


def _window_sum(ext, w, shift):
    s = 1
    while s < w:
        ext = ext + shift(ext, s)
        s *= 2
    return ext


def _window_count(tile_index, rows, cols, w):
    t = tile_index * rows + lax.broadcasted_iota(jnp.int32, (rows, cols), 0)
    return jnp.minimum(t + 1, w).astype(F32)


def _mixer_sizes(proj, conv_w):
    t, e = proj.shape
    dc = conv_w.shape[1]
    dp = e - 3 * dc
    cg = dp // len(POOL_WINDOWS)
    tt = _tile(t, MIXER_TILE)
    assert tt % HALO_ROWS == 0 and tt >= HALO_ROWS
    cw = _tile(dc, cg)
    return t, e, dc, dp, cg, tt, cw


def _mixer_fwd(proj, conv_w, pool_w, pool_scale, name):
    t, e, dc, dp, cg, tt, cw = _mixer_sizes(proj, conv_w)
    per_halo = tt // HALO_ROWS

    def body(cur_ref, prev_ref, cw_ref, pw_ref, ps_ref, y_ref):
        i = pl.program_id(0)
        first = i == 0

        def cur(lo, width):
            return cur_ref[:, lo:lo + width].astype(F32)

        def prev(lo, width):
            return jnp.where(first, 0.0, prev_ref[:, lo:lo + width].astype(F32))

        for lo in range(0, dc, cw):
            u = cur(dc + lo, cw) * cur(2 * dc + lo, cw)
            ext = jnp.concatenate([prev(dc + lo, cw) * prev(2 * dc + lo, cw), u], axis=0)
            u1 = _shift_down(ext, 1)[HALO_ROWS:]
            u2 = _shift_down(ext, 2)[HALO_ROWS:]
            conv = cw_ref[0:1, lo:lo + cw] * u2 + cw_ref[1:2, lo:lo + cw] * u1 + cw_ref[2:3, lo:lo + cw] * u
            y_ref[:, lo:lo + cw] = (cur(lo, cw) * conv).astype(BF16)

        for g, w in enumerate(POOL_WINDOWS):
            lo = 3 * dc + g * cg
            v = cur(lo, cg)
            ext = jnp.concatenate([prev(lo, cg), v], axis=0)
            mean = _window_sum(ext, w, _shift_down)[HALO_ROWS:] / _window_count(i, tt, cg, w)
            z = jnp.dot((mean - v).astype(BF16), pw_ref[g], preferred_element_type=F32)
            y_ref[:, dc + g * cg:dc + (g + 1) * cg] = (z * ps_ref[0:1, g * cg:(g + 1) * cg]).astype(BF16)

    return pl.pallas_call(
        body, name=name, grid=(t // tt,),
        out_shape=jax.ShapeDtypeStruct((t, dc + dp), BF16),
        in_specs=[pl.BlockSpec((tt, e), lambda i: (i, 0)),
                  pl.BlockSpec((HALO_ROWS, e), lambda i: (jnp.maximum(i * per_halo - 1, 0), 0)),
                  pl.BlockSpec((3, dc), lambda i: (0, 0)),
                  pl.BlockSpec((len(POOL_WINDOWS), cg, cg), lambda i: (0, 0, 0)),
                  pl.BlockSpec((1, dp), lambda i: (0, 0))],
        out_specs=pl.BlockSpec((tt, dc + dp), lambda i: (i, 0)),
        compiler_params=_params("parallel"),
    )(proj, proj, conv_w, pool_w, pool_scale.reshape(1, dp))


def _mixer_bwd(proj, dy, conv_w, pool_w, pool_scale, name):
    t, e, dc, dp, cg, tt, cw = _mixer_sizes(proj, conv_w)
    per_halo = tt // HALO_ROWS
    steps = t // tt
    n_groups = len(POOL_WINDOWS)

    def body(cur_ref, prev_ref, next_ref, dy_ref, dyn_ref, cw_ref, pw_ref, ps_ref,
             dp_ref, dcw_ref, dpw_ref, dps_ref):
        i = pl.program_id(0)
        first = i == 0
        last = i == steps - 1

        @pl.when(first)
        def _():
            dcw_ref[...] = jnp.zeros_like(dcw_ref)
            dpw_ref[...] = jnp.zeros_like(dpw_ref)
            dps_ref[...] = jnp.zeros_like(dps_ref)

        def cur(lo, width):
            return cur_ref[:, lo:lo + width].astype(F32)

        def prev(lo, width):
            return jnp.where(first, 0.0, prev_ref[:, lo:lo + width].astype(F32))

        def nxt(ref, lo, width):
            return jnp.where(last, 0.0, ref[:, lo:lo + width].astype(F32))

        def colsum(v):
            return jnp.sum(v, axis=0, keepdims=True)

        for lo in range(0, dc, cw):
            cols = slice(lo, lo + cw)
            b, c, xt = cur(lo, cw), cur(dc + lo, cw), cur(2 * dc + lo, cw)
            u = c * xt
            ext = jnp.concatenate([prev(dc + lo, cw) * prev(2 * dc + lo, cw), u], axis=0)
            u1 = _shift_down(ext, 1)[HALO_ROWS:]
            u2 = _shift_down(ext, 2)[HALO_ROWS:]
            w0, w1, w2 = cw_ref[0:1, cols], cw_ref[1:2, cols], cw_ref[2:3, cols]
            dyc = dy_ref[:, cols].astype(F32)
            dp_ref[:, cols] = (dyc * (w0 * u2 + w1 * u1 + w2 * u)).astype(BF16)
            dconv = dyc * b
            dcw_ref[0:1, cols] += colsum(dconv * u2)
            dcw_ref[1:2, cols] += colsum(dconv * u1)
            dcw_ref[2:3, cols] += colsum(dconv * u)
            dext = jnp.concatenate([dconv, nxt(dyn_ref, lo, cw) * nxt(next_ref, lo, cw)], axis=0)
            du = w2 * dconv + w1 * _shift_up(dext, 1)[:tt] + w0 * _shift_up(dext, 2)[:tt]
            dp_ref[:, dc + lo:dc + lo + cw] = (du * xt).astype(BF16)
            dp_ref[:, 2 * dc + lo:2 * dc + lo + cw] = (du * c).astype(BF16)

        for g, w in enumerate(POOL_WINDOWS):
            lo = 3 * dc + g * cg
            ycols = slice(dc + g * cg, dc + (g + 1) * cg)
            pcols = slice(g * cg, (g + 1) * cg)
            v = cur(lo, cg)
            ext = jnp.concatenate([prev(lo, cg), v], axis=0)
            count = _window_count(i, tt, cg, w)
            d = ((_window_sum(ext, w, _shift_down)[HALO_ROWS:] / count) - v).astype(BF16)
            pw = pw_ref[g]
            scale = ps_ref[0:1, pcols]
            dyp = dy_ref[:, ycols].astype(F32)
            z = jnp.dot(d, pw, preferred_element_type=F32)
            dps_ref[0:1, pcols] += colsum(dyp * z)
            dz = (dyp * scale).astype(BF16)
            dpw_ref[g] += lax.dot_general(d, dz, (((0,), (0,)), ((), ())), preferred_element_type=F32)
            dd = lax.dot_general(dz, pw, (((1,), (1,)), ((), ())), preferred_element_type=F32)
            dzn = (nxt(dyn_ref, dc + g * cg, cg) * scale).astype(BF16)
            ddn = lax.dot_general(dzn, pw, (((1,), (1,)), ((), ())), preferred_element_type=F32)
            qext = jnp.concatenate([dd / count, ddn / float(w)], axis=0)
            dp_ref[:, lo:lo + cg] = (_window_sum(qext, w, _shift_up)[:tt] - dd).astype(BF16)

    cur_spec = lambda width: pl.BlockSpec((tt, width), lambda i: (i, 0))
    prev_spec = pl.BlockSpec((HALO_ROWS, e), lambda i: (jnp.maximum(i * per_halo - 1, 0), 0))
    next_spec = lambda width: pl.BlockSpec(
        (HALO_ROWS, width), lambda i: (jnp.minimum((i + 1) * per_halo, t // HALO_ROWS - 1), 0))
    return pl.pallas_call(
        body, name=name, grid=(steps,),
        out_shape=[jax.ShapeDtypeStruct((t, e), BF16), jax.ShapeDtypeStruct((3, dc), F32),
                   jax.ShapeDtypeStruct((n_groups, cg, cg), F32), jax.ShapeDtypeStruct((1, dp), F32)],
        in_specs=[cur_spec(e), prev_spec, next_spec(e), cur_spec(dc + dp), next_spec(dc + dp),
                  pl.BlockSpec((3, dc), lambda i: (0, 0)),
                  pl.BlockSpec((n_groups, cg, cg), lambda i: (0, 0, 0)),
                  pl.BlockSpec((1, dp), lambda i: (0, 0))],
        out_specs=[cur_spec(e), pl.BlockSpec((3, dc), lambda i: (0, 0)),
                   pl.BlockSpec((n_groups, cg, cg), lambda i: (0, 0, 0)),
                   pl.BlockSpec((1, dp), lambda i: (0, 0))],
        compiler_params=_params("arbitrary"),
    )(proj, proj, proj, dy, dy, conv_w, pool_w, pool_scale.reshape(1, dp))


def _adamw(partials, w, m, v, name):
    n_layers, r, c = w.shape
    assert len(partials) == n_layers
    n_slots = partials[0].shape[0]
    tr = r if r * c <= ADAM_BLOCK_ELEMS else _tile(r, max(16, ADAM_BLOCK_ELEMS // c))

    def body(*refs):
        p_refs = refs[:n_layers]
        w_ref, m_ref, v_ref, g_out, d_out, m_out, v_out = refs[n_layers:]
        for l in range(n_layers):
            g = p_refs[l][0].astype(F32)
            for s in range(1, n_slots):
                g = g + p_refs[l][s].astype(F32)
            m_new = ADAM_B1 * m_ref[l] + (1.0 - ADAM_B1) * g
            v_new = ADAM_B2 * v_ref[l] + (1.0 - ADAM_B2) * (g * g)
            m_hat = m_new / (1.0 - ADAM_B1 ** ADAM_STEP)
            v_hat = v_new / (1.0 - ADAM_B2 ** ADAM_STEP)
            g_out[l] = g
            d_out[l] = -ADAM_LR * (m_hat / (jnp.sqrt(v_hat) + ADAM_EPS) + ADAM_WD * w_ref[l])
            m_out[l] = m_new
            v_out[l] = v_new

    p_spec = pl.BlockSpec((n_slots, tr, c), lambda i: (0, i, 0))
    w_spec = pl.BlockSpec((n_layers, tr, c), lambda i: (0, i, 0))
    return pl.pallas_call(
        body, name=name, grid=(r // tr,),
        out_shape=[jax.ShapeDtypeStruct(w.shape, F32)] * 4,
        in_specs=[p_spec] * n_layers + [w_spec] * 3, out_specs=[w_spec] * 4,
        compiler_params=_params("parallel"),
    )(*partials, w, m, v)


def kernel(x, w_in, conv_w, pool_w, pool_scale, w_out, norm_mix, norm_mlp, w_up, w_down, norm_final, loss_target, m_w_in, m_conv_w, m_pool_w, m_pool_scale, m_w_out, m_norm_mix, m_norm_mlp, m_w_up, m_w_down, m_norm_final, v_w_in, v_conv_w, v_pool_w, v_pool_scale, v_w_out, v_norm_mix, v_norm_mlp, v_w_up, v_w_down, v_norm_final):
    n_layers, d, e_shard = w_in.shape
    t = x.shape[1]
    n_groups, cg_shard, cg = pool_w.shape[1:]
    dc_shard = conv_w.shape[2]
    dc, dp = dc_shard * N_DEV, n_groups * cg
    f_shard = w_up.shape[2]
    xs = x.reshape(t, d)
    target = loss_target.reshape(t, d)

    conv_g, pool_g = _all_gather(
        [conv_w.reshape(n_layers * 3, dc_shard), pool_w.reshape(n_layers * n_groups * cg_shard, cg)], "gather_mixer_weights")
    conv_full = conv_g.reshape(N_DEV, n_layers, 3, dc_shard).transpose(1, 2, 0, 3).reshape(n_layers, 3, dc)
    pool_full = pool_g.reshape(N_DEV, n_layers, n_groups, cg_shard, cg).transpose(1, 2, 0, 3, 4)
    pool_full = pool_full.reshape(n_layers, n_groups, cg, cg).astype(BF16)

    weights = []
    for l in range(n_layers):
        win, wout, wup, wdown = _all_gather(
            [w_in[l].astype(BF16), w_out[l].astype(BF16), w_up[l].astype(BF16), w_down[l].astype(BF16)], "gather_layer_weights")
        weights.append((win, wout.reshape(1, d, d), wup, wdown.reshape(1, f_shard * N_DEV, d)))

    saved = []
    xc = xs
    for l in range(n_layers):
        win, wout, wup, wdown = weights[l]
        h1 = _rmsnorm(xc, norm_mix[l], "norm_mix")
        proj = _mm_nn(h1, win, out_dtype=BF16, name="in_proj")
        y = _mixer_fwd(proj, conv_full[l], pool_full[l], pool_scale[l], "mixer_fwd")
        x1 = _mm_nn(y, wout, out_dtype=F32, res=xc, name="out_proj")
        h2 = _rmsnorm(x1, norm_mlp[l], "norm_mlp")
        a = _mm_nn(h2, wup, out_dtype=BF16, name="mlp_up")
        x2 = _mm_nn(a, wdown, out_dtype=F32, relu2_lhs=True, res=x1, name="mlp_down")
        saved.append((xc, h1, proj, y, x1, h2, a))
        xc = x2

    loss_part, dx, dxb, g_norm_final = _loss_head(xc, norm_final, target, "loss_head")
    loss = lax.psum(loss_part[0, 0], MESH_AXES)

    received = [None] * n_layers
    g_conv, g_scale, g_mix, g_mlp = ([None] * n_layers for _ in range(4))
    for l in reversed(range(n_layers)):
        win, wout, wup, wdown = weights[l]
        x0, h1, proj, y, x1, h2, a = saved[l]
        da = _mm_nt(dxb, wdown, out_dtype=BF16, relu2_grad_of=a, name="mlp_down_dx")
        gw_down = _mm_tn(a, dxb, n_blocks=1, relu2_lhs=True, name="mlp_down_dw")
        gw_up = _mm_tn(h2, da, n_blocks=N_DEV, name="mlp_up_dw")
        dh2 = _mm_nt(da, wup, out_dtype=F32, name="mlp_up_dx")
        dx1, dx1b, g_mlp[l] = _rmsnorm_bwd(dh2, x1, norm_mlp[l], dx, "norm_mlp_bwd")
        dy = _mm_nt(dx1b, wout, out_dtype=BF16, name="out_proj_dx")
        gw_out = _mm_tn(y, dx1b, n_blocks=1, name="out_proj_dw")
        dproj, g_conv[l], gw_pool, g_scale[l] = _mixer_bwd(proj, dy, conv_full[l], pool_full[l], pool_scale[l], "mixer_bwd")
        gw_in = _mm_tn(h1, dproj, n_blocks=N_DEV, name="in_proj_dw")
        dh1 = _mm_nt(dproj, win, out_dtype=F32, name="in_proj_dx")
        dx, dxb, g_mix[l] = _rmsnorm_bwd(dh1, x0, norm_mix[l], dx1, "norm_mix_bwd")
        gw_pool = gw_pool.reshape(n_groups, N_DEV, cg_shard, cg).transpose(1, 0, 2, 3)
        received[l] = _all_to_all(
            [gw_in, gw_out.reshape(N_DEV, d // N_DEV, d), gw_up, gw_down.reshape(N_DEV, f_shard, d),
             gw_pool.reshape(N_DEV, n_groups * cg_shard, cg).astype(BF16)], "exchange_layer_grads")
    grad_x = dx.reshape(x.shape)

    small = _all_gather(
        [jnp.stack(g_conv).reshape(n_layers * 3, dc), jnp.concatenate(g_scale, axis=0),
         jnp.concatenate(g_mix, axis=0), jnp.concatenate(g_mlp, axis=0), g_norm_final], "gather_small_grads")
    me = _slot(_mesh_position())
    conv_parts = lax.dynamic_slice_in_dim(small[0], me * dc_shard, dc_shard, axis=2)

    def update(partials, w, m, v, name):
        shape = w.shape
        rc = (shape[0], -1, shape[-1]) if w.ndim > 2 else (1, *shape) if w.ndim == 2 else (1, 1, *shape)
        outs = _adamw(partials, w.reshape(rc), m.reshape(rc), v.reshape(rc), name)
        return [o.reshape(shape) for o in outs]

    results = {
        "w_in": update([r[0] for r in received], w_in, m_w_in, v_w_in, "adamw_w_in"),
        "conv_w": update([conv_parts], conv_w.reshape(n_layers * 3, dc_shard), m_conv_w.reshape(n_layers * 3, dc_shard),
                         v_conv_w.reshape(n_layers * 3, dc_shard), "adamw_conv_w"),
        "pool_w": update([r[4] for r in received], pool_w, m_pool_w, v_pool_w, "adamw_pool_w"),
        "pool_scale": update([small[1]], pool_scale, m_pool_scale, v_pool_scale, "adamw_pool_scale"),
        "w_out": update([r[1] for r in received], w_out, m_w_out, v_w_out, "adamw_w_out"),
        "norm_mix": update([small[2]], norm_mix, m_norm_mix, v_norm_mix, "adamw_norm_mix"),
        "norm_mlp": update([small[3]], norm_mlp, m_norm_mlp, v_norm_mlp, "adamw_norm_mlp"),
        "w_up": update([r[2] for r in received], w_up, m_w_up, v_w_up, "adamw_w_up"),
        "w_down": update([r[3] for r in received], w_down, m_w_down, v_w_down, "adamw_w_down"),
        "norm_final": update([small[4]], norm_final, m_norm_final, v_norm_final, "adamw_norm_final"),
    }
    results["conv_w"] = [o.reshape(conv_w.shape) for o in results["conv_w"]]
    order = ("w_in", "conv_w", "pool_w", "pool_scale", "w_out", "norm_mix", "norm_mlp", "w_up", "w_down", "norm_final")
    return (loss, grad_x, *[results[k][0] for k in order], *[results[k][1] for k in order],
            *[results[k][2] for k in order], *[results[k][3] for k in order])
```

```python
import functools

import jax
import jax.numpy as jnp
from jax import lax
from jax.experimental import pallas as pl
from jax.experimental.pallas import tpu as pltpu

F32 = jnp.float32
BF16 = jnp.bfloat16

N_DEV = 8
MESH_AXES = ("x", "y", "c")
NORM_EPS = 1e-6
POOL_WINDOWS = (2, 4, 8, 16)
HALO_ROWS = 16

ADAM_LR = 0.001
ADAM_B1 = 0.9
ADAM_B2 = 0.999
ADAM_EPS = 1e-08
ADAM_WD = 0.01
ADAM_STEP = 10

VMEM_BYTES_V7X = 64 * 1024 * 1024
VMEM_LIMIT = (VMEM_BYTES_V7X * 3) // 4

MM_TILE = 1024
MM_TILE_K = 2048
ROW_TILE = 256
MIXER_TILE = 512
ADAM_BLOCK_ELEMS = 64 * 1024


def _params(*semantics):
    return pltpu.CompilerParams(dimension_semantics=semantics, vmem_limit_bytes=VMEM_LIMIT)


def _tile(dim, pref):
    t = min(dim, pref)
    assert dim % t == 0, (dim, pref)
    return t


def _mesh_position():
    return lax.axis_index("x"), lax.axis_index("y"), lax.axis_index("c")


def _slot(p):
    return 4 * p[0] + 2 * p[1] + p[2]


def _all_gather(arrs, name):
    n = len(arrs)

    def body(*refs):
        ins, outs = refs[:n], refs[n:2 * n]
        send_sems, recv_sems, local_sems = refs[2 * n:]
        x, y, c = _mesh_position()
        me, sibling = (x, y, c), (x, y, 1 - c)
        chips = [(1 - x, y), (x, 1 - y), (1 - x, 1 - y)]

        def copy(a, k, block, to, src=None):
            dst = outs[a].at[_slot(block)]
            return pltpu.make_async_remote_copy(
                src_ref=dst if src is None else src, dst_ref=dst,
                send_sem=send_sems.at[a, k], recv_sem=recv_sems.at[a, k],
                device_id=to, device_id_type=pl.DeviceIdType.MESH)

        mine = [pltpu.make_async_copy(ins[a], outs[a].at[_slot(me)], local_sems.at[a]) for a in range(n)]
        for cp in mine:
            cp.start()
        first = []
        for a in range(n):
            first.append(copy(a, 0, me, sibling, src=ins[a]))
            first += [copy(a, 1 + j, me, (*chip, c), src=ins[a]) for j, chip in enumerate(chips)]
        for cp in first:
            cp.start()
        passed = []
        for j, chip in enumerate(chips):
            for a in range(n):
                copy(a, 1 + j, (*chip, c), me).wait_recv()
                cp = copy(a, 4 + j, (*chip, c), sibling)
                cp.start()
                passed.append(cp)
        for a in range(n):
            copy(a, 0, sibling, me).wait_recv()
            for j, chip in enumerate(chips):
                copy(a, 4 + j, (*chip, 1 - c), me).wait_recv()
        for cp in first + passed:
            cp.wait_send()
        for cp in mine:
            cp.wait()

    any_spec = pl.BlockSpec(memory_space=pl.ANY)
    return pl.pallas_call(
        body, name=name,
        out_shape=[jax.ShapeDtypeStruct((N_DEV, *a.shape), a.dtype) for a in arrs],
        in_specs=[any_spec] * n, out_specs=[any_spec] * n,
        scratch_shapes=[pltpu.SemaphoreType.DMA((n, 7)), pltpu.SemaphoreType.DMA((n, 7)),
                        pltpu.SemaphoreType.DMA((n,))],
    )(*arrs)


def _peer(k):
    x, y, c = _mesh_position()
    return (1 - x if k & 4 else x, 1 - y if k & 2 else y, 1 - c if k & 1 else c)


_HBM = pl.BlockSpec(memory_space=pltpu.HBM)
_SEM = pl.BlockSpec(memory_space=pltpu.SEMAPHORE)
_EFFECT = pltpu.SideEffectType.DATAFLOW_SIDE_EFFECTING


def _exchange_copy(g_ref, land_ref, send_sems, recv_sems, a, k):
    return pltpu.make_async_remote_copy(
        src_ref=g_ref.at[_slot(_peer(k))], dst_ref=land_ref.at[k - 1],
        send_sem=send_sems.at[a * (N_DEV - 1) + k - 1], recv_sem=recv_sems.at[a * (N_DEV - 1) + k - 1],
        device_id=_peer(k), device_id_type=pl.DeviceIdType.MESH)


def _exchange_start(arrs, name):
    n = len(arrs)

    def body(*refs):
        g_refs, land_refs = refs[:n], refs[n:2 * n]
        send_sems, recv_sems = refs[2 * n:2 * n + 2]
        token = refs[-1]
        for k in range(1, N_DEV):
            for a in range(n):
                _exchange_copy(g_refs[a], land_refs[a], send_sems, recv_sems, a, k).start()
        token[...] = jnp.zeros_like(token)

    lands = [lax.empty((N_DEV - 1, *g.shape[1:]), g.dtype) for g in arrs]
    outs = pl.pallas_call(
        body, name=name,
        out_shape=(pltpu.SemaphoreType.DMA((n * (N_DEV - 1),)), pltpu.SemaphoreType.DMA((n * (N_DEV - 1),)),
                   *[pltpu.HBM(g.shape, g.dtype) for g in arrs], *[pltpu.HBM(z.shape, z.dtype) for z in lands],
                   jax.ShapeDtypeStruct((8, 128), F32)),
        in_specs=[_HBM] * (2 * n),
        out_specs=(_SEM, _SEM, *[_HBM] * (2 * n), pl.BlockSpec(memory_space=pltpu.VMEM)),
        input_output_aliases={i: 2 + i for i in range(2 * n)},
        compiler_params=pltpu.CompilerParams(has_side_effects=_EFFECT),
    )(*[pltpu.with_memory_space_constraint(g, pltpu.HBM) for g in arrs],
      *[pltpu.with_memory_space_constraint(z, pltpu.HBM) for z in lands])
    return outs[0], outs[1], list(outs[2:2 + n]), list(outs[2 + n:2 + 2 * n]), outs[-1]


def _exchange_wait(send_sems, recv_sems, arrs, lands, after, name):
    n = len(arrs)

    def body(*refs):
        g_refs, land_refs = refs[:n], refs[n:2 * n]
        send_sems_ref, recv_sems_ref = refs[2 * n:2 * n + 2]
        for k in range(1, N_DEV):
            for a in range(n):
                cp = _exchange_copy(g_refs[a], land_refs[a], send_sems_ref, recv_sems_ref, a, k)
                cp.wait_send()
                cp.wait_recv()

    outs = pl.pallas_call(
        body, name=name,
        out_shape=(*[pltpu.HBM(g.shape, g.dtype) for g in arrs], *[pltpu.HBM(z.shape, z.dtype) for z in lands]),
        in_specs=[_HBM] * (2 * n) + [_SEM, _SEM, pl.BlockSpec(memory_space=pl.ANY)],
        out_specs=[_HBM] * (2 * n),
        input_output_aliases={i: i for i in range(2 * n)},
        compiler_params=pltpu.CompilerParams(has_side_effects=_EFFECT),
    )(*arrs, *lands, send_sems, recv_sems, after)
    return list(outs[:n]), list(outs[n:])


def _relu2(a):
    r = jnp.maximum(a.astype(F32), 0.0)
    return (r * r).astype(BF16)


def _run_after(deps, body, in_specs, operands):
    n_deps = len(deps)
    if n_deps == 0:
        return body, in_specs, operands

    def body_behind(*refs):
        body(*refs[n_deps:])

    return body_behind, [pl.BlockSpec(memory_space=pl.ANY)] * n_deps + list(in_specs), list(deps) + list(operands)


def _accumulate(acc_ref, p, k, nk, finish):
    if nk == 1:
        finish(p)
        return

    @pl.when(k == 0)
    def _():
        acc_ref[...] = p

    @pl.when(k > 0)
    def _():
        acc_ref[...] += p

    @pl.when(k == nk - 1)
    def _():
        finish(acc_ref[...])


def _mm_nn(a, b3, *, out_dtype, name, relu2_lhs=False, res=None):
    m, kdim = a.shape
    nb, kb_, nw = b3.shape
    assert kb_ == kdim
    tm, tn, tk = _tile(m, MM_TILE), _tile(nw, MM_TILE), _tile(kdim, MM_TILE_K if not relu2_lhs else MM_TILE)
    per_block = nw // tn
    nk = kdim // tk

    def body(*refs):
        if res is None:
            a_ref, b_ref, o_ref = refs[:3]
            r_ref = None
        else:
            a_ref, b_ref, r_ref, o_ref = refs[:4]
        acc_ref = refs[-1] if nk > 1 else None
        av = a_ref[...]
        if relu2_lhs:
            av = _relu2(av)
        p = jnp.dot(av, b_ref[...], preferred_element_type=F32)

        def finish(total):
            if r_ref is not None:
                total = total + r_ref[...]
            o_ref[...] = total.astype(out_dtype)

        _accumulate(acc_ref, p, pl.program_id(2), nk, finish)

    in_specs = [pl.BlockSpec((tm, tk), lambda i, j, k: (i, k)),
                pl.BlockSpec((None, tk, tn), lambda i, j, k: (j // per_block, k, j % per_block))]
    operands = [a, b3]
    if res is not None:
        in_specs.append(pl.BlockSpec((tm, tn), lambda i, j, k: (i, j)))
        operands.append(res)
    return pl.pallas_call(
        body, name=name, grid=(m // tm, (nb * nw) // tn, nk),
        out_shape=jax.ShapeDtypeStruct((m, nb * nw), out_dtype),
        in_specs=in_specs, out_specs=pl.BlockSpec((tm, tn), lambda i, j, k: (i, j)),
        scratch_shapes=[pltpu.VMEM((tm, tn), F32)] if nk > 1 else [],
        compiler_params=_params("parallel", "parallel", "arbitrary"),
    )(*operands)


def _mm_nt(a, b3, *, out_dtype, name, relu2_grad_of=None, deps=()):
    m, kdim = a.shape
    kb, n, kw = b3.shape
    assert kb * kw == kdim
    tm, tn, tk = _tile(m, MM_TILE), _tile(n, MM_TILE), _tile(kw, MM_TILE_K)
    per_block = kw // tk
    nk = kdim // tk

    def body(*refs):
        if relu2_grad_of is None:
            a_ref, b_ref, o_ref = refs[:3]
            g_ref = None
        else:
            a_ref, b_ref, g_ref, o_ref = refs[:4]
        acc_ref = refs[-1] if nk > 1 else None
        p = lax.dot_general(a_ref[...], b_ref[...], (((1,), (1,)), ((), ())), preferred_element_type=F32)

        def finish(total):
            if g_ref is not None:
                total = total * (2.0 * jnp.maximum(g_ref[...].astype(F32), 0.0))
            o_ref[...] = total.astype(out_dtype)

        _accumulate(acc_ref, p, pl.program_id(2), nk, finish)

    in_specs = [pl.BlockSpec((tm, tk), lambda i, j, k: (i, k)),
                pl.BlockSpec((None, tn, tk), lambda i, j, k: (k // per_block, j, k % per_block))]
    operands = [a, b3]
    if relu2_grad_of is not None:
        in_specs.append(pl.BlockSpec((tm, tn), lambda i, j, k: (i, j)))
        operands.append(relu2_grad_of)
    body, in_specs, operands = _run_after(deps, body, in_specs, operands)
    return pl.pallas_call(
        body, name=name, grid=(m // tm, n // tn, nk),
        out_shape=jax.ShapeDtypeStruct((m, n), out_dtype),
        in_specs=in_specs, out_specs=pl.BlockSpec((tm, tn), lambda i, j, k: (i, j)),
        scratch_shapes=[pltpu.VMEM((tm, tn), F32)] if nk > 1 else [],
        compiler_params=_params("parallel", "parallel", "arbitrary"),
    )(*operands)


def _mm_tn(a, b, *, n_blocks, name, relu2_lhs=False, deps=()):
    t, m = a.shape
    t2, n = b.shape
    assert t == t2 and n % n_blocks == 0
    nw = n // n_blocks
    tm, tn, tk = _tile(m, MM_TILE), _tile(nw, MM_TILE), _tile(t, MM_TILE)
    per_block = nw // tn
    nk = t // tk

    def body(a_ref, b_ref, o_ref, *scratch):
        acc_ref = scratch[0] if nk > 1 else None
        av = a_ref[...]
        if relu2_lhs:
            av = _relu2(av)
        p = lax.dot_general(av, b_ref[...], (((0,), (0,)), ((), ())), preferred_element_type=F32)

        def finish(total):
            o_ref[...] = total.astype(BF16)

        _accumulate(acc_ref, p, pl.program_id(2), nk, finish)

    in_specs = [pl.BlockSpec((tk, tm), lambda i, j, k: (k, i)), pl.BlockSpec((tk, tn), lambda i, j, k: (k, j))]
    body, in_specs, operands = _run_after(deps, body, in_specs, [a, b])
    return pl.pallas_call(
        body, name=name, grid=(m // tm, n // tn, nk),
        out_shape=jax.ShapeDtypeStruct((n_blocks, m, nw), BF16),
        in_specs=in_specs,
        out_specs=pl.BlockSpec((None, tm, tn), lambda i, j, k: (j // per_block, i, j % per_block)),
        scratch_shapes=[pltpu.VMEM((tm, tn), F32)] if nk > 1 else [],
        compiler_params=_params("parallel", "parallel", "arbitrary"),
    )(*operands)


def _normalise(x):
    r = lax.rsqrt(jnp.mean(x * x, axis=-1, keepdims=True) + NORM_EPS)
    return x * r, r


def _rmsnorm_backward(dh, xhat, r, gain):
    dxhat = dh * gain
    return r * (dxhat - xhat * jnp.mean(dxhat * xhat, axis=-1, keepdims=True))


def _rmsnorm(x, gain, name):
    t, d = x.shape
    tr = _tile(t, ROW_TILE)

    def body(x_ref, g_ref, o_ref):
        xhat, _ = _normalise(x_ref[...])
        o_ref[...] = (xhat * g_ref[...]).astype(BF16)

    return pl.pallas_call(
        body, name=name, grid=(t // tr,),
        out_shape=jax.ShapeDtypeStruct((t, d), BF16),
        in_specs=[pl.BlockSpec((tr, d), lambda i: (i, 0)), pl.BlockSpec((1, d), lambda i: (0, 0))],
        out_specs=pl.BlockSpec((tr, d), lambda i: (i, 0)),
        compiler_params=_params("parallel"),
    )(x, gain.reshape(1, d))


def _rmsnorm_bwd(dh, x, gain, dres, name):
    t, d = x.shape
    tr = _tile(t, ROW_TILE)

    def body(dh_ref, x_ref, g_ref, r_ref, dx_ref, dxb_ref, dg_ref):
        xhat, r = _normalise(x_ref[...])
        dh_v = dh_ref[...]
        dx = r_ref[...] + _rmsnorm_backward(dh_v, xhat, r, g_ref[...])
        dx_ref[...] = dx
        dxb_ref[...] = dx.astype(BF16)
        part = jnp.sum(dh_v * xhat, axis=0, keepdims=True)

        @pl.when(pl.program_id(0) == 0)
        def _():
            dg_ref[...] = part

        @pl.when(pl.program_id(0) > 0)
        def _():
            dg_ref[...] += part

    row = pl.BlockSpec((tr, d), lambda i: (i, 0))
    vec = pl.BlockSpec((1, d), lambda i: (0, 0))
    return pl.pallas_call(
        body, name=name, grid=(t // tr,),
        out_shape=[jax.ShapeDtypeStruct((t, d), F32), jax.ShapeDtypeStruct((t, d), BF16),
                   jax.ShapeDtypeStruct((1, d), F32)],
        in_specs=[row, row, vec, row], out_specs=[row, row, vec],
        compiler_params=_params("arbitrary"),
    )(dh, x, gain.reshape(1, d), dres)


def _loss_head(x, gain, target, name):
    t, d = x.shape
    tr = _tile(t, ROW_TILE)
    steps = t // tr

    def body(x_ref, g_ref, t_ref, loss_ref, dx_ref, dxb_ref, dg_ref, sq_ref):
        i = pl.program_id(0)
        xhat, r = _normalise(x_ref[...])
        gain_v = g_ref[...]
        diff = xhat * gain_v - t_ref[...]
        dy = diff / float(d)
        dx = _rmsnorm_backward(dy, xhat, r, gain_v)
        dx_ref[...] = dx
        dxb_ref[...] = dx.astype(BF16)
        dg_part = jnp.sum(dy * xhat, axis=0, keepdims=True)
        sq_part = jnp.sum(diff * diff, axis=0, keepdims=True)

        @pl.when(i == 0)
        def _():
            dg_ref[...] = dg_part
            sq_ref[...] = sq_part

        @pl.when(i > 0)
        def _():
            dg_ref[...] += dg_part
            sq_ref[...] += sq_part

        @pl.when(i == steps - 1)
        def _():
            loss_ref[...] = (0.5 / float(d)) * jnp.sum(sq_ref[...], axis=1, keepdims=True)

    row = pl.BlockSpec((tr, d), lambda i: (i, 0))
    vec = pl.BlockSpec((1, d), lambda i: (0, 0))
    return pl.pallas_call(
        body, name=name, grid=(steps,),
        out_shape=[jax.ShapeDtypeStruct((1, 1), F32), jax.ShapeDtypeStruct((t, d), F32),
                   jax.ShapeDtypeStruct((t, d), BF16), jax.ShapeDtypeStruct((1, d), F32)],
        in_specs=[row, vec, row],
        out_specs=[pl.BlockSpec((1, 1), lambda i: (0, 0)), row, row, vec],
        scratch_shapes=[pltpu.VMEM((1, d), F32)],
        compiler_params=_params("arbitrary"),
    )(x, gain.reshape(1, d), target)


def _shift_down(ext, s):
    return pltpu.roll(ext, s, 0)


def _shift_up(ext, s):
    return pltpu.roll(ext, ext.shape[0] - s, 0)


def _window_sum(ext, w, shift):
    s = 1
    while s < w:
        ext = ext + shift(ext, s)
        s *= 2
    return ext


def _window_count(tile_index, rows, cols, w):
    t = tile_index * rows + lax.broadcasted_iota(jnp.int32, (rows, cols), 0)
    return jnp.minimum(t + 1, w).astype(F32)


def _mixer_sizes(proj, conv_w):
    t, e = proj.shape
    dc = conv_w.shape[1]
    dp = e - 3 * dc
    cg = dp // len(POOL_WINDOWS)
    tt = _tile(t, MIXER_TILE)
    assert tt % HALO_ROWS == 0 and tt >= HALO_ROWS
    cw = _tile(dc, cg)
    return t, e, dc, dp, cg, tt, cw


def _mixer_fwd(proj, conv_w, pool_w, pool_scale, name):
    t, e, dc, dp, cg, tt, cw = _mixer_sizes(proj, conv_w)
    per_halo = tt // HALO_ROWS

    def body(cur_ref, prev_ref, cw_ref, pw_ref, ps_ref, y_ref):
        i = pl.program_id(0)
        first = i == 0

        def cur(lo, width):
            return cur_ref[:, lo:lo + width].astype(F32)

        def prev(lo, width):
            return jnp.where(first, 0.0, prev_ref[:, lo:lo + width].astype(F32))

        for lo in range(0, dc, cw):
            u = cur(dc + lo, cw) * cur(2 * dc + lo, cw)
            ext = jnp.concatenate([prev(dc + lo, cw) * prev(2 * dc + lo, cw), u], axis=0)
            u1 = _shift_down(ext, 1)[HALO_ROWS:]
            u2 = _shift_down(ext, 2)[HALO_ROWS:]
            conv = cw_ref[0:1, lo:lo + cw] * u2 + cw_ref[1:2, lo:lo + cw] * u1 + cw_ref[2:3, lo:lo + cw] * u
            y_ref[:, lo:lo + cw] = (cur(lo, cw) * conv).astype(BF16)

        for g, w in enumerate(POOL_WINDOWS):
            lo = 3 * dc + g * cg
            v = cur(lo, cg)
            ext = jnp.concatenate([prev(lo, cg), v], axis=0)
            mean = _window_sum(ext, w, _shift_down)[HALO_ROWS:] / _window_count(i, tt, cg, w)
            z = jnp.dot((mean - v).astype(BF16), pw_ref[g], preferred_element_type=F32)
            y_ref[:, dc + g * cg:dc + (g + 1) * cg] = (z * ps_ref[0:1, g * cg:(g + 1) * cg]).astype(BF16)

    return pl.pallas_call(
        body, name=name, grid=(t // tt,),
        out_shape=jax.ShapeDtypeStruct((t, dc + dp), BF16),
        in_specs=[pl.BlockSpec((tt, e), lambda i: (i, 0)),
                  pl.BlockSpec((HALO_ROWS, e), lambda i: (jnp.maximum(i * per_halo - 1, 0), 0)),
                  pl.BlockSpec((3, dc), lambda i: (0, 0)),
                  pl.BlockSpec((len(POOL_WINDOWS), cg, cg), lambda i: (0, 0, 0)),
                  pl.BlockSpec((1, dp), lambda i: (0, 0))],
        out_specs=pl.BlockSpec((tt, dc + dp), lambda i: (i, 0)),
        compiler_params=_params("parallel"),
    )(proj, proj, conv_w, pool_w, pool_scale.reshape(1, dp))


def _mixer_bwd(proj, dy, conv_w, pool_w, pool_scale, name, deps=()):
    t, e, dc, dp, cg, tt, cw = _mixer_sizes(proj, conv_w)
    per_halo = tt // HALO_ROWS
    steps = t // tt
    n_groups = len(POOL_WINDOWS)

    def body(cur_ref, prev_ref, next_ref, dy_ref, dyn_ref, cw_ref, pw_ref, ps_ref,
             dp_ref, dcw_ref, dpw_ref, dps_ref):
        i = pl.program_id(0)
        first = i == 0
        last = i == steps - 1

        @pl.when(first)
        def _():
            dcw_ref[...] = jnp.zeros_like(dcw_ref)
            dpw_ref[...] = jnp.zeros_like(dpw_ref)
            dps_ref[...] = jnp.zeros_like(dps_ref)

        def cur(lo, width):
            return cur_ref[:, lo:lo + width].astype(F32)

        def prev(lo, width):
            return jnp.where(first, 0.0, prev_ref[:, lo:lo + width].astype(F32))

        def nxt(ref, lo, width):
            return jnp.where(last, 0.0, ref[:, lo:lo + width].astype(F32))

        def colsum(v):
            return jnp.sum(v, axis=0, keepdims=True)

        for lo in range(0, dc, cw):
            cols = slice(lo, lo + cw)
            b, c, xt = cur(lo, cw), cur(dc + lo, cw), cur(2 * dc + lo, cw)
            u = c * xt
            ext = jnp.concatenate([prev(dc + lo, cw) * prev(2 * dc + lo, cw), u], axis=0)
            u1 = _shift_down(ext, 1)[HALO_ROWS:]
            u2 = _shift_down(ext, 2)[HALO_ROWS:]
            w0, w1, w2 = cw_ref[0:1, cols], cw_ref[1:2, cols], cw_ref[2:3, cols]
            dyc = dy_ref[:, cols].astype(F32)
            dp_ref[:, cols] = (dyc * (w0 * u2 + w1 * u1 + w2 * u)).astype(BF16)
            dconv = dyc * b
            dcw_ref[0:1, cols] += colsum(dconv * u2)
            dcw_ref[1:2, cols] += colsum(dconv * u1)
            dcw_ref[2:3, cols] += colsum(dconv * u)
            dext = jnp.concatenate([dconv, nxt(dyn_ref, lo, cw) * nxt(next_ref, lo, cw)], axis=0)
            du = w2 * dconv + w1 * _shift_up(dext, 1)[:tt] + w0 * _shift_up(dext, 2)[:tt]
            dp_ref[:, dc + lo:dc + lo + cw] = (du * xt).astype(BF16)
            dp_ref[:, 2 * dc + lo:2 * dc + lo + cw] = (du * c).astype(BF16)

        for g, w in enumerate(POOL_WINDOWS):
            lo = 3 * dc + g * cg
            ycols = slice(dc + g * cg, dc + (g + 1) * cg)
            pcols = slice(g * cg, (g + 1) * cg)
            v = cur(lo, cg)
            ext = jnp.concatenate([prev(lo, cg), v], axis=0)
            count = _window_count(i, tt, cg, w)
            d = ((_window_sum(ext, w, _shift_down)[HALO_ROWS:] / count) - v).astype(BF16)
            pw = pw_ref[g]
            scale = ps_ref[0:1, pcols]
            dyp = dy_ref[:, ycols].astype(F32)
            z = jnp.dot(d, pw, preferred_element_type=F32)
            dps_ref[0:1, pcols] += colsum(dyp * z)
            dz = (dyp * scale).astype(BF16)
            dpw_ref[g] += lax.dot_general(d, dz, (((0,), (0,)), ((), ())), preferred_element_type=F32)
            dd = lax.dot_general(dz, pw, (((1,), (1,)), ((), ())), preferred_element_type=F32)
            dzn = (nxt(dyn_ref, dc + g * cg, cg) * scale).astype(BF16)
            ddn = lax.dot_general(dzn, pw, (((1,), (1,)), ((), ())), preferred_element_type=F32)
            qext = jnp.concatenate([dd / count, ddn / float(w)], axis=0)
            dp_ref[:, lo:lo + cg] = (_window_sum(qext, w, _shift_up)[:tt] - dd).astype(BF16)

    cur_spec = lambda width: pl.BlockSpec((tt, width), lambda i: (i, 0))
    prev_spec = pl.BlockSpec((HALO_ROWS, e), lambda i: (jnp.maximum(i * per_halo - 1, 0), 0))
    next_spec = lambda width: pl.BlockSpec(
        (HALO_ROWS, width), lambda i: (jnp.minimum((i + 1) * per_halo, t // HALO_ROWS - 1), 0))
    in_specs = [cur_spec(e), prev_spec, next_spec(e), cur_spec(dc + dp), next_spec(dc + dp),
                pl.BlockSpec((3, dc), lambda i: (0, 0)),
                pl.BlockSpec((n_groups, cg, cg), lambda i: (0, 0, 0)),
                pl.BlockSpec((1, dp), lambda i: (0, 0))]
    body, in_specs, operands = _run_after(
        deps, body, in_specs, [proj, proj, proj, dy, dy, conv_w, pool_w, pool_scale.reshape(1, dp)])
    return pl.pallas_call(
        body, name=name, grid=(steps,),
        out_shape=[jax.ShapeDtypeStruct((t, e), BF16), jax.ShapeDtypeStruct((3, dc), F32),
                   jax.ShapeDtypeStruct((n_groups, cg, cg), F32), jax.ShapeDtypeStruct((1, dp), F32)],
        in_specs=in_specs,
        out_specs=[cur_spec(e), pl.BlockSpec((3, dc), lambda i: (0, 0)),
                   pl.BlockSpec((n_groups, cg, cg), lambda i: (0, 0, 0)),
                   pl.BlockSpec((1, dp), lambda i: (0, 0))],
        compiler_params=_params("arbitrary"),
    )(*operands)


def _adamw(partials, w, m, v, name):
    n_layers, r, c = w.shape
    assert len(partials) == n_layers
    per_layer = len(partials[0])
    flat = [p for layer in partials for p in layer]
    tr = r if r * c <= ADAM_BLOCK_ELEMS else _tile(r, max(16, ADAM_BLOCK_ELEMS // c))

    def body(*refs):
        p_refs = refs[:len(flat)]
        w_ref, m_ref, v_ref, g_out, d_out, m_out, v_out = refs[len(flat):]
        for l in range(n_layers):
            g = None
            for p_ref in p_refs[l * per_layer:(l + 1) * per_layer]:
                for s in range(p_ref.shape[0]):
                    part = p_ref[s].astype(F32)
                    g = part if g is None else g + part
            m_new = ADAM_B1 * m_ref[l] + (1.0 - ADAM_B1) * g
            v_new = ADAM_B2 * v_ref[l] + (1.0 - ADAM_B2) * (g * g)
            m_hat = m_new / (1.0 - ADAM_B1 ** ADAM_STEP)
            v_hat = v_new / (1.0 - ADAM_B2 ** ADAM_STEP)
            g_out[l] = g
            d_out[l] = -ADAM_LR * (m_hat / (jnp.sqrt(v_hat) + ADAM_EPS) + ADAM_WD * w_ref[l])
            m_out[l] = m_new
            v_out[l] = v_new

    p_specs = [pl.BlockSpec((p.shape[0], tr, c), lambda i: (0, i, 0)) for p in flat]
    w_spec = pl.BlockSpec((n_layers, tr, c), lambda i: (0, i, 0))
    return pl.pallas_call(
        body, name=name, grid=(r // tr,),
        out_shape=[jax.ShapeDtypeStruct(w.shape, F32)] * 4,
        in_specs=p_specs + [w_spec] * 3, out_specs=[w_spec] * 4,
        compiler_params=_params("parallel"),
    )(*flat, w, m, v)


def kernel(x, w_in, conv_w, pool_w, pool_scale, w_out, norm_mix, norm_mlp, w_up, w_down, norm_final, loss_target, m_w_in, m_conv_w, m_pool_w, m_pool_scale, m_w_out, m_norm_mix, m_norm_mlp, m_w_up, m_w_down, m_norm_final, v_w_in, v_conv_w, v_pool_w, v_pool_scale, v_w_out, v_norm_mix, v_norm_mlp, v_w_up, v_w_down, v_norm_final):
    n_layers, d, e_shard = w_in.shape
    t = x.shape[1]
    n_groups, cg_shard, cg = pool_w.shape[1:]
    dc_shard = conv_w.shape[2]
    dc, dp = dc_shard * N_DEV, n_groups * cg
    f_shard = w_up.shape[2]
    xs = x.reshape(t, d)
    target = loss_target.reshape(t, d)

    conv_g, pool_g = _all_gather(
        [conv_w.reshape(n_layers * 3, dc_shard), pool_w.reshape(n_layers * n_groups * cg_shard, cg)], "gather_mixer_weights")
    conv_full = conv_g.reshape(N_DEV, n_layers, 3, dc_shard).transpose(1, 2, 0, 3).reshape(n_layers, 3, dc)
    pool_full = pool_g.reshape(N_DEV, n_layers, n_groups, cg_shard, cg).transpose(1, 2, 0, 3, 4)
    pool_full = pool_full.reshape(n_layers, n_groups, cg, cg).astype(BF16)

    weights = []
    for l in range(n_layers):
        win, wout, wup, wdown = _all_gather(
            [w_in[l].astype(BF16), w_out[l].astype(BF16), w_up[l].astype(BF16), w_down[l].astype(BF16)], "gather_layer_weights")
        weights.append((win, wout.reshape(1, d, d), wup, wdown.reshape(1, f_shard * N_DEV, d)))

    saved = []
    xc = xs
    for l in range(n_layers):
        win, wout, wup, wdown = weights[l]
        h1 = _rmsnorm(xc, norm_mix[l], "norm_mix")
        proj = _mm_nn(h1, win, out_dtype=BF16, name="in_proj")
        y = _mixer_fwd(proj, conv_full[l], pool_full[l], pool_scale[l], "mixer_fwd")
        x1 = _mm_nn(y, wout, out_dtype=F32, res=xc, name="out_proj")
        h2 = _rmsnorm(x1, norm_mlp[l], "norm_mlp")
        a = _mm_nn(h2, wup, out_dtype=BF16, name="mlp_up")
        x2 = _mm_nn(a, wdown, out_dtype=F32, relu2_lhs=True, res=x1, name="mlp_down")
        saved.append((xc, h1, proj, y, x1, h2, a))
        xc = x2

    loss_part, dx, dxb, g_norm_final = _loss_head(xc, norm_final, target, "loss_head")
    loss = lax.psum(loss_part[0, 0], MESH_AXES)

    me = _slot(_mesh_position())
    in_flight = {}
    received = {}

    def push(l, arrs, keys, tag):
        send_sems, recv_sems, thru, lands, token = _exchange_start(arrs, f"grads_start_{tag}_{l}")
        in_flight.setdefault(l, []).append((send_sems, recv_sems, thru, lands, keys, tag))
        return token

    def land(l, after):
        for send_sems, recv_sems, thru, lands, keys, tag in in_flight.pop(l):
            thru, lands = _exchange_wait(send_sems, recv_sems, thru, lands, after, f"grads_wait_{tag}_{l}")
            for key, own, got in zip(keys, thru, lands):
                received[key, l] = [lax.dynamic_index_in_dim(own, me, 0, keepdims=True), got]

    g_conv, g_scale, g_mix, g_mlp = ([None] * n_layers for _ in range(4))
    for l in reversed(range(n_layers)):
        win, wout, wup, wdown = weights[l]
        x0, h1, proj, y, x1, h2, a = saved[l]
        da = _mm_nt(dxb, wdown, out_dtype=BF16, relu2_grad_of=a, name="mlp_down_dx")
        gw_down = _mm_tn(a, dxb, n_blocks=1, relu2_lhs=True, name="mlp_down_dw")
        tok_down = push(l, [gw_down.reshape(N_DEV, f_shard, d)], ["w_down"], "down")
        gw_up = _mm_tn(h2, da, n_blocks=N_DEV, name="mlp_up_dw", deps=(tok_down,))
        tok_up = push(l, [gw_up], ["w_up"], "up")
        dh2 = _mm_nt(da, wup, out_dtype=F32, name="mlp_up_dx", deps=(tok_up,))
        dx1, dx1b, g_mlp[l] = _rmsnorm_bwd(dh2, x1, norm_mlp[l], dx, "norm_mlp_bwd")
        dy = _mm_nt(dx1b, wout, out_dtype=BF16, name="out_proj_dx")
        gw_out = _mm_tn(y, dx1b, n_blocks=1, name="out_proj_dw")
        tok_out = push(l, [gw_out.reshape(N_DEV, d // N_DEV, d)], ["w_out"], "out")
        dproj, g_conv[l], gw_pool, g_scale[l] = _mixer_bwd(
            proj, dy, conv_full[l], pool_full[l], pool_scale[l], "mixer_bwd", deps=(tok_out,))
        gw_in = _mm_tn(h1, dproj, n_blocks=N_DEV, name="in_proj_dw")
        gw_pool = gw_pool.reshape(n_groups, N_DEV, cg_shard, cg).transpose(1, 0, 2, 3)
        tok_in = push(l, [gw_in, gw_pool.reshape(N_DEV, n_groups * cg_shard, cg).astype(BF16)], ["w_in", "pool_w"], "in")
        dh1 = _mm_nt(dproj, win, out_dtype=F32, name="in_proj_dx", deps=(tok_in,))
        dx, dxb, g_mix[l] = _rmsnorm_bwd(dh1, x0, norm_mix[l], dx1, "norm_mix_bwd")
        if l + 1 < n_layers:
            land(l + 1, dx)
    land(0, dx)
    grad_x = dx.reshape(x.shape)

    small = _all_gather(
        [jnp.stack(g_conv).reshape(n_layers * 3, dc), jnp.concatenate(g_scale, axis=0),
         jnp.concatenate(g_mix, axis=0), jnp.concatenate(g_mlp, axis=0), g_norm_final], "gather_small_grads")
    conv_parts = lax.dynamic_slice_in_dim(small[0], me * dc_shard, dc_shard, axis=2)

    def update(partials, w, m, v, name):
        shape = w.shape
        rc = (shape[0], -1, shape[-1]) if w.ndim > 2 else (1, *shape) if w.ndim == 2 else (1, 1, *shape)
        outs = _adamw(partials, w.reshape(rc), m.reshape(rc), v.reshape(rc), name)
        return [o.reshape(shape) for o in outs]

    def exchanged(key):
        return [received[key, l] for l in range(n_layers)]

    results = {
        "w_in": update(exchanged("w_in"), w_in, m_w_in, v_w_in, "adamw_w_in"),
        "conv_w": update([[conv_parts]], conv_w.reshape(n_layers * 3, dc_shard), m_conv_w.reshape(n_layers * 3, dc_shard),
                         v_conv_w.reshape(n_layers * 3, dc_shard), "adamw_conv_w"),
        "pool_w": update(exchanged("pool_w"), pool_w, m_pool_w, v_pool_w, "adamw_pool_w"),
        "pool_scale": update([[small[1]]], pool_scale, m_pool_scale, v_pool_scale, "adamw_pool_scale"),
        "w_out": update(exchanged("w_out"), w_out, m_w_out, v_w_out, "adamw_w_out"),
        "norm_mix": update([[small[2]]], norm_mix, m_norm_mix, v_norm_mix, "adamw_norm_mix"),
        "norm_mlp": update([[small[3]]], norm_mlp, m_norm_mlp, v_norm_mlp, "adamw_norm_mlp"),
        "w_up": update(exchanged("w_up"), w_up, m_w_up, v_w_up, "adamw_w_up"),
        "w_down": update(exchanged("w_down"), w_down, m_w_down, v_w_down, "adamw_w_down"),
        "norm_final": update([[small[4]]], norm_final, m_norm_final, v_norm_final, "adamw_norm_final"),
    }
    results["conv_w"] = [o.reshape(conv_w.shape) for o in results["conv_w"]]
    order = ("w_in", "conv_w", "pool_w", "pool_scale", "w_out", "norm_mix", "norm_mlp", "w_up", "w_down", "norm_final")
    return (loss, grad_x, *[results[k][0] for k in order], *[results[k][1] for k in order],
            *[results[k][2] for k in order], *[results[k][3] for k in order])
```

```python
import functools

import jax
import jax.numpy as jnp
from jax import lax
from jax.experimental import pallas as pl
from jax.experimental.pallas import tpu as pltpu

F32 = jnp.float32
BF16 = jnp.bfloat16

N_DEV = 8
MESH_AXES = ("x", "y", "c")
NORM_EPS = 1e-6
POOL_WINDOWS = (2, 4, 8, 16)
HALO_ROWS = 16

ADAM_LR = 0.001
ADAM_B1 = 0.9
ADAM_B2 = 0.999
ADAM_EPS = 1e-08
ADAM_WD = 0.01
ADAM_STEP = 10

VMEM_BYTES_V7X = 64 * 1024 * 1024
VMEM_LIMIT = (VMEM_BYTES_V7X * 3) // 4

MM_TILE = 1024
MM_TILE_K = 2048
ROW_TILE = 256
MIXER_TILE = 512
ADAM_BLOCK_ELEMS = 64 * 1024


def _params(*semantics):
    return pltpu.CompilerParams(dimension_semantics=semantics, vmem_limit_bytes=VMEM_LIMIT)


def _tile(dim, pref):
    t = min(dim, pref)
    assert dim % t == 0, (dim, pref)
    return t


def _mesh_position():
    return lax.axis_index("x"), lax.axis_index("y"), lax.axis_index("c")


def _slot(p):
    return 4 * p[0] + 2 * p[1] + p[2]


def _all_gather(arrs, name):
    n = len(arrs)

    def body(*refs):
        ins, outs = refs[:n], refs[n:2 * n]
        send_sems, recv_sems, local_sems = refs[2 * n:]
        x, y, c = _mesh_position()
        me, sibling = (x, y, c), (x, y, 1 - c)
        chips = [(1 - x, y), (x, 1 - y), (1 - x, 1 - y)]

        def copy(a, k, block, to, src=None):
            dst = outs[a].at[_slot(block)]
            return pltpu.make_async_remote_copy(
                src_ref=dst if src is None else src, dst_ref=dst,
                send_sem=send_sems.at[a, k], recv_sem=recv_sems.at[a, k],
                device_id=to, device_id_type=pl.DeviceIdType.MESH)

        mine = [pltpu.make_async_copy(ins[a], outs[a].at[_slot(me)], local_sems.at[a]) for a in range(n)]
        for cp in mine:
            cp.start()
        first = []
        for a in range(n):
            first.append(copy(a, 0, me, sibling, src=ins[a]))
            first += [copy(a, 1 + j, me, (*chip, c), src=ins[a]) for j, chip in enumerate(chips)]
        for cp in first:
            cp.start()
        passed = []
        for j, chip in enumerate(chips):
            for a in range(n):
                copy(a, 1 + j, (*chip, c), me).wait_recv()
                cp = copy(a, 4 + j, (*chip, c), sibling)
                cp.start()
                passed.append(cp)
        for a in range(n):
            copy(a, 0, sibling, me).wait_recv()
            for j, chip in enumerate(chips):
                copy(a, 4 + j, (*chip, 1 - c), me).wait_recv()
        for cp in first + passed:
            cp.wait_send()
        for cp in mine:
            cp.wait()

    any_spec = pl.BlockSpec(memory_space=pl.ANY)
    return pl.pallas_call(
        body, name=name,
        out_shape=[jax.ShapeDtypeStruct((N_DEV, *a.shape), a.dtype) for a in arrs],
        in_specs=[any_spec] * n, out_specs=[any_spec] * n,
        scratch_shapes=[pltpu.SemaphoreType.DMA((n, 7)), pltpu.SemaphoreType.DMA((n, 7)),
                        pltpu.SemaphoreType.DMA((n,))],
    )(*arrs)


def _peer(k):
    x, y, c = _mesh_position()
    return (1 - x if k & 4 else x, 1 - y if k & 2 else y, 1 - c if k & 1 else c)


_HBM = pl.BlockSpec(memory_space=pltpu.HBM)
_SEM = pl.BlockSpec(memory_space=pltpu.SEMAPHORE)
_EFFECT = pltpu.SideEffectType.DATAFLOW_SIDE_EFFECTING


def _exchange_copy(g_ref, land_ref, send_sems, recv_sems, a, k):
    return pltpu.make_async_remote_copy(
        src_ref=g_ref.at[_slot(_peer(k))], dst_ref=land_ref.at[k - 1],
        send_sem=send_sems.at[a * (N_DEV - 1) + k - 1], recv_sem=recv_sems.at[a * (N_DEV - 1) + k - 1],
        device_id=_peer(k), device_id_type=pl.DeviceIdType.MESH)


def _exchange_start(arrs, name):
    n = len(arrs)

    def body(*refs):
        g_refs, land_refs = refs[:n], refs[n:2 * n]
        send_sems, recv_sems = refs[2 * n:2 * n + 2]
        token = refs[-1]
        for k in range(1, N_DEV):
            for a in range(n):
                _exchange_copy(g_refs[a], land_refs[a], send_sems, recv_sems, a, k).start()
        token[...] = jnp.zeros_like(token)

    lands = [lax.empty((N_DEV - 1, *g.shape[1:]), g.dtype) for g in arrs]
    outs = pl.pallas_call(
        body, name=name,
        out_shape=(pltpu.SemaphoreType.DMA((n * (N_DEV - 1),)), pltpu.SemaphoreType.DMA((n * (N_DEV - 1),)),
                   *[pltpu.HBM(g.shape, g.dtype) for g in arrs], *[pltpu.HBM(z.shape, z.dtype) for z in lands],
                   jax.ShapeDtypeStruct((8, 128), F32)),
        in_specs=[_HBM] * (2 * n),
        out_specs=(_SEM, _SEM, *[_HBM] * (2 * n), pl.BlockSpec(memory_space=pltpu.VMEM)),
        input_output_aliases={i: 2 + i for i in range(2 * n)},
        compiler_params=pltpu.CompilerParams(has_side_effects=_EFFECT),
    )(*[pltpu.with_memory_space_constraint(g, pltpu.HBM) for g in arrs],
      *[pltpu.with_memory_space_constraint(z, pltpu.HBM) for z in lands])
    return outs[0], outs[1], list(outs[2:2 + n]), list(outs[2 + n:2 + 2 * n]), outs[-1]


def _exchange_wait(send_sems, recv_sems, arrs, lands, after, name):
    n = len(arrs)

    def body(*refs):
        g_refs, land_refs = refs[:n], refs[n:2 * n]
        send_sems_ref, recv_sems_ref = refs[2 * n:2 * n + 2]
        for k in range(1, N_DEV):
            for a in range(n):
                cp = _exchange_copy(g_refs[a], land_refs[a], send_sems_ref, recv_sems_ref, a, k)
                cp.wait_send()
                cp.wait_recv()

    outs = pl.pallas_call(
        body, name=name,
        out_shape=(*[pltpu.HBM(g.shape, g.dtype) for g in arrs], *[pltpu.HBM(z.shape, z.dtype) for z in lands]),
        in_specs=[_HBM] * (2 * n) + [_SEM, _SEM, pl.BlockSpec(memory_space=pl.ANY)],
        out_specs=[_HBM] * (2 * n),
        input_output_aliases={i: i for i in range(2 * n)},
        compiler_params=pltpu.CompilerParams(has_side_effects=_EFFECT),
    )(*arrs, *lands, send_sems, recv_sems, after)
    return list(outs[:n]), list(outs[n:])


N_FIRST = 4
N_PASSED = 3


def _other_chips():
    x, y, _ = _mesh_position()
    return [(1 - x, y), (x, 1 - y), (1 - x, 1 - y)]


def _gather_copy(src_ref, land_ref, block, to, send_sem, recv_sem):
    rows = land_ref.at[_slot(block)]
    return pltpu.make_async_remote_copy(
        src_ref=rows if src_ref is None else src_ref, dst_ref=rows, send_sem=send_sem, recv_sem=recv_sem,
        device_id=to, device_id_type=pl.DeviceIdType.MESH)


def _gather_start(arrs, me_slot, after, name):
    n = len(arrs)

    def body(*refs):
        src_refs, land_refs = refs[:n], refs[n:2 * n]
        send_sems, recv_sems = refs[2 * n + 1:2 * n + 3]
        token = refs[-1]
        x, y, c = _mesh_position()
        targets = [(x, y, 1 - c)] + [(*chip, c) for chip in _other_chips()]
        for a in range(n):
            for k, to in enumerate(targets):
                _gather_copy(src_refs[a], land_refs[a], (x, y, c), to,
                             send_sems.at[a * N_FIRST + k], recv_sems.at[a * N_FIRST + k]).start()
        token[...] = jnp.zeros_like(token)

    lands = [lax.dynamic_update_slice_in_dim(lax.empty((N_DEV, *a.shape), a.dtype), a[None], me_slot, axis=0)
             for a in arrs]
    outs = pl.pallas_call(
        body, name=name,
        out_shape=(pltpu.SemaphoreType.DMA((n * N_FIRST,)), pltpu.SemaphoreType.DMA((n * N_FIRST,)),
                   *[pltpu.HBM(a.shape, a.dtype) for a in arrs], *[pltpu.HBM(z.shape, z.dtype) for z in lands],
                   jax.ShapeDtypeStruct((8, 128), F32)),
        in_specs=[_HBM] * (2 * n) + [pl.BlockSpec(memory_space=pl.ANY)],
        out_specs=(_SEM, _SEM, *[_HBM] * (2 * n), pl.BlockSpec(memory_space=pltpu.VMEM)),
        input_output_aliases={i: 2 + i for i in range(2 * n)},
        compiler_params=pltpu.CompilerParams(has_side_effects=_EFFECT),
    )(*[pltpu.with_memory_space_constraint(a, pltpu.HBM) for a in arrs],
      *[pltpu.with_memory_space_constraint(z, pltpu.HBM) for z in lands], after)
    return outs[0], outs[1], list(outs[2:2 + n]), list(outs[2 + n:2 + 2 * n]), outs[-1]


def _gather_pass_on(first_recv_sems, base, lands, after, name):
    n = len(lands)

    def body(*refs):
        land_refs = refs[:n]
        first_recv = refs[n]
        send_sems, recv_sems = refs[n + 2:n + 4]
        x, y, c = _mesh_position()
        sibling = (x, y, 1 - c)
        for j, chip in enumerate(_other_chips()):
            for a in range(n):
                _gather_copy(None, land_refs[a], (*chip, c), sibling,
                             send_sems.at[a * N_PASSED + j], first_recv.at[(base + a) * N_FIRST + 1 + j]).wait_recv()
                _gather_copy(None, land_refs[a], (*chip, c), sibling,
                             send_sems.at[a * N_PASSED + j], recv_sems.at[a * N_PASSED + j]).start()

    outs = pl.pallas_call(
        body, name=name,
        out_shape=(pltpu.SemaphoreType.DMA((n * N_PASSED,)), pltpu.SemaphoreType.DMA((n * N_PASSED,)),
                   *[pltpu.HBM(z.shape, z.dtype) for z in lands]),
        in_specs=[_HBM] * n + [_SEM, pl.BlockSpec(memory_space=pl.ANY)],
        out_specs=(_SEM, _SEM, *[_HBM] * n),
        input_output_aliases={i: 2 + i for i in range(n)},
        compiler_params=pltpu.CompilerParams(has_side_effects=_EFFECT),
    )(*lands, first_recv_sems, after)
    return outs[0], outs[1], list(outs[2:])


def _gather_finish(first_sems, base, passed_sems, arrs, lands, after, name):
    n = len(lands)

    def body(*refs):
        src_refs, land_refs = refs[:n], refs[n:2 * n]
        first_send, first_recv, passed_send, passed_recv = refs[2 * n:2 * n + 4]
        x, y, c = _mesh_position()
        sibling = (x, y, 1 - c)
        chips = _other_chips()
        for a in range(n):
            _gather_copy(src_refs[a], land_refs[a], sibling, sibling,
                         first_send.at[(base + a) * N_FIRST], first_recv.at[(base + a) * N_FIRST]).wait_recv()
            for j, chip in enumerate(chips):
                _gather_copy(None, land_refs[a], (*chip, 1 - c), sibling,
                             passed_send.at[a * N_PASSED + j], passed_recv.at[a * N_PASSED + j]).wait_recv()
        for a in range(n):
            for k in range(N_FIRST):
                _gather_copy(src_refs[a], land_refs[a], (x, y, c), sibling,
                             first_send.at[(base + a) * N_FIRST + k], first_recv.at[(base + a) * N_FIRST + k]).wait_send()
            for j, chip in enumerate(chips):
                _gather_copy(None, land_refs[a], (*chip, c), sibling,
                             passed_send.at[a * N_PASSED + j], passed_recv.at[a * N_PASSED + j]).wait_send()

    outs = pl.pallas_call(
        body, name=name,
        out_shape=(*[pltpu.HBM(a.shape, a.dtype) for a in arrs], *[pltpu.HBM(z.shape, z.dtype) for z in lands]),
        in_specs=[_HBM] * (2 * n) + [_SEM] * 4 + [pl.BlockSpec(memory_space=pl.ANY)],
        out_specs=[_HBM] * (2 * n),
        input_output_aliases={i: i for i in range(2 * n)},
        compiler_params=pltpu.CompilerParams(has_side_effects=_EFFECT),
    )(*arrs, *lands, *first_sems, *passed_sems, after)
    return list(outs[n:])


def _relu2(a):
    r = jnp.maximum(a.astype(F32), 0.0)
    return (r * r).astype(BF16)


def _run_after(deps, body, in_specs, operands):
    n_deps = len(deps)
    if n_deps == 0:
        return body, in_specs, operands

    def body_behind(*refs):
        body(*refs[n_deps:])

    return body_behind, [pl.BlockSpec(memory_space=pl.ANY)] * n_deps + list(in_specs), list(deps) + list(operands)


def _accumulate(acc_ref, p, k, nk, finish):
    if nk == 1:
        finish(p)
        return

    @pl.when(k == 0)
    def _():
        acc_ref[...] = p

    @pl.when(k > 0)
    def _():
        acc_ref[...] += p

    @pl.when(k == nk - 1)
    def _():
        finish(acc_ref[...])


def _mm_nn(a, b3, *, out_dtype, name, relu2_lhs=False, res=None, deps=()):
    m, kdim = a.shape
    nb, kb_, nw = b3.shape
    assert kb_ == kdim
    tm, tn, tk = _tile(m, MM_TILE), _tile(nw, MM_TILE), _tile(kdim, MM_TILE_K if not relu2_lhs else MM_TILE)
    per_block = nw // tn
    nk = kdim // tk

    def body(*refs):
        if res is None:
            a_ref, b_ref, o_ref = refs[:3]
            r_ref = None
        else:
            a_ref, b_ref, r_ref, o_ref = refs[:4]
        acc_ref = refs[-1] if nk > 1 else None
        av = a_ref[...]
        if relu2_lhs:
            av = _relu2(av)
        p = jnp.dot(av, b_ref[...], preferred_element_type=F32)

        def finish(total):
            if r_ref is not None:
                total = total + r_ref[...]
            o_ref[...] = total.astype(out_dtype)

        _accumulate(acc_ref, p, pl.program_id(2), nk, finish)

    in_specs = [pl.BlockSpec((tm, tk), lambda i, j, k: (i, k)),
                pl.BlockSpec((None, tk, tn), lambda i, j, k: (j // per_block, k, j % per_block))]
    operands = [a, b3]
    if res is not None:
        in_specs.append(pl.BlockSpec((tm, tn), lambda i, j, k: (i, j)))
        operands.append(res)
    body, in_specs, operands = _run_after(deps, body, in_specs, operands)
    return pl.pallas_call(
        body, name=name, grid=(m // tm, (nb * nw) // tn, nk),
        out_shape=jax.ShapeDtypeStruct((m, nb * nw), out_dtype),
        in_specs=in_specs, out_specs=pl.BlockSpec((tm, tn), lambda i, j, k: (i, j)),
        scratch_shapes=[pltpu.VMEM((tm, tn), F32)] if nk > 1 else [],
        compiler_params=_params("parallel", "parallel", "arbitrary"),
    )(*operands)


def _mm_nt(a, b3, *, out_dtype, name, relu2_grad_of=None, deps=()):
    m, kdim = a.shape
    kb, n, kw = b3.shape
    assert kb * kw == kdim
    tm, tn, tk = _tile(m, MM_TILE), _tile(n, MM_TILE), _tile(kw, MM_TILE_K)
    per_block = kw // tk
    nk = kdim // tk

    def body(*refs):
        if relu2_grad_of is None:
            a_ref, b_ref, o_ref = refs[:3]
            g_ref = None
        else:
            a_ref, b_ref, g_ref, o_ref = refs[:4]
        acc_ref = refs[-1] if nk > 1 else None
        p = lax.dot_general(a_ref[...], b_ref[...], (((1,), (1,)), ((), ())), preferred_element_type=F32)

        def finish(total):
            if g_ref is not None:
                total = total * (2.0 * jnp.maximum(g_ref[...].astype(F32), 0.0))
            o_ref[...] = total.astype(out_dtype)

        _accumulate(acc_ref, p, pl.program_id(2), nk, finish)

    in_specs = [pl.BlockSpec((tm, tk), lambda i, j, k: (i, k)),
                pl.BlockSpec((None, tn, tk), lambda i, j, k: (k // per_block, j, k % per_block))]
    operands = [a, b3]
    if relu2_grad_of is not None:
        in_specs.append(pl.BlockSpec((tm, tn), lambda i, j, k: (i, j)))
        operands.append(relu2_grad_of)
    body, in_specs, operands = _run_after(deps, body, in_specs, operands)
    return pl.pallas_call(
        body, name=name, grid=(m // tm, n // tn, nk),
        out_shape=jax.ShapeDtypeStruct((m, n), out_dtype),
        in_specs=in_specs, out_specs=pl.BlockSpec((tm, tn), lambda i, j, k: (i, j)),
        scratch_shapes=[pltpu.VMEM((tm, tn), F32)] if nk > 1 else [],
        compiler_params=_params("parallel", "parallel", "arbitrary"),
    )(*operands)


def _mm_tn(a, b, *, n_blocks, name, relu2_lhs=False, deps=()):
    t, m = a.shape
    t2, n = b.shape
    assert t == t2 and n % n_blocks == 0
    nw = n // n_blocks
    tm, tn, tk = _tile(m, MM_TILE), _tile(nw, MM_TILE), _tile(t, MM_TILE)
    per_block = nw // tn
    nk = t // tk

    def body(a_ref, b_ref, o_ref, *scratch):
        acc_ref = scratch[0] if nk > 1 else None
        av = a_ref[...]
        if relu2_lhs:
            av = _relu2(av)
        p = lax.dot_general(av, b_ref[...], (((0,), (0,)), ((), ())), preferred_element_type=F32)

        def finish(total):
            o_ref[...] = total.astype(BF16)

        _accumulate(acc_ref, p, pl.program_id(2), nk, finish)

    in_specs = [pl.BlockSpec((tk, tm), lambda i, j, k: (k, i)), pl.BlockSpec((tk, tn), lambda i, j, k: (k, j))]
    body, in_specs, operands = _run_after(deps, body, in_specs, [a, b])
    return pl.pallas_call(
        body, name=name, grid=(m // tm, n // tn, nk),
        out_shape=jax.ShapeDtypeStruct((n_blocks, m, nw), BF16),
        in_specs=in_specs,
        out_specs=pl.BlockSpec((None, tm, tn), lambda i, j, k: (j // per_block, i, j % per_block)),
        scratch_shapes=[pltpu.VMEM((tm, tn), F32)] if nk > 1 else [],
        compiler_params=_params("parallel", "parallel", "arbitrary"),
    )(*operands)


def _normalise(x):
    r = lax.rsqrt(jnp.mean(x * x, axis=-1, keepdims=True) + NORM_EPS)
    return x * r, r


def _rmsnorm_backward(dh, xhat, r, gain):
    dxhat = dh * gain
    return r * (dxhat - xhat * jnp.mean(dxhat * xhat, axis=-1, keepdims=True))


def _rmsnorm(x, gain, name, deps=()):
    t, d = x.shape
    tr = _tile(t, ROW_TILE)

    def body(x_ref, g_ref, o_ref):
        xhat, _ = _normalise(x_ref[...])
        o_ref[...] = (xhat * g_ref[...]).astype(BF16)

    in_specs = [pl.BlockSpec((tr, d), lambda i: (i, 0)), pl.BlockSpec((1, d), lambda i: (0, 0))]
    body, in_specs, operands = _run_after(deps, body, in_specs, [x, gain.reshape(1, d)])
    return pl.pallas_call(
        body, name=name, grid=(t // tr,),
        out_shape=jax.ShapeDtypeStruct((t, d), BF16),
        in_specs=in_specs,
        out_specs=pl.BlockSpec((tr, d), lambda i: (i, 0)),
        compiler_params=_params("parallel"),
    )(*operands)


def _rmsnorm_bwd(dh, x, gain, dres, name):
    t, d = x.shape
    tr = _tile(t, ROW_TILE)

    def body(dh_ref, x_ref, g_ref, r_ref, dx_ref, dxb_ref, dg_ref):
        xhat, r = _normalise(x_ref[...])
        dh_v = dh_ref[...]
        dx = r_ref[...] + _rmsnorm_backward(dh_v, xhat, r, g_ref[...])
        dx_ref[...] = dx
        dxb_ref[...] = dx.astype(BF16)
        part = jnp.sum(dh_v * xhat, axis=0, keepdims=True)

        @pl.when(pl.program_id(0) == 0)
        def _():
            dg_ref[...] = part

        @pl.when(pl.program_id(0) > 0)
        def _():
            dg_ref[...] += part

    row = pl.BlockSpec((tr, d), lambda i: (i, 0))
    vec = pl.BlockSpec((1, d), lambda i: (0, 0))
    return pl.pallas_call(
        body, name=name, grid=(t // tr,),
        out_shape=[jax.ShapeDtypeStruct((t, d), F32), jax.ShapeDtypeStruct((t, d), BF16),
                   jax.ShapeDtypeStruct((1, d), F32)],
        in_specs=[row, row, vec, row], out_specs=[row, row, vec],
        compiler_params=_params("arbitrary"),
    )(dh, x, gain.reshape(1, d), dres)


def _loss_head(x, gain, target, name):
    t, d = x.shape
    tr = _tile(t, ROW_TILE)
    steps = t // tr

    def body(x_ref, g_ref, t_ref, loss_ref, dx_ref, dxb_ref, dg_ref, sq_ref):
        i = pl.program_id(0)
        xhat, r = _normalise(x_ref[...])
        gain_v = g_ref[...]
        diff = xhat * gain_v - t_ref[...]
        dy = diff / float(d)
        dx = _rmsnorm_backward(dy, xhat, r, gain_v)
        dx_ref[...] = dx
        dxb_ref[...] = dx.astype(BF16)
        dg_part = jnp.sum(dy * xhat, axis=0, keepdims=True)
        sq_part = jnp.sum(diff * diff, axis=0, keepdims=True)

        @pl.when(i == 0)
        def _():
            dg_ref[...] = dg_part
            sq_ref[...] = sq_part

        @pl.when(i > 0)
        def _():
            dg_ref[...] += dg_part
            sq_ref[...] += sq_part

        @pl.when(i == steps - 1)
        def _():
            loss_ref[...] = (0.5 / float(d)) * jnp.sum(sq_ref[...], axis=1, keepdims=True)

    row = pl.BlockSpec((tr, d), lambda i: (i, 0))
    vec = pl.BlockSpec((1, d), lambda i: (0, 0))
    return pl.pallas_call(
        body, name=name, grid=(steps,),
        out_shape=[jax.ShapeDtypeStruct((1, 1), F32), jax.ShapeDtypeStruct((t, d), F32),
                   jax.ShapeDtypeStruct((t, d), BF16), jax.ShapeDtypeStruct((1, d), F32)],
        in_specs=[row, vec, row],
        out_specs=[pl.BlockSpec((1, 1), lambda i: (0, 0)), row, row, vec],
        scratch_shapes=[pltpu.VMEM((1, d), F32)],
        compiler_params=_params("arbitrary"),
    )(x, gain.reshape(1, d), target)


def _shift_down(ext, s):
    return pltpu.roll(ext, s, 0)


def _shift_up(ext, s):
    return pltpu.roll(ext, ext.shape[0] - s, 0)


def _window_sum(ext, w, shift):
    s = 1
    while s < w:
        ext = ext + shift(ext, s)
        s *= 2
    return ext


def _window_count(tile_index, rows, cols, w):
    t = tile_index * rows + lax.broadcasted_iota(jnp.int32, (rows, cols), 0)
    return jnp.minimum(t + 1, w).astype(F32)


def _mixer_sizes(proj, conv_w):
    t, e = proj.shape
    dc = conv_w.shape[1]
    dp = e - 3 * dc
    cg = dp // len(POOL_WINDOWS)
    tt = _tile(t, MIXER_TILE)
    assert tt % HALO_ROWS == 0 and tt >= HALO_ROWS
    cw = _tile(dc, cg)
    return t, e, dc, dp, cg, tt, cw


def _mixer_fwd(proj, conv_w, pool_w, pool_scale, name, deps=()):
    t, e, dc, dp, cg, tt, cw = _mixer_sizes(proj, conv_w)
    per_halo = tt // HALO_ROWS

    def body(cur_ref, prev_ref, cw_ref, pw_ref, ps_ref, y_ref):
        i = pl.program_id(0)
        first = i == 0

        def cur(lo, width):
            return cur_ref[:, lo:lo + width].astype(F32)

        def prev(lo, width):
            return jnp.where(first, 0.0, prev_ref[:, lo:lo + width].astype(F32))

        for lo in range(0, dc, cw):
            u = cur(dc + lo, cw) * cur(2 * dc + lo, cw)
            ext = jnp.concatenate([prev(dc + lo, cw) * prev(2 * dc + lo, cw), u], axis=0)
            u1 = _shift_down(ext, 1)[HALO_ROWS:]
            u2 = _shift_down(ext, 2)[HALO_ROWS:]
            conv = cw_ref[0:1, lo:lo + cw] * u2 + cw_ref[1:2, lo:lo + cw] * u1 + cw_ref[2:3, lo:lo + cw] * u
            y_ref[:, lo:lo + cw] = (cur(lo, cw) * conv).astype(BF16)

        for g, w in enumerate(POOL_WINDOWS):
            lo = 3 * dc + g * cg
            v = cur(lo, cg)
            ext = jnp.concatenate([prev(lo, cg), v], axis=0)
            mean = _window_sum(ext, w, _shift_down)[HALO_ROWS:] / _window_count(i, tt, cg, w)
            z = jnp.dot((mean - v).astype(BF16), pw_ref[g], preferred_element_type=F32)
            y_ref[:, dc + g * cg:dc + (g + 1) * cg] = (z * ps_ref[0:1, g * cg:(g + 1) * cg]).astype(BF16)

    in_specs = [pl.BlockSpec((tt, e), lambda i: (i, 0)),
                pl.BlockSpec((HALO_ROWS, e), lambda i: (jnp.maximum(i * per_halo - 1, 0), 0)),
                pl.BlockSpec((3, dc), lambda i: (0, 0)),
                pl.BlockSpec((len(POOL_WINDOWS), cg, cg), lambda i: (0, 0, 0)),
                pl.BlockSpec((1, dp), lambda i: (0, 0))]
    body, in_specs, operands = _run_after(
        deps, body, in_specs, [proj, proj, conv_w, pool_w, pool_scale.reshape(1, dp)])
    return pl.pallas_call(
        body, name=name, grid=(t // tt,),
        out_shape=jax.ShapeDtypeStruct((t, dc + dp), BF16),
        in_specs=in_specs,
        out_specs=pl.BlockSpec((tt, dc + dp), lambda i: (i, 0)),
        compiler_params=_params("parallel"),
    )(*operands)


def _mixer_bwd(proj, dy, conv_w, pool_w, pool_scale, name, deps=()):
    t, e, dc, dp, cg, tt, cw = _mixer_sizes(proj, conv_w)
    per_halo = tt // HALO_ROWS
    steps = t // tt
    n_groups = len(POOL_WINDOWS)

    def body(cur_ref, prev_ref, next_ref, dy_ref, dyn_ref, cw_ref, pw_ref, ps_ref,
             dp_ref, dcw_ref, dpw_ref, dps_ref):
        i = pl.program_id(0)
        first = i == 0
        last = i == steps - 1

        @pl.when(first)
        def _():
            dcw_ref[...] = jnp.zeros_like(dcw_ref)
            dpw_ref[...] = jnp.zeros_like(dpw_ref)
            dps_ref[...] = jnp.zeros_like(dps_ref)

        def cur(lo, width):
            return cur_ref[:, lo:lo + width].astype(F32)

        def prev(lo, width):
            return jnp.where(first, 0.0, prev_ref[:, lo:lo + width].astype(F32))

        def nxt(ref, lo, width):
            return jnp.where(last, 0.0, ref[:, lo:lo + width].astype(F32))

        def colsum(v):
            return jnp.sum(v, axis=0, keepdims=True)

        for lo in range(0, dc, cw):
            cols = slice(lo, lo + cw)
            b, c, xt = cur(lo, cw), cur(dc + lo, cw), cur(2 * dc + lo, cw)
            u = c * xt
            ext = jnp.concatenate([prev(dc + lo, cw) * prev(2 * dc + lo, cw), u], axis=0)
            u1 = _shift_down(ext, 1)[HALO_ROWS:]
            u2 = _shift_down(ext, 2)[HALO_ROWS:]
            w0, w1, w2 = cw_ref[0:1, cols], cw_ref[1:2, cols], cw_ref[2:3, cols]
            dyc = dy_ref[:, cols].astype(F32)
            dp_ref[:, cols] = (dyc * (w0 * u2 + w1 * u1 + w2 * u)).astype(BF16)
            dconv = dyc * b
            dcw_ref[0:1, cols] += colsum(dconv * u2)
            dcw_ref[1:2, cols] += colsum(dconv * u1)
            dcw_ref[2:3, cols] += colsum(dconv * u)
            dext = jnp.concatenate([dconv, nxt(dyn_ref, lo, cw) * nxt(next_ref, lo, cw)], axis=0)
            du = w2 * dconv + w1 * _shift_up(dext, 1)[:tt] + w0 * _shift_up(dext, 2)[:tt]
            dp_ref[:, dc + lo:dc + lo + cw] = (du * xt).astype(BF16)
            dp_ref[:, 2 * dc + lo:2 * dc + lo + cw] = (du * c).astype(BF16)

        for g, w in enumerate(POOL_WINDOWS):
            lo = 3 * dc + g * cg
            ycols = slice(dc + g * cg, dc + (g + 1) * cg)
            pcols = slice(g * cg, (g + 1) * cg)
            v = cur(lo, cg)
            ext = jnp.concatenate([prev(lo, cg), v], axis=0)
            count = _window_count(i, tt, cg, w)
            d = ((_window_sum(ext, w, _shift_down)[HALO_ROWS:] / count) - v).astype(BF16)
            pw = pw_ref[g]
            scale = ps_ref[0:1, pcols]
            dyp = dy_ref[:, ycols].astype(F32)
            z = jnp.dot(d, pw, preferred_element_type=F32)
            dps_ref[0:1, pcols] += colsum(dyp * z)
            dz = (dyp * scale).astype(BF16)
            dpw_ref[g] += lax.dot_general(d, dz, (((0,), (0,)), ((), ())), preferred_element_type=F32)
            dd = lax.dot_general(dz, pw, (((1,), (1,)), ((), ())), preferred_element_type=F32)
            dzn = (nxt(dyn_ref, dc + g * cg, cg) * scale).astype(BF16)
            ddn = lax.dot_general(dzn, pw, (((1,), (1,)), ((), ())), preferred_element_type=F32)
            qext = jnp.concatenate([dd / count, ddn / float(w)], axis=0)
            dp_ref[:, lo:lo + cg] = (_window_sum(qext, w, _shift_up)[:tt] - dd).astype(BF16)

    cur_spec = lambda width: pl.BlockSpec((tt, width), lambda i: (i, 0))
    prev_spec = pl.BlockSpec((HALO_ROWS, e), lambda i: (jnp.maximum(i * per_halo - 1, 0), 0))
    next_spec = lambda width: pl.BlockSpec(
        (HALO_ROWS, width), lambda i: (jnp.minimum((i + 1) * per_halo, t // HALO_ROWS - 1), 0))
    in_specs = [cur_spec(e), prev_spec, next_spec(e), cur_spec(dc + dp), next_spec(dc + dp),
                pl.BlockSpec((3, dc), lambda i: (0, 0)),
                pl.BlockSpec((n_groups, cg, cg), lambda i: (0, 0, 0)),
                pl.BlockSpec((1, dp), lambda i: (0, 0))]
    body, in_specs, operands = _run_after(
        deps, body, in_specs, [proj, proj, proj, dy, dy, conv_w, pool_w, pool_scale.reshape(1, dp)])
    return pl.pallas_call(
        body, name=name, grid=(steps,),
        out_shape=[jax.ShapeDtypeStruct((t, e), BF16), jax.ShapeDtypeStruct((3, dc), F32),
                   jax.ShapeDtypeStruct((n_groups, cg, cg), F32), jax.ShapeDtypeStruct((1, dp), F32)],
        in_specs=in_specs,
        out_specs=[cur_spec(e), pl.BlockSpec((3, dc), lambda i: (0, 0)),
                   pl.BlockSpec((n_groups, cg, cg), lambda i: (0, 0, 0)),
                   pl.BlockSpec((1, dp), lambda i: (0, 0))],
        compiler_params=_params("arbitrary"),
    )(*operands)


def _adamw(partials, w, m, v, name):
    n_layers, r, c = w.shape
    assert len(partials) == n_layers
    per_layer = len(partials[0])
    flat = [p for layer in partials for p in layer]
    tr = r if r * c <= ADAM_BLOCK_ELEMS else _tile(r, max(16, ADAM_BLOCK_ELEMS // c))

    def body(*refs):
        p_refs = refs[:len(flat)]
        w_ref, m_ref, v_ref, g_out, d_out, m_out, v_out = refs[len(flat):]
        for l in range(n_layers):
            g = None
            for p_ref in p_refs[l * per_layer:(l + 1) * per_layer]:
                for s in range(p_ref.shape[0]):
                    part = p_ref[s].astype(F32)
                    g = part if g is None else g + part
            m_new = ADAM_B1 * m_ref[l] + (1.0 - ADAM_B1) * g
            v_new = ADAM_B2 * v_ref[l] + (1.0 - ADAM_B2) * (g * g)
            m_hat = m_new / (1.0 - ADAM_B1 ** ADAM_STEP)
            v_hat = v_new / (1.0 - ADAM_B2 ** ADAM_STEP)
            g_out[l] = g
            d_out[l] = -ADAM_LR * (m_hat / (jnp.sqrt(v_hat) + ADAM_EPS) + ADAM_WD * w_ref[l])
            m_out[l] = m_new
            v_out[l] = v_new

    p_specs = [pl.BlockSpec((p.shape[0], tr, c), lambda i: (0, i, 0)) for p in flat]
    w_spec = pl.BlockSpec((n_layers, tr, c), lambda i: (0, i, 0))
    return pl.pallas_call(
        body, name=name, grid=(r // tr,),
        out_shape=[jax.ShapeDtypeStruct(w.shape, F32)] * 4,
        in_specs=p_specs + [w_spec] * 3, out_specs=[w_spec] * 4,
        compiler_params=_params("parallel"),
    )(*flat, w, m, v)


def kernel(x, w_in, conv_w, pool_w, pool_scale, w_out, norm_mix, norm_mlp, w_up, w_down, norm_final, loss_target, m_w_in, m_conv_w, m_pool_w, m_pool_scale, m_w_out, m_norm_mix, m_norm_mlp, m_w_up, m_w_down, m_norm_final, v_w_in, v_conv_w, v_pool_w, v_pool_scale, v_w_out, v_norm_mix, v_norm_mlp, v_w_up, v_w_down, v_norm_final):
    n_layers, d, e_shard = w_in.shape
    t = x.shape[1]
    n_groups, cg_shard, cg = pool_w.shape[1:]
    dc_shard = conv_w.shape[2]
    dc, dp = dc_shard * N_DEV, n_groups * cg
    f_shard = w_up.shape[2]
    xs = x.reshape(t, d)
    target = loss_target.reshape(t, d)

    conv_g, pool_g = _all_gather(
        [conv_w.reshape(n_layers * 3, dc_shard), pool_w.reshape(n_layers * n_groups * cg_shard, cg)], "gather_mixer_weights")
    conv_full = conv_g.reshape(N_DEV, n_layers, 3, dc_shard).transpose(1, 2, 0, 3).reshape(n_layers, 3, dc)
    pool_full = pool_g.reshape(N_DEV, n_layers, n_groups, cg_shard, cg).transpose(1, 2, 0, 3, 4)
    pool_full = pool_full.reshape(n_layers, n_groups, cg, cg).astype(BF16)

    me = _slot(_mesh_position())
    shards = [w.astype(BF16) for w in (w_in, w_out, w_up, w_down)]
    n_kinds = len(shards)
    blocks = [shards[k][l] for l in range(n_layers) for k in range(n_kinds)]
    first_send, first_recv, blocks, lands, token = _gather_start(blocks, me, conv_g, "weights_start")

    def passed_on(l, k, after):
        i = n_kinds * l + k
        send_sems, recv_sems, (lands[i],) = _gather_pass_on(first_recv, i, [lands[i]], after, f"weights_pass_{l}_{k}")
        return send_sems, recv_sems

    def gathered(l, k, passed_sems, after):
        i = n_kinds * l + k
        (lands[i],) = _gather_finish((first_send, first_recv), i, passed_sems, [blocks[i]], [lands[i]], after,
                                     f"weights_finish_{l}_{k}")
        return lands[i]

    saved, weights = [], []
    xc = xs
    after = token
    for l in range(n_layers):
        sems = passed_on(l, 0, after)
        h1 = _rmsnorm(xc, norm_mix[l], "norm_mix", deps=(lands[n_kinds * l],))
        win = gathered(l, 0, sems, h1)
        proj = _mm_nn(h1, win, out_dtype=BF16, name="in_proj")
        sems = passed_on(l, 1, proj)
        y = _mixer_fwd(proj, conv_full[l], pool_full[l], pool_scale[l], "mixer_fwd", deps=(lands[n_kinds * l + 1],))
        wout = gathered(l, 1, sems, y).reshape(1, d, d)
        x1 = _mm_nn(y, wout, out_dtype=F32, res=xc, name="out_proj")
        sems = passed_on(l, 2, x1)
        h2 = _rmsnorm(x1, norm_mlp[l], "norm_mlp", deps=(lands[n_kinds * l + 2],))
        wup = gathered(l, 2, sems, h2)
        a = _mm_nn(h2, wup, out_dtype=BF16, name="mlp_up")
        sems = passed_on(l, 3, a)
        wdown = gathered(l, 3, sems, a).reshape(1, f_shard * N_DEV, d)
        x2 = _mm_nn(a, wdown, out_dtype=F32, relu2_lhs=True, res=x1, name="mlp_down")
        weights.append((win, wout, wup, wdown))
        saved.append((xc, h1, proj, y, x1, h2, a))
        xc = after = x2

    loss_part, dx, dxb, g_norm_final = _loss_head(xc, norm_final, target, "loss_head")
    loss = lax.psum(loss_part[0, 0], MESH_AXES)

    in_flight = {}
    received = {}

    def push(l, arrs, keys, tag):
        send_sems, recv_sems, thru, lands, token = _exchange_start(arrs, f"grads_start_{tag}_{l}")
        in_flight.setdefault(l, []).append((send_sems, recv_sems, thru, lands, keys, tag))
        return token

    def land(l, after):
        for send_sems, recv_sems, thru, lands, keys, tag in in_flight.pop(l):
            thru, lands = _exchange_wait(send_sems, recv_sems, thru, lands, after, f"grads_wait_{tag}_{l}")
            for key, own, got in zip(keys, thru, lands):
                received[key, l] = [lax.dynamic_index_in_dim(own, me, 0, keepdims=True), got]

    g_conv, g_scale, g_mix, g_mlp = ([None] * n_layers for _ in range(4))
    for l in reversed(range(n_layers)):
        win, wout, wup, wdown = weights[l]
        x0, h1, proj, y, x1, h2, a = saved[l]
        da = _mm_nt(dxb, wdown, out_dtype=BF16, relu2_grad_of=a, name="mlp_down_dx")
        gw_down = _mm_tn(a, dxb, n_blocks=1, relu2_lhs=True, name="mlp_down_dw")
        tok_down = push(l, [gw_down.reshape(N_DEV, f_shard, d)], ["w_down"], "down")
        gw_up = _mm_tn(h2, da, n_blocks=N_DEV, name="mlp_up_dw", deps=(tok_down,))
        tok_up = push(l, [gw_up], ["w_up"], "up")
        dh2 = _mm_nt(da, wup, out_dtype=F32, name="mlp_up_dx", deps=(tok_up,))
        dx1, dx1b, g_mlp[l] = _rmsnorm_bwd(dh2, x1, norm_mlp[l], dx, "norm_mlp_bwd")
        dy = _mm_nt(dx1b, wout, out_dtype=BF16, name="out_proj_dx")
        gw_out = _mm_tn(y, dx1b, n_blocks=1, name="out_proj_dw")
        tok_out = push(l, [gw_out.reshape(N_DEV, d // N_DEV, d)], ["w_out"], "out")
        dproj, g_conv[l], gw_pool, g_scale[l] = _mixer_bwd(
            proj, dy, conv_full[l], pool_full[l], pool_scale[l], "mixer_bwd", deps=(tok_out,))
        gw_in = _mm_tn(h1, dproj, n_blocks=N_DEV, name="in_proj_dw")
        gw_pool = gw_pool.reshape(n_groups, N_DEV, cg_shard, cg).transpose(1, 0, 2, 3)
        tok_in = push(l, [gw_in, gw_pool.reshape(N_DEV, n_groups * cg_shard, cg).astype(BF16)], ["w_in", "pool_w"], "in")
        dh1 = _mm_nt(dproj, win, out_dtype=F32, name="in_proj_dx", deps=(tok_in,))
        dx, dxb, g_mix[l] = _rmsnorm_bwd(dh1, x0, norm_mix[l], dx1, "norm_mix_bwd")
        if l + 1 < n_layers:
            land(l + 1, dx)
    land(0, dx)
    grad_x = dx.reshape(x.shape)

    small = _all_gather(
        [jnp.stack(g_conv).reshape(n_layers * 3, dc), jnp.concatenate(g_scale, axis=0),
         jnp.concatenate(g_mix, axis=0), jnp.concatenate(g_mlp, axis=0), g_norm_final], "gather_small_grads")
    conv_parts = lax.dynamic_slice_in_dim(small[0], me * dc_shard, dc_shard, axis=2)

    def update(partials, w, m, v, name):
        shape = w.shape
        rc = (shape[0], -1, shape[-1]) if w.ndim > 2 else (1, *shape) if w.ndim == 2 else (1, 1, *shape)
        outs = _adamw(partials, w.reshape(rc), m.reshape(rc), v.reshape(rc), name)
        return [o.reshape(shape) for o in outs]

    def exchanged(key):
        return [received[key, l] for l in range(n_layers)]

    results = {
        "w_in": update(exchanged("w_in"), w_in, m_w_in, v_w_in, "adamw_w_in"),
        "conv_w": update([[conv_parts]], conv_w.reshape(n_layers * 3, dc_shard), m_conv_w.reshape(n_layers * 3, dc_shard),
                         v_conv_w.reshape(n_layers * 3, dc_shard), "adamw_conv_w"),
        "pool_w": update(exchanged("pool_w"), pool_w, m_pool_w, v_pool_w, "adamw_pool_w"),
        "pool_scale": update([[small[1]]], pool_scale, m_pool_scale, v_pool_scale, "adamw_pool_scale"),
        "w_out": update(exchanged("w_out"), w_out, m_w_out, v_w_out, "adamw_w_out"),
        "norm_mix": update([[small[2]]], norm_mix, m_norm_mix, v_norm_mix, "adamw_norm_mix"),
        "norm_mlp": update([[small[3]]], norm_mlp, m_norm_mlp, v_norm_mlp, "adamw_norm_mlp"),
        "w_up": update(exchanged("w_up"), w_up, m_w_up, v_w_up, "adamw_w_up"),
        "w_down": update(exchanged("w_down"), w_down, m_w_down, v_w_down, "adamw_w_down"),
        "norm_final": update([[small[4]]], norm_final, m_norm_final, v_norm_final, "adamw_norm_final"),
    }
    results["conv_w"] = [o.reshape(conv_w.shape) for o in results["conv_w"]]
    order = ("w_in", "conv_w", "pool_w", "pool_scale", "w_out", "norm_mix", "norm_mlp", "w_up", "w_down", "norm_final")
    return (loss, grad_x, *[results[k][0] for k in order], *[results[k][1] for k in order],
            *[results[k][2] for k in order], *[results[k][3] for k in order])
```

```python
import functools

import jax
import jax.numpy as jnp
from jax import lax
from jax.experimental import pallas as pl
from jax.experimental.pallas import tpu as pltpu

F32 = jnp.float32
BF16 = jnp.bfloat16

N_DEV = 8
MESH_AXES = ("x", "y", "c")
NORM_EPS = 1e-6
POOL_WINDOWS = (2, 4, 8, 16)
HALO_ROWS = 16

ADAM_LR = 0.001
ADAM_B1 = 0.9
ADAM_B2 = 0.999
ADAM_EPS = 1e-08
ADAM_WD = 0.01
ADAM_STEP = 10

VMEM_BYTES_V7X = 64 * 1024 * 1024
VMEM_LIMIT = (VMEM_BYTES_V7X * 3) // 4

MM_TILE = 1024
MM_TILE_K = 2048
ROW_TILE = 256
MIXER_TILE = 512
ADAM_BLOCK_ELEMS = 64 * 1024
CAST_BLOCK_ELEMS = 512 * 1024


def _params(*semantics):
    return pltpu.CompilerParams(dimension_semantics=semantics, vmem_limit_bytes=VMEM_LIMIT)


def _tile(dim, pref):
    t = min(dim, pref)
    assert dim % t == 0, (dim, pref)
    return t


def _mesh_position():
    return lax.axis_index("x"), lax.axis_index("y"), lax.axis_index("c")


def _slot(p):
    return 4 * p[0] + 2 * p[1] + p[2]


def _all_gather(arrs, name):
    n = len(arrs)

    def body(*refs):
        ins, outs = refs[:n], refs[n:2 * n]
        send_sems, recv_sems, local_sems = refs[2 * n:]
        x, y, c = _mesh_position()
        me, sibling = (x, y, c), (x, y, 1 - c)
        chips = [(1 - x, y), (x, 1 - y), (1 - x, 1 - y)]

        def copy(a, k, block, to, src=None):
            dst = outs[a].at[_slot(block)]
            return pltpu.make_async_remote_copy(
                src_ref=dst if src is None else src, dst_ref=dst,
                send_sem=send_sems.at[a, k], recv_sem=recv_sems.at[a, k],
                device_id=to, device_id_type=pl.DeviceIdType.MESH)

        mine = [pltpu.make_async_copy(ins[a], outs[a].at[_slot(me)], local_sems.at[a]) for a in range(n)]
        for cp in mine:
            cp.start()
        first = []
        for a in range(n):
            first.append(copy(a, 0, me, sibling, src=ins[a]))
            first += [copy(a, 1 + j, me, (*chip, c), src=ins[a]) for j, chip in enumerate(chips)]
        for cp in first:
            cp.start()
        passed = []
        for j, chip in enumerate(chips):
            for a in range(n):
                copy(a, 1 + j, (*chip, c), me).wait_recv()
                cp = copy(a, 4 + j, (*chip, c), sibling)
                cp.start()
                passed.append(cp)
        for a in range(n):
            copy(a, 0, sibling, me).wait_recv()
            for j, chip in enumerate(chips):
                copy(a, 4 + j, (*chip, 1 - c), me).wait_recv()
        for cp in first + passed:
            cp.wait_send()
        for cp in mine:
            cp.wait()

    any_spec = pl.BlockSpec(memory_space=pl.ANY)
    return pl.pallas_call(
        body, name=name,
        out_shape=[jax.ShapeDtypeStruct((N_DEV, *a.shape), a.dtype) for a in arrs],
        in_specs=[any_spec] * n, out_specs=[any_spec] * n,
        scratch_shapes=[pltpu.SemaphoreType.DMA((n, 7)), pltpu.SemaphoreType.DMA((n, 7)),
                        pltpu.SemaphoreType.DMA((n,))],
    )(*arrs)


def _peer(k):
    x, y, c = _mesh_position()
    return (1 - x if k & 4 else x, 1 - y if k & 2 else y, 1 - c if k & 1 else c)


_HBM = pl.BlockSpec(memory_space=pltpu.HBM)
_SEM = pl.BlockSpec(memory_space=pltpu.SEMAPHORE)
_EFFECT = pltpu.SideEffectType.DATAFLOW_SIDE_EFFECTING


def _exchange_copy(g_ref, land_ref, send_sems, recv_sems, a, k):
    return pltpu.make_async_remote_copy(
        src_ref=g_ref.at[_slot(_peer(k))], dst_ref=land_ref.at[k - 1],
        send_sem=send_sems.at[a * (N_DEV - 1) + k - 1], recv_sem=recv_sems.at[a * (N_DEV - 1) + k - 1],
        device_id=_peer(k), device_id_type=pl.DeviceIdType.MESH)


def _exchange_start(arrs, name):
    n = len(arrs)

    def body(*refs):
        g_refs, land_refs = refs[:n], refs[n:2 * n]
        send_sems, recv_sems = refs[2 * n:2 * n + 2]
        token = refs[-1]
        for k in range(1, N_DEV):
            for a in range(n):
                _exchange_copy(g_refs[a], land_refs[a], send_sems, recv_sems, a, k).start()
        token[...] = jnp.zeros_like(token)

    lands = [lax.empty((N_DEV - 1, *g.shape[1:]), g.dtype) for g in arrs]
    outs = pl.pallas_call(
        body, name=name,
        out_shape=(pltpu.SemaphoreType.DMA((n * (N_DEV - 1),)), pltpu.SemaphoreType.DMA((n * (N_DEV - 1),)),
                   *[pltpu.HBM(g.shape, g.dtype) for g in arrs], *[pltpu.HBM(z.shape, z.dtype) for z in lands],
                   jax.ShapeDtypeStruct((8, 128), F32)),
        in_specs=[_HBM] * (2 * n),
        out_specs=(_SEM, _SEM, *[_HBM] * (2 * n), pl.BlockSpec(memory_space=pltpu.VMEM)),
        input_output_aliases={i: 2 + i for i in range(2 * n)},
        compiler_params=pltpu.CompilerParams(has_side_effects=_EFFECT),
    )(*[pltpu.with_memory_space_constraint(g, pltpu.HBM) for g in arrs],
      *[pltpu.with_memory_space_constraint(z, pltpu.HBM) for z in lands])
    return outs[0], outs[1], list(outs[2:2 + n]), list(outs[2 + n:2 + 2 * n]), outs[-1]


def _exchange_wait(send_sems, recv_sems, arrs, lands, after, name):
    n = len(arrs)

    def body(*refs):
        g_refs, land_refs = refs[:n], refs[n:2 * n]
        send_sems_ref, recv_sems_ref = refs[2 * n:2 * n + 2]
        for k in range(1, N_DEV):
            for a in range(n):
                cp = _exchange_copy(g_refs[a], land_refs[a], send_sems_ref, recv_sems_ref, a, k)
                cp.wait_send()
                cp.wait_recv()

    outs = pl.pallas_call(
        body, name=name,
        out_shape=(*[pltpu.HBM(g.shape, g.dtype) for g in arrs], *[pltpu.HBM(z.shape, z.dtype) for z in lands]),
        in_specs=[_HBM] * (2 * n) + [_SEM, _SEM, pl.BlockSpec(memory_space=pl.ANY)],
        out_specs=[_HBM] * (2 * n),
        input_output_aliases={i: i for i in range(2 * n)},
        compiler_params=pltpu.CompilerParams(has_side_effects=_EFFECT),
    )(*arrs, *lands, send_sems, recv_sems, after)
    return list(outs[:n]), list(outs[n:])


N_FIRST = 4
N_PASSED = 3


def _other_chips():
    x, y, _ = _mesh_position()
    return [(1 - x, y), (x, 1 - y), (1 - x, 1 - y)]


def _gather_copy(src_ref, land_ref, block, to, send_sem, recv_sem):
    rows = land_ref.at[_slot(block)]
    return pltpu.make_async_remote_copy(
        src_ref=rows if src_ref is None else src_ref, dst_ref=rows, send_sem=send_sem, recv_sem=recv_sem,
        device_id=to, device_id_type=pl.DeviceIdType.MESH)


def _gather_start(arrs, me_slot, after, name):
    n = len(arrs)

    def body(*refs):
        src_refs, land_refs = refs[:n], refs[n:2 * n]
        send_sems, recv_sems = refs[2 * n + 1:2 * n + 3]
        token = refs[-1]
        x, y, c = _mesh_position()
        targets = [(x, y, 1 - c)] + [(*chip, c) for chip in _other_chips()]
        for a in range(n):
            for k, to in enumerate(targets):
                _gather_copy(src_refs[a], land_refs[a], (x, y, c), to,
                             send_sems.at[a * N_FIRST + k], recv_sems.at[a * N_FIRST + k]).start()
        token[...] = jnp.zeros_like(token)

    lands = [lax.dynamic_update_slice_in_dim(lax.empty((N_DEV, *a.shape), a.dtype), a[None], me_slot, axis=0)
             for a in arrs]
    outs = pl.pallas_call(
        body, name=name,
        out_shape=(pltpu.SemaphoreType.DMA((n * N_FIRST,)), pltpu.SemaphoreType.DMA((n * N_FIRST,)),
                   *[pltpu.HBM(a.shape, a.dtype) for a in arrs], *[pltpu.HBM(z.shape, z.dtype) for z in lands],
                   jax.ShapeDtypeStruct((8, 128), F32)),
        in_specs=[_HBM] * (2 * n) + [pl.BlockSpec(memory_space=pl.ANY)],
        out_specs=(_SEM, _SEM, *[_HBM] * (2 * n), pl.BlockSpec(memory_space=pltpu.VMEM)),
        input_output_aliases={i: 2 + i for i in range(2 * n)},
        compiler_params=pltpu.CompilerParams(has_side_effects=_EFFECT),
    )(*[pltpu.with_memory_space_constraint(a, pltpu.HBM) for a in arrs],
      *[pltpu.with_memory_space_constraint(z, pltpu.HBM) for z in lands], after)
    return outs[0], outs[1], list(outs[2:2 + n]), list(outs[2 + n:2 + 2 * n]), outs[-1]


def _gather_pass_on(first_recv_sems, base, lands, after, name):
    n = len(lands)

    def body(*refs):
        land_refs = refs[:n]
        first_recv = refs[n]
        send_sems, recv_sems = refs[n + 2:n + 4]
        x, y, c = _mesh_position()
        sibling = (x, y, 1 - c)
        for j, chip in enumerate(_other_chips()):
            for a in range(n):
                _gather_copy(None, land_refs[a], (*chip, c), sibling,
                             send_sems.at[a * N_PASSED + j], first_recv.at[(base + a) * N_FIRST + 1 + j]).wait_recv()
                _gather_copy(None, land_refs[a], (*chip, c), sibling,
                             send_sems.at[a * N_PASSED + j], recv_sems.at[a * N_PASSED + j]).start()

    outs = pl.pallas_call(
        body, name=name,
        out_shape=(pltpu.SemaphoreType.DMA((n * N_PASSED,)), pltpu.SemaphoreType.DMA((n * N_PASSED,)),
                   *[pltpu.HBM(z.shape, z.dtype) for z in lands]),
        in_specs=[_HBM] * n + [_SEM, pl.BlockSpec(memory_space=pl.ANY)],
        out_specs=(_SEM, _SEM, *[_HBM] * n),
        input_output_aliases={i: 2 + i for i in range(n)},
        compiler_params=pltpu.CompilerParams(has_side_effects=_EFFECT),
    )(*lands, first_recv_sems, after)
    return outs[0], outs[1], list(outs[2:])


def _gather_finish(first_sems, base, passed_sems, arrs, lands, after, name):
    n = len(lands)

    def body(*refs):
        src_refs, land_refs = refs[:n], refs[n:2 * n]
        first_send, first_recv, passed_send, passed_recv = refs[2 * n:2 * n + 4]
        x, y, c = _mesh_position()
        sibling = (x, y, 1 - c)
        chips = _other_chips()
        for a in range(n):
            _gather_copy(src_refs[a], land_refs[a], sibling, sibling,
                         first_send.at[(base + a) * N_FIRST], first_recv.at[(base + a) * N_FIRST]).wait_recv()
            for j, chip in enumerate(chips):
                _gather_copy(None, land_refs[a], (*chip, 1 - c), sibling,
                             passed_send.at[a * N_PASSED + j], passed_recv.at[a * N_PASSED + j]).wait_recv()
        for a in range(n):
            for k in range(N_FIRST):
                _gather_copy(src_refs[a], land_refs[a], (x, y, c), sibling,
                             first_send.at[(base + a) * N_FIRST + k], first_recv.at[(base + a) * N_FIRST + k]).wait_send()
            for j, chip in enumerate(chips):
                _gather_copy(None, land_refs[a], (*chip, c), sibling,
                             passed_send.at[a * N_PASSED + j], passed_recv.at[a * N_PASSED + j]).wait_send()

    outs = pl.pallas_call(
        body, name=name,
        out_shape=(*[pltpu.HBM(a.shape, a.dtype) for a in arrs], *[pltpu.HBM(z.shape, z.dtype) for z in lands]),
        in_specs=[_HBM] * (2 * n) + [_SEM] * 4 + [pl.BlockSpec(memory_space=pl.ANY)],
        out_specs=[_HBM] * (2 * n),
        input_output_aliases={i: i for i in range(2 * n)},
        compiler_params=pltpu.CompilerParams(has_side_effects=_EFFECT),
    )(*arrs, *lands, *first_sems, *passed_sems, after)
    return list(outs[n:])


def _run_after(deps, body, in_specs, operands):
    n_deps = len(deps)
    if n_deps == 0:
        return body, in_specs, operands

    def body_behind(*refs):
        body(*refs[n_deps:])

    return body_behind, [pl.BlockSpec(memory_space=pl.ANY)] * n_deps + list(in_specs), list(deps) + list(operands)


def _zero_at_first(acc_ref, k, nk):
    if nk > 1:
        @pl.when(k == 0)
        def _():
            acc_ref[...] = jnp.zeros_like(acc_ref)


def _accumulate(acc_ref, p, k, nk, finish):
    if nk == 1:
        finish(p)
        return
    acc_ref[...] += p

    @pl.when(k == nk - 1)
    def _():
        finish(acc_ref[...])


def _blocks_per_tile(n_blocks, width, pref):
    if width >= pref:
        return 1
    per = min(n_blocks, pref // width)
    assert n_blocks % per == 0
    return per


def _mm_nn(a, b3, *, out_dtype, name, res=None, with_relu2=False, deps=()):
    m, kdim = a.shape
    nb, kb_, nw = b3.shape
    assert kb_ == kdim
    per_tile = _blocks_per_tile(nb, nw, MM_TILE)
    tw = _tile(nw, MM_TILE)
    per_block = nw // tw
    tn = per_tile * tw
    tm, tk = _tile(m, MM_TILE), _tile(kdim, MM_TILE_K)
    nk = kdim // tk
    n_in = 2 + (res is not None)

    def body(*refs):
        a_ref, b_ref = refs[:2]
        r_ref = refs[2] if res is not None else None
        o_ref = refs[n_in]
        s_ref = refs[n_in + 1] if with_relu2 else None
        acc_ref = refs[-1] if nk > 1 else None
        k = pl.program_id(2)
        _zero_at_first(acc_ref, k, nk)
        av = a_ref[...]
        for b in range(per_tile):
            cols = slice(b * tw, (b + 1) * tw)
            p = jnp.dot(av, b_ref[b], preferred_element_type=F32)

            def finish(total, cols=cols):
                if r_ref is not None:
                    total = total + r_ref[:, cols]
                o_ref[:, cols] = total.astype(out_dtype)
                if s_ref is not None:
                    r = jnp.maximum(total, 0.0)
                    s_ref[:, cols] = (r * r).astype(BF16)

            _accumulate(acc_ref.at[:, cols] if nk > 1 else None, p, k, nk, finish)

    if per_tile > 1:
        b_spec = pl.BlockSpec((per_tile, tk, nw), lambda i, j, k: (j, k, 0))
    else:
        b_spec = pl.BlockSpec((1, tk, tw), lambda i, j, k: (j // per_block, k, j % per_block))
    in_specs = [pl.BlockSpec((tm, tk), lambda i, j, k: (i, k)), b_spec]
    operands = [a, b3]
    if res is not None:
        in_specs.append(pl.BlockSpec((tm, tn), lambda i, j, k: (i, j)))
        operands.append(res)
    body, in_specs, operands = _run_after(deps, body, in_specs, operands)
    o_spec = pl.BlockSpec((tm, tn), lambda i, j, k: (i, j))
    out_shape = [jax.ShapeDtypeStruct((m, nb * nw), out_dtype)]
    if with_relu2:
        out_shape.append(jax.ShapeDtypeStruct((m, nb * nw), BF16))
    outs = pl.pallas_call(
        body, name=name, grid=(m // tm, (nb * nw) // tn, nk),
        out_shape=out_shape, in_specs=in_specs, out_specs=[o_spec] * len(out_shape),
        scratch_shapes=[pltpu.VMEM((tm, tn), F32)] if nk > 1 else [],
        compiler_params=_params("parallel", "parallel", "arbitrary"),
    )(*operands)
    return outs if with_relu2 else outs[0]


def _mm_nt(a, b3, *, out_dtype, name, relu2_grad_of=None, deps=()):
    m, kdim = a.shape
    kb, n, kw = b3.shape
    assert kb * kw == kdim
    per_step = _blocks_per_tile(kb, kw, MM_TILE_K)
    tw = _tile(kw, MM_TILE_K)
    per_block = kw // tw
    tk = per_step * tw
    tm, tn = _tile(m, MM_TILE), _tile(n, MM_TILE)
    nk = kdim // tk

    def body(*refs):
        if relu2_grad_of is None:
            a_ref, b_ref, o_ref = refs[:3]
            g_ref = None
        else:
            a_ref, b_ref, g_ref, o_ref = refs[:4]
        acc_ref = refs[-1] if nk > 1 else None
        k = pl.program_id(2)
        _zero_at_first(acc_ref, k, nk)
        p = None
        for b in range(per_step):
            part = lax.dot_general(a_ref[:, b * tw:(b + 1) * tw], b_ref[b], (((1,), (1,)), ((), ())),
                                   preferred_element_type=F32)
            p = part if p is None else p + part

        def finish(total):
            if g_ref is not None:
                total = total * (2.0 * jnp.maximum(g_ref[...].astype(F32), 0.0))
            o_ref[...] = total.astype(out_dtype)

        _accumulate(acc_ref, p, k, nk, finish)

    if per_step > 1:
        b_spec = pl.BlockSpec((per_step, tn, kw), lambda i, j, k: (k, j, 0))
    else:
        b_spec = pl.BlockSpec((1, tn, tw), lambda i, j, k: (k // per_block, j, k % per_block))
    in_specs = [pl.BlockSpec((tm, tk), lambda i, j, k: (i, k)), b_spec]
    operands = [a, b3]
    if relu2_grad_of is not None:
        in_specs.append(pl.BlockSpec((tm, tn), lambda i, j, k: (i, j)))
        operands.append(relu2_grad_of)
    body, in_specs, operands = _run_after(deps, body, in_specs, operands)
    return pl.pallas_call(
        body, name=name, grid=(m // tm, n // tn, nk),
        out_shape=jax.ShapeDtypeStruct((m, n), out_dtype),
        in_specs=in_specs, out_specs=pl.BlockSpec((tm, tn), lambda i, j, k: (i, j)),
        scratch_shapes=[pltpu.VMEM((tm, tn), F32)] if nk > 1 else [],
        compiler_params=_params("parallel", "parallel", "arbitrary"),
    )(*operands)


def _mm_tn(a, b, *, n_blocks, name, deps=()):
    t, m = a.shape
    t2, n = b.shape
    assert t == t2 and n % n_blocks == 0
    nw = n // n_blocks
    per_tile = _blocks_per_tile(n_blocks, nw, MM_TILE)
    tw = _tile(nw, MM_TILE)
    per_block = nw // tw
    tn = per_tile * tw
    tm, tk = _tile(m, MM_TILE), _tile(t, MM_TILE_K)
    nk = t // tk

    def body(a_ref, b_ref, o_ref, *scratch):
        acc_ref = scratch[0] if nk > 1 else None
        k = pl.program_id(2)
        _zero_at_first(acc_ref, k, nk)
        p = lax.dot_general(a_ref[...], b_ref[...], (((0,), (0,)), ((), ())), preferred_element_type=F32)

        def finish(total):
            for b in range(per_tile):
                o_ref[b] = total[:, b * tw:(b + 1) * tw].astype(BF16)

        _accumulate(acc_ref, p, k, nk, finish)

    if per_tile > 1:
        o_spec = pl.BlockSpec((per_tile, tm, nw), lambda i, j, k: (j, i, 0))
    else:
        o_spec = pl.BlockSpec((1, tm, tw), lambda i, j, k: (j // per_block, i, j % per_block))
    in_specs = [pl.BlockSpec((tk, tm), lambda i, j, k: (k, i)), pl.BlockSpec((tk, tn), lambda i, j, k: (k, j))]
    body, in_specs, operands = _run_after(deps, body, in_specs, [a, b])
    return pl.pallas_call(
        body, name=name, grid=(m // tm, n // tn, nk),
        out_shape=jax.ShapeDtypeStruct((n_blocks, m, nw), BF16),
        in_specs=in_specs, out_specs=o_spec,
        scratch_shapes=[pltpu.VMEM((tm, tn), F32)] if nk > 1 else [],
        compiler_params=_params("parallel", "parallel", "arbitrary"),
    )(*operands)


def _normalise(x):
    r = lax.rsqrt(jnp.mean(x * x, axis=-1, keepdims=True) + NORM_EPS)
    return x * r, r


def _rmsnorm_backward(dh, xhat, r, gain):
    dxhat = dh * gain
    return r * (dxhat - xhat * jnp.mean(dxhat * xhat, axis=-1, keepdims=True))


def _rmsnorm(x, gain, name, deps=()):
    t, d = x.shape
    tr = _tile(t, ROW_TILE)

    def body(x_ref, g_ref, o_ref):
        xhat, _ = _normalise(x_ref[...])
        o_ref[...] = (xhat * g_ref[...]).astype(BF16)

    in_specs = [pl.BlockSpec((tr, d), lambda i: (i, 0)), pl.BlockSpec((1, d), lambda i: (0, 0))]
    body, in_specs, operands = _run_after(deps, body, in_specs, [x, gain.reshape(1, d)])
    return pl.pallas_call(
        body, name=name, grid=(t // tr,),
        out_shape=jax.ShapeDtypeStruct((t, d), BF16),
        in_specs=in_specs,
        out_specs=pl.BlockSpec((tr, d), lambda i: (i, 0)),
        compiler_params=_params("parallel"),
    )(*operands)


def _rmsnorm_bwd(dh, x, gain, dres, name):
    t, d = x.shape
    tr = _tile(t, ROW_TILE)

    def body(dh_ref, x_ref, g_ref, r_ref, dx_ref, dxb_ref, dg_ref):
        xhat, r = _normalise(x_ref[...])
        dh_v = dh_ref[...]
        dx = r_ref[...] + _rmsnorm_backward(dh_v, xhat, r, g_ref[...])
        dx_ref[...] = dx
        dxb_ref[...] = dx.astype(BF16)
        part = jnp.sum(dh_v * xhat, axis=0, keepdims=True)

        @pl.when(pl.program_id(0) == 0)
        def _():
            dg_ref[...] = part

        @pl.when(pl.program_id(0) > 0)
        def _():
            dg_ref[...] += part

    row = pl.BlockSpec((tr, d), lambda i: (i, 0))
    vec = pl.BlockSpec((1, d), lambda i: (0, 0))
    return pl.pallas_call(
        body, name=name, grid=(t // tr,),
        out_shape=[jax.ShapeDtypeStruct((t, d), F32), jax.ShapeDtypeStruct((t, d), BF16),
                   jax.ShapeDtypeStruct((1, d), F32)],
        in_specs=[row, row, vec, row], out_specs=[row, row, vec],
        compiler_params=_params("arbitrary"),
    )(dh, x, gain.reshape(1, d), dres)


def _loss_head(x, gain, target, name):
    t, d = x.shape
    tr = _tile(t, ROW_TILE)
    steps = t // tr

    def body(x_ref, g_ref, t_ref, loss_ref, dx_ref, dxb_ref, dg_ref, sq_ref):
        i = pl.program_id(0)
        xhat, r = _normalise(x_ref[...])
        gain_v = g_ref[...]
        diff = xhat * gain_v - t_ref[...]
        dy = diff / float(d)
        dx = _rmsnorm_backward(dy, xhat, r, gain_v)
        dx_ref[...] = dx
        dxb_ref[...] = dx.astype(BF16)
        dg_part = jnp.sum(dy * xhat, axis=0, keepdims=True)
        sq_part = jnp.sum(diff * diff, axis=0, keepdims=True)

        @pl.when(i == 0)
        def _():
            dg_ref[...] = dg_part
            sq_ref[...] = sq_part

        @pl.when(i > 0)
        def _():
            dg_ref[...] += dg_part
            sq_ref[...] += sq_part

        @pl.when(i == steps - 1)
        def _():
            loss_ref[...] = (0.5 / float(d)) * jnp.sum(sq_ref[...], axis=1, keepdims=True)

    row = pl.BlockSpec((tr, d), lambda i: (i, 0))
    vec = pl.BlockSpec((1, d), lambda i: (0, 0))
    return pl.pallas_call(
        body, name=name, grid=(steps,),
        out_shape=[jax.ShapeDtypeStruct((1, 1), F32), jax.ShapeDtypeStruct((t, d), F32),
                   jax.ShapeDtypeStruct((t, d), BF16), jax.ShapeDtypeStruct((1, d), F32)],
        in_specs=[row, vec, row],
        out_specs=[pl.BlockSpec((1, 1), lambda i: (0, 0)), row, row, vec],
        scratch_shapes=[pltpu.VMEM((1, d), F32)],
        compiler_params=_params("arbitrary"),
    )(x, gain.reshape(1, d), target)


def _shift_down(ext, s):
    return pltpu.roll(ext, s, 0)


def _shift_up(ext, s):
    return pltpu.roll(ext, ext.shape[0] - s, 0)


def _window_sum(ext, w, shift):
    s = 1
    while s < w:
        ext = ext + shift(ext, s)
        s *= 2
    return ext


def _window_count(tile_index, rows, cols, w):
    t = tile_index * rows + lax.broadcasted_iota(jnp.int32, (rows, cols), 0)
    return jnp.minimum(t + 1, w).astype(F32)


def _mixer_sizes(proj, conv_w):
    t, e = proj.shape
    dc = conv_w.shape[1]
    dp = e - 3 * dc
    cg = dp // len(POOL_WINDOWS)
    tt = _tile(t, MIXER_TILE)
    assert tt % HALO_ROWS == 0 and tt >= HALO_ROWS
    cw = _tile(dc, cg)
    return t, e, dc, dp, cg, tt, cw


def _mixer_fwd(proj, conv_w, pool_w, pool_scale, name, deps=()):
    t, e, dc, dp, cg, tt, cw = _mixer_sizes(proj, conv_w)
    per_halo = tt // HALO_ROWS

    def body(cur_ref, prev_ref, cw_ref, pw_ref, ps_ref, y_ref):
        i = pl.program_id(0)
        first = i == 0

        def cur(lo, width):
            return cur_ref[:, lo:lo + width].astype(F32)

        def prev(lo, width):
            return jnp.where(first, 0.0, prev_ref[:, lo:lo + width].astype(F32))

        for lo in range(0, dc, cw):
            u = cur(dc + lo, cw) * cur(2 * dc + lo, cw)
            ext = jnp.concatenate([prev(dc + lo, cw) * prev(2 * dc + lo, cw), u], axis=0)
            u1 = _shift_down(ext, 1)[HALO_ROWS:]
            u2 = _shift_down(ext, 2)[HALO_ROWS:]
            conv = cw_ref[0:1, lo:lo + cw] * u2 + cw_ref[1:2, lo:lo + cw] * u1 + cw_ref[2:3, lo:lo + cw] * u
            y_ref[:, lo:lo + cw] = (cur(lo, cw) * conv).astype(BF16)

        for g, w in enumerate(POOL_WINDOWS):
            lo = 3 * dc + g * cg
            v = cur(lo, cg)
            ext = jnp.concatenate([prev(lo, cg), v], axis=0)
            mean = _window_sum(ext, w, _shift_down)[HALO_ROWS:] / _window_count(i, tt, cg, w)
            z = jnp.dot((mean - v).astype(BF16), pw_ref[g], preferred_element_type=F32)
            y_ref[:, dc + g * cg:dc + (g + 1) * cg] = (z * ps_ref[0:1, g * cg:(g + 1) * cg]).astype(BF16)

    in_specs = [pl.BlockSpec((tt, e), lambda i: (i, 0)),
                pl.BlockSpec((HALO_ROWS, e), lambda i: (jnp.maximum(i * per_halo - 1, 0), 0)),
                pl.BlockSpec((3, dc), lambda i: (0, 0)),
                pl.BlockSpec((len(POOL_WINDOWS), cg, cg), lambda i: (0, 0, 0)),
                pl.BlockSpec((1, dp), lambda i: (0, 0))]
    body, in_specs, operands = _run_after(
        deps, body, in_specs, [proj, proj, conv_w, pool_w, pool_scale.reshape(1, dp)])
    return pl.pallas_call(
        body, name=name, grid=(t // tt,),
        out_shape=jax.ShapeDtypeStruct((t, dc + dp), BF16),
        in_specs=in_specs,
        out_specs=pl.BlockSpec((tt, dc + dp), lambda i: (i, 0)),
        compiler_params=_params("parallel"),
    )(*operands)


def _mixer_bwd(proj, dy, conv_w, pool_w, pool_scale, name, deps=()):
    t, e, dc, dp, cg, tt, cw = _mixer_sizes(proj, conv_w)
    per_halo = tt // HALO_ROWS
    steps = t // tt
    n_groups = len(POOL_WINDOWS)

    def body(cur_ref, prev_ref, next_ref, dy_ref, dyn_ref, cw_ref, pw_ref, ps_ref,
             dp_ref, dcw_ref, dpw_ref, dps_ref):
        i = pl.program_id(0)
        first = i == 0
        last = i == steps - 1

        @pl.when(first)
        def _():
            dcw_ref[...] = jnp.zeros_like(dcw_ref)
            dpw_ref[...] = jnp.zeros_like(dpw_ref)
            dps_ref[...] = jnp.zeros_like(dps_ref)

        def cur(lo, width):
            return cur_ref[:, lo:lo + width].astype(F32)

        def prev(lo, width):
            return jnp.where(first, 0.0, prev_ref[:, lo:lo + width].astype(F32))

        def nxt(ref, lo, width):
            return jnp.where(last, 0.0, ref[:, lo:lo + width].astype(F32))

        def colsum(v):
            return jnp.sum(v, axis=0, keepdims=True)

        for lo in range(0, dc, cw):
            cols = slice(lo, lo + cw)
            b, c, xt = cur(lo, cw), cur(dc + lo, cw), cur(2 * dc + lo, cw)
            u = c * xt
            ext = jnp.concatenate([prev(dc + lo, cw) * prev(2 * dc + lo, cw), u], axis=0)
            u1 = _shift_down(ext, 1)[HALO_ROWS:]
            u2 = _shift_down(ext, 2)[HALO_ROWS:]
            w0, w1, w2 = cw_ref[0:1, cols], cw_ref[1:2, cols], cw_ref[2:3, cols]
            dyc = dy_ref[:, cols].astype(F32)
            dp_ref[:, cols] = (dyc * (w0 * u2 + w1 * u1 + w2 * u)).astype(BF16)
            dconv = dyc * b
            dcw_ref[0:1, cols] += colsum(dconv * u2)
            dcw_ref[1:2, cols] += colsum(dconv * u1)
            dcw_ref[2:3, cols] += colsum(dconv * u)
            dext = jnp.concatenate([dconv, nxt(dyn_ref, lo, cw) * nxt(next_ref, lo, cw)], axis=0)
            du = w2 * dconv + w1 * _shift_up(dext, 1)[:tt] + w0 * _shift_up(dext, 2)[:tt]
            dp_ref[:, dc + lo:dc + lo + cw] = (du * xt).astype(BF16)
            dp_ref[:, 2 * dc + lo:2 * dc + lo + cw] = (du * c).astype(BF16)

        for g, w in enumerate(POOL_WINDOWS):
            lo = 3 * dc + g * cg
            ycols = slice(dc + g * cg, dc + (g + 1) * cg)
            pcols = slice(g * cg, (g + 1) * cg)
            v = cur(lo, cg)
            ext = jnp.concatenate([prev(lo, cg), v], axis=0)
            count = _window_count(i, tt, cg, w)
            d = ((_window_sum(ext, w, _shift_down)[HALO_ROWS:] / count) - v).astype(BF16)
            pw = pw_ref[g]
            scale = ps_ref[0:1, pcols]
            dyp = dy_ref[:, ycols].astype(F32)
            z = jnp.dot(d, pw, preferred_element_type=F32)
            dps_ref[0:1, pcols] += colsum(dyp * z)
            dz = (dyp * scale).astype(BF16)
            dpw_ref[g] += lax.dot_general(d, dz, (((0,), (0,)), ((), ())), preferred_element_type=F32)
            dd = lax.dot_general(dz, pw, (((1,), (1,)), ((), ())), preferred_element_type=F32)
            dzn = (nxt(dyn_ref, dc + g * cg, cg) * scale).astype(BF16)
            ddn = lax.dot_general(dzn, pw, (((1,), (1,)), ((), ())), preferred_element_type=F32)
            qext = jnp.concatenate([dd / count, ddn / float(w)], axis=0)
            dp_ref[:, lo:lo + cg] = (_window_sum(qext, w, _shift_up)[:tt] - dd).astype(BF16)

    cur_spec = lambda width: pl.BlockSpec((tt, width), lambda i: (i, 0))
    prev_spec = pl.BlockSpec((HALO_ROWS, e), lambda i: (jnp.maximum(i * per_halo - 1, 0), 0))
    next_spec = lambda width: pl.BlockSpec(
        (HALO_ROWS, width), lambda i: (jnp.minimum((i + 1) * per_halo, t // HALO_ROWS - 1), 0))
    in_specs = [cur_spec(e), prev_spec, next_spec(e), cur_spec(dc + dp), next_spec(dc + dp),
                pl.BlockSpec((3, dc), lambda i: (0, 0)),
                pl.BlockSpec((n_groups, cg, cg), lambda i: (0, 0, 0)),
                pl.BlockSpec((1, dp), lambda i: (0, 0))]
    body, in_specs, operands = _run_after(
        deps, body, in_specs, [proj, proj, proj, dy, dy, conv_w, pool_w, pool_scale.reshape(1, dp)])
    return pl.pallas_call(
        body, name=name, grid=(steps,),
        out_shape=[jax.ShapeDtypeStruct((t, e), BF16), jax.ShapeDtypeStruct((3, dc), F32),
                   jax.ShapeDtypeStruct((n_groups, cg, cg), F32), jax.ShapeDtypeStruct((1, dp), F32)],
        in_specs=in_specs,
        out_specs=[cur_spec(e), pl.BlockSpec((3, dc), lambda i: (0, 0)),
                   pl.BlockSpec((n_groups, cg, cg), lambda i: (0, 0, 0)),
                   pl.BlockSpec((1, dp), lambda i: (0, 0))],
        compiler_params=_params("arbitrary"),
    )(*operands)


def _adamw(partials, w, m, v, name):
    n_layers, r, c = w.shape
    assert len(partials) == n_layers
    per_layer = len(partials[0])
    flat = [p for layer in partials for p in layer]
    tr = r if r * c <= ADAM_BLOCK_ELEMS else _tile(r, max(16, ADAM_BLOCK_ELEMS // c))

    def body(*refs):
        p_refs = refs[:len(flat)]
        w_ref, m_ref, v_ref, g_out, d_out, m_out, v_out = refs[len(flat):]
        for l in range(n_layers):
            g = None
            for p_ref in p_refs[l * per_layer:(l + 1) * per_layer]:
                for s in range(p_ref.shape[0]):
                    part = p_ref[s].astype(F32)
                    g = part if g is None else g + part
            m_new = ADAM_B1 * m_ref[l] + (1.0 - ADAM_B1) * g
            v_new = ADAM_B2 * v_ref[l] + (1.0 - ADAM_B2) * (g * g)
            m_hat = m_new / (1.0 - ADAM_B1 ** ADAM_STEP)
            v_hat = v_new / (1.0 - ADAM_B2 ** ADAM_STEP)
            g_out[l] = g
            d_out[l] = -ADAM_LR * (m_hat / (jnp.sqrt(v_hat) + ADAM_EPS) + ADAM_WD * w_ref[l])
            m_out[l] = m_new
            v_out[l] = v_new

    p_specs = [pl.BlockSpec((p.shape[0], tr, c), lambda i: (0, i, 0)) for p in flat]
    w_spec = pl.BlockSpec((n_layers, tr, c), lambda i: (0, i, 0))
    return pl.pallas_call(
        body, name=name, grid=(r // tr,),
        out_shape=[jax.ShapeDtypeStruct(w.shape, F32)] * 4,
        in_specs=p_specs + [w_spec] * 3, out_specs=[w_spec] * 4,
        compiler_params=_params("parallel"),
    )(*flat, w, m, v)


def _to_bf16(w, first, last, name, deps=()):
    _, r, c = w.shape
    tr = r if r * c <= CAST_BLOCK_ELEMS else _tile(r, max(16, CAST_BLOCK_ELEMS // c))

    def body(w_ref, o_ref):
        o_ref[...] = w_ref[...].astype(BF16)

    in_specs = [pl.BlockSpec((1, tr, c), lambda l, i: (first + l, i, 0))]
    body, in_specs, operands = _run_after(deps, body, in_specs, [w])
    return pl.pallas_call(
        body, name=name, grid=(last - first, r // tr),
        out_shape=jax.ShapeDtypeStruct((last - first, r, c), BF16),
        in_specs=in_specs, out_specs=pl.BlockSpec((1, tr, c), lambda l, i: (l, i, 0)),
        compiler_params=_params("parallel", "parallel"),
    )(*operands)


def kernel(x, w_in, conv_w, pool_w, pool_scale, w_out, norm_mix, norm_mlp, w_up, w_down, norm_final, loss_target, m_w_in, m_conv_w, m_pool_w, m_pool_scale, m_w_out, m_norm_mix, m_norm_mlp, m_w_up, m_w_down, m_norm_final, v_w_in, v_conv_w, v_pool_w, v_pool_scale, v_w_out, v_norm_mix, v_norm_mlp, v_w_up, v_w_down, v_norm_final):
    n_layers, d, e_shard = w_in.shape
    t = x.shape[1]
    n_groups, cg_shard, cg = pool_w.shape[1:]
    dc_shard = conv_w.shape[2]
    dc, dp = dc_shard * N_DEV, n_groups * cg
    f_shard = w_up.shape[2]
    xs = x.reshape(t, d)
    target = loss_target.reshape(t, d)

    me = _slot(_mesh_position())
    big = (w_in, w_out, w_up, w_down)
    n_kinds = len(big)
    first_layer = [_to_bf16(w, 0, 1, "cast_first_layer")[0] for w in big]
    mixer_blocks = [conv_w.reshape(n_layers * 3, dc_shard), pool_w.reshape(n_layers * n_groups * cg_shard, cg)]
    groups = {}

    def start(tag, blocks, after):
        send_sems, recv_sems, blocks, lands, token = _gather_start(blocks, me, after, f"weights_start_{tag}")
        groups[tag] = (send_sems, recv_sems, blocks, lands)
        return token

    token = start("first", [first_layer[0], *mixer_blocks, *first_layer[1:]], xs)
    if n_layers > 1:
        others = [_to_bf16(w, 1, n_layers, "cast_other_layers", deps=(token,)) for w in big]
        token = start("rest", [others[k][l - 1] for l in range(1, n_layers) for k in range(n_kinds)], token)
    mixer_place = ("first", 1)

    def place(l, k):
        return ("first", 0 if k == 0 else k + len(mixer_blocks)) if l == 0 else ("rest", n_kinds * (l - 1) + k)

    def passed_on(where, n, after, name):
        tag, i = where
        lands = groups[tag][3]
        send_sems, recv_sems, lands[i:i + n] = _gather_pass_on(groups[tag][1], i, lands[i:i + n], after, f"weights_pass_{name}")
        return send_sems, recv_sems

    def gathered(where, n, passed_sems, after, name):
        tag, i = where
        first_send, first_recv, blocks, lands = groups[tag]
        lands[i:i + n] = _gather_finish((first_send, first_recv), i, passed_sems, blocks[i:i + n], lands[i:i + n], after,
                                        f"weights_finish_{name}")
        return lands[i:i + n]

    def landing(where):
        return groups[where[0]][3][where[1]]

    saved, weights = [], []
    xc = xs
    after = token
    for l in range(n_layers):
        sems = passed_on(place(l, 0), 1, after, f"{l}_0")
        h1 = _rmsnorm(xc, norm_mix[l], "norm_mix", deps=(landing(place(l, 0)),))
        win, = gathered(place(l, 0), 1, sems, h1, f"{l}_0")
        proj = _mm_nn(h1, win, out_dtype=BF16, name="in_proj")
        if l == 0:
            mixer_sems = passed_on(mixer_place, len(mixer_blocks), proj, "mixer")
        sems = passed_on(place(l, 1), 1, proj, f"{l}_1")
        if l == 0:
            conv_g, pool_g = gathered(mixer_place, len(mixer_blocks), mixer_sems, proj, "mixer")
            conv_full = conv_g.reshape(N_DEV, n_layers, 3, dc_shard).transpose(1, 2, 0, 3).reshape(n_layers, 3, dc)
            pool_full = pool_g.reshape(N_DEV, n_layers, n_groups, cg_shard, cg).transpose(1, 2, 0, 3, 4)
            pool_full = pool_full.reshape(n_layers, n_groups, cg, cg).astype(BF16)
        y = _mixer_fwd(proj, conv_full[l], pool_full[l], pool_scale[l], "mixer_fwd", deps=(landing(place(l, 1)),))
        wout = gathered(place(l, 1), 1, sems, y, f"{l}_1")[0].reshape(1, d, d)
        x1 = _mm_nn(y, wout, out_dtype=F32, res=xc, name="out_proj")
        sems = passed_on(place(l, 2), 1, x1, f"{l}_2")
        h2 = _rmsnorm(x1, norm_mlp[l], "norm_mlp", deps=(landing(place(l, 2)),))
        wup, = gathered(place(l, 2), 1, sems, h2, f"{l}_2")
        a, s = _mm_nn(h2, wup, out_dtype=BF16, with_relu2=True, name="mlp_up")
        sems = passed_on(place(l, 3), 1, a, f"{l}_3")
        wdown = gathered(place(l, 3), 1, sems, a, f"{l}_3")[0].reshape(1, f_shard * N_DEV, d)
        x2 = _mm_nn(s, wdown, out_dtype=F32, res=x1, name="mlp_down")
        weights.append((win, wout, wup, wdown))
        saved.append((xc, h1, proj, y, x1, h2, a, s))
        xc = after = x2

    loss_part, dx, dxb, g_norm_final = _loss_head(xc, norm_final, target, "loss_head")
    loss = lax.psum(loss_part[0, 0], MESH_AXES)

    in_flight = {}
    received = {}

    def push(l, arrs, keys, tag):
        send_sems, recv_sems, thru, lands, token = _exchange_start(arrs, f"grads_start_{tag}_{l}")
        in_flight.setdefault(l, []).append((send_sems, recv_sems, thru, lands, keys, tag))
        return token

    def land(l, after):
        for send_sems, recv_sems, thru, lands, keys, tag in in_flight.pop(l):
            thru, lands = _exchange_wait(send_sems, recv_sems, thru, lands, after, f"grads_wait_{tag}_{l}")
            for key, own, got in zip(keys, thru, lands):
                received[key, l] = [lax.dynamic_index_in_dim(own, me, 0, keepdims=True), got]

    g_conv, g_scale, g_mix, g_mlp = ([None] * n_layers for _ in range(4))
    for l in reversed(range(n_layers)):
        win, wout, wup, wdown = weights[l]
        x0, h1, proj, y, x1, h2, a, s = saved[l]
        da = _mm_nt(dxb, wdown, out_dtype=BF16, relu2_grad_of=a, name="mlp_down_dx")
        gw_down = _mm_tn(s, dxb, n_blocks=1, name="mlp_down_dw")
        tok_down = push(l, [gw_down.reshape(N_DEV, f_shard, d)], ["w_down"], "down")
        gw_up = _mm_tn(h2, da, n_blocks=N_DEV, name="mlp_up_dw", deps=(tok_down,))
        tok_up = push(l, [gw_up], ["w_up"], "up")
        dh2 = _mm_nt(da, wup, out_dtype=F32, name="mlp_up_dx", deps=(tok_up,))
        dx1, dx1b, g_mlp[l] = _rmsnorm_bwd(dh2, x1, norm_mlp[l], dx, "norm_mlp_bwd")
        dy = _mm_nt(dx1b, wout, out_dtype=BF16, name="out_proj_dx")
        gw_out = _mm_tn(y, dx1b, n_blocks=1, name="out_proj_dw")
        tok_out = push(l, [gw_out.reshape(N_DEV, d // N_DEV, d)], ["w_out"], "out")
        dproj, g_conv[l], gw_pool, g_scale[l] = _mixer_bwd(
            proj, dy, conv_full[l], pool_full[l], pool_scale[l], "mixer_bwd", deps=(tok_out,))
        gw_in = _mm_tn(h1, dproj, n_blocks=N_DEV, name="in_proj_dw")
        gw_pool = gw_pool.reshape(n_groups, N_DEV, cg_shard, cg).transpose(1, 0, 2, 3)
        tok_in = push(l, [gw_in, gw_pool.reshape(N_DEV, n_groups * cg_shard, cg).astype(BF16)], ["w_in", "pool_w"], "in")
        dh1 = _mm_nt(dproj, win, out_dtype=F32, name="in_proj_dx", deps=(tok_in,))
        dx, dxb, g_mix[l] = _rmsnorm_bwd(dh1, x0, norm_mix[l], dx1, "norm_mix_bwd")
        if l + 1 < n_layers:
            land(l + 1, dx)
    land(0, dx)
    grad_x = dx.reshape(x.shape)

    small = _all_gather(
        [jnp.stack(g_conv).reshape(n_layers * 3, dc), jnp.concatenate(g_scale, axis=0),
         jnp.concatenate(g_mix, axis=0), jnp.concatenate(g_mlp, axis=0), g_norm_final], "gather_small_grads")
    conv_parts = lax.dynamic_slice_in_dim(small[0], me * dc_shard, dc_shard, axis=2)

    def update(partials, w, m, v, name):
        shape = w.shape
        rc = (shape[0], -1, shape[-1]) if w.ndim > 2 else (1, *shape) if w.ndim == 2 else (1, 1, *shape)
        outs = _adamw(partials, w.reshape(rc), m.reshape(rc), v.reshape(rc), name)
        return [o.reshape(shape) for o in outs]

    def exchanged(key):
        return [received[key, l] for l in range(n_layers)]

    results = {
        "w_in": update(exchanged("w_in"), w_in, m_w_in, v_w_in, "adamw_w_in"),
        "conv_w": update([[conv_parts]], conv_w.reshape(n_layers * 3, dc_shard), m_conv_w.reshape(n_layers * 3, dc_shard),
                         v_conv_w.reshape(n_layers * 3, dc_shard), "adamw_conv_w"),
        "pool_w": update(exchanged("pool_w"), pool_w, m_pool_w, v_pool_w, "adamw_pool_w"),
        "pool_scale": update([[small[1]]], pool_scale, m_pool_scale, v_pool_scale, "adamw_pool_scale"),
        "w_out": update(exchanged("w_out"), w_out, m_w_out, v_w_out, "adamw_w_out"),
        "norm_mix": update([[small[2]]], norm_mix, m_norm_mix, v_norm_mix, "adamw_norm_mix"),
        "norm_mlp": update([[small[3]]], norm_mlp, m_norm_mlp, v_norm_mlp, "adamw_norm_mlp"),
        "w_up": update(exchanged("w_up"), w_up, m_w_up, v_w_up, "adamw_w_up"),
        "w_down": update(exchanged("w_down"), w_down, m_w_down, v_w_down, "adamw_w_down"),
        "norm_final": update([[small[4]]], norm_final, m_norm_final, v_norm_final, "adamw_norm_final"),
    }
    results["conv_w"] = [o.reshape(conv_w.shape) for o in results["conv_w"]]
    order = ("w_in", "conv_w", "pool_w", "pool_scale", "w_out", "norm_mix", "norm_mlp", "w_up", "w_down", "norm_final")
    return (loss, grad_x, *[results[k][0] for k in order], *[results[k][1] for k in order],
            *[results[k][2] for k in order], *[results[k][3] for k in order])
```

```python
import functools

import jax
import jax.numpy as jnp
from jax import lax
from jax.experimental import pallas as pl
from jax.experimental.pallas import tpu as pltpu

F32 = jnp.float32
BF16 = jnp.bfloat16

N_DEV = 8
MESH_AXES = ("x", "y", "c")
NORM_EPS = 1e-6
POOL_WINDOWS = (2, 4, 8, 16)
HALO_ROWS = 16

ADAM_LR = 0.001
ADAM_B1 = 0.9
ADAM_B2 = 0.999
ADAM_EPS = 1e-08
ADAM_WD = 0.01
ADAM_STEP = 10

VMEM_BYTES_V7X = 64 * 1024 * 1024
VMEM_LIMIT = (VMEM_BYTES_V7X * 3) // 4

MM_TILE = 1024
MM_TILE_K = 2048
ROW_TILE = 256
MIXER_TILE = 512
ADAM_BLOCK_ELEMS = 64 * 1024
CAST_BLOCK_ELEMS = 512 * 1024


def _params(*semantics):
    return pltpu.CompilerParams(dimension_semantics=semantics, vmem_limit_bytes=VMEM_LIMIT)


def _tile(dim, pref):
    t = min(dim, pref)
    assert dim % t == 0, (dim, pref)
    return t


def _mesh_position():
    return lax.axis_index("x"), lax.axis_index("y"), lax.axis_index("c")


def _slot(p):
    return 4 * p[0] + 2 * p[1] + p[2]


def _all_gather(arrs, name):
    n = len(arrs)

    def body(*refs):
        ins, outs = refs[:n], refs[n:2 * n]
        send_sems, recv_sems, local_sems = refs[2 * n:]
        x, y, c = _mesh_position()
        me, sibling = (x, y, c), (x, y, 1 - c)
        chips = [(1 - x, y), (x, 1 - y), (1 - x, 1 - y)]

        def copy(a, k, block, to, src=None):
            dst = outs[a].at[_slot(block)]
            return pltpu.make_async_remote_copy(
                src_ref=dst if src is None else src, dst_ref=dst,
                send_sem=send_sems.at[a, k], recv_sem=recv_sems.at[a, k],
                device_id=to, device_id_type=pl.DeviceIdType.MESH)

        mine = [pltpu.make_async_copy(ins[a], outs[a].at[_slot(me)], local_sems.at[a]) for a in range(n)]
        for cp in mine:
            cp.start()
        first = []
        for a in range(n):
            first.append(copy(a, 0, me, sibling, src=ins[a]))
            first += [copy(a, 1 + j, me, (*chip, c), src=ins[a]) for j, chip in enumerate(chips)]
        for cp in first:
            cp.start()
        passed = []
        for j, chip in enumerate(chips):
            for a in range(n):
                copy(a, 1 + j, (*chip, c), me).wait_recv()
                cp = copy(a, 4 + j, (*chip, c), sibling)
                cp.start()
                passed.append(cp)
        for a in range(n):
            copy(a, 0, sibling, me).wait_recv()
            for j, chip in enumerate(chips):
                copy(a, 4 + j, (*chip, 1 - c), me).wait_recv()
        for cp in first + passed:
            cp.wait_send()
        for cp in mine:
            cp.wait()

    any_spec = pl.BlockSpec(memory_space=pl.ANY)
    return pl.pallas_call(
        body, name=name,
        out_shape=[jax.ShapeDtypeStruct((N_DEV, *a.shape), a.dtype) for a in arrs],
        in_specs=[any_spec] * n, out_specs=[any_spec] * n,
        scratch_shapes=[pltpu.SemaphoreType.DMA((n, 7)), pltpu.SemaphoreType.DMA((n, 7)),
                        pltpu.SemaphoreType.DMA((n,))],
    )(*arrs)


def _peer(k):
    x, y, c = _mesh_position()
    return (1 - x if k & 4 else x, 1 - y if k & 2 else y, 1 - c if k & 1 else c)


_HBM = pl.BlockSpec(memory_space=pltpu.HBM)
_SEM = pl.BlockSpec(memory_space=pltpu.SEMAPHORE)
_EFFECT = pltpu.SideEffectType.DATAFLOW_SIDE_EFFECTING


def _exchange_copy(g_ref, land_ref, send_sems, recv_sems, a, k):
    return pltpu.make_async_remote_copy(
        src_ref=g_ref.at[_slot(_peer(k))], dst_ref=land_ref.at[k - 1],
        send_sem=send_sems.at[a * (N_DEV - 1) + k - 1], recv_sem=recv_sems.at[a * (N_DEV - 1) + k - 1],
        device_id=_peer(k), device_id_type=pl.DeviceIdType.MESH)


def _exchange_start(arrs, name):
    n = len(arrs)

    def body(*refs):
        g_refs, land_refs = refs[:n], refs[n:2 * n]
        send_sems, recv_sems = refs[2 * n:2 * n + 2]
        token = refs[-1]
        for k in range(1, N_DEV):
            for a in range(n):
                _exchange_copy(g_refs[a], land_refs[a], send_sems, recv_sems, a, k).start()
        token[...] = jnp.zeros_like(token)

    lands = [lax.empty((N_DEV - 1, *g.shape[1:]), g.dtype) for g in arrs]
    outs = pl.pallas_call(
        body, name=name,
        out_shape=(pltpu.SemaphoreType.DMA((n * (N_DEV - 1),)), pltpu.SemaphoreType.DMA((n * (N_DEV - 1),)),
                   *[pltpu.HBM(g.shape, g.dtype) for g in arrs], *[pltpu.HBM(z.shape, z.dtype) for z in lands],
                   jax.ShapeDtypeStruct((8, 128), F32)),
        in_specs=[_HBM] * (2 * n),
        out_specs=(_SEM, _SEM, *[_HBM] * (2 * n), pl.BlockSpec(memory_space=pltpu.VMEM)),
        input_output_aliases={i: 2 + i for i in range(2 * n)},
        compiler_params=pltpu.CompilerParams(has_side_effects=_EFFECT),
    )(*[pltpu.with_memory_space_constraint(g, pltpu.HBM) for g in arrs],
      *[pltpu.with_memory_space_constraint(z, pltpu.HBM) for z in lands])
    return outs[0], outs[1], list(outs[2:2 + n]), list(outs[2 + n:2 + 2 * n]), outs[-1]


def _exchange_wait(send_sems, recv_sems, arrs, lands, after, name):
    n = len(arrs)

    def body(*refs):
        g_refs, land_refs = refs[:n], refs[n:2 * n]
        send_sems_ref, recv_sems_ref = refs[2 * n:2 * n + 2]
        for k in range(1, N_DEV):
            for a in range(n):
                cp = _exchange_copy(g_refs[a], land_refs[a], send_sems_ref, recv_sems_ref, a, k)
                cp.wait_send()
                cp.wait_recv()

    outs = pl.pallas_call(
        body, name=name,
        out_shape=(*[pltpu.HBM(g.shape, g.dtype) for g in arrs], *[pltpu.HBM(z.shape, z.dtype) for z in lands]),
        in_specs=[_HBM] * (2 * n) + [_SEM, _SEM, pl.BlockSpec(memory_space=pl.ANY)],
        out_specs=[_HBM] * (2 * n),
        input_output_aliases={i: i for i in range(2 * n)},
        compiler_params=pltpu.CompilerParams(has_side_effects=_EFFECT),
    )(*arrs, *lands, send_sems, recv_sems, after)
    return list(outs[:n]), list(outs[n:])


N_FIRST = 4
N_PASSED = 3


def _other_chips():
    x, y, _ = _mesh_position()
    return [(1 - x, y), (x, 1 - y), (1 - x, 1 - y)]


def _gather_copy(src_ref, land_ref, block, to, send_sem, recv_sem):
    rows = land_ref.at[_slot(block)]
    return pltpu.make_async_remote_copy(
        src_ref=rows if src_ref is None else src_ref, dst_ref=rows, send_sem=send_sem, recv_sem=recv_sem,
        device_id=to, device_id_type=pl.DeviceIdType.MESH)


def _gather_start(arrs, me_slot, after, name):
    n = len(arrs)

    def body(*refs):
        src_refs, land_refs = refs[:n], refs[n:2 * n]
        send_sems, recv_sems = refs[2 * n + 1:2 * n + 3]
        token = refs[-1]
        x, y, c = _mesh_position()
        targets = [(x, y, 1 - c)] + [(*chip, c) for chip in _other_chips()]
        for a in range(n):
            for k, to in enumerate(targets):
                _gather_copy(src_refs[a], land_refs[a], (x, y, c), to,
                             send_sems.at[a * N_FIRST + k], recv_sems.at[a * N_FIRST + k]).start()
        token[...] = jnp.zeros_like(token)

    lands = [lax.dynamic_update_slice_in_dim(lax.empty((N_DEV, *a.shape), a.dtype), a[None], me_slot, axis=0)
             for a in arrs]
    outs = pl.pallas_call(
        body, name=name,
        out_shape=(pltpu.SemaphoreType.DMA((n * N_FIRST,)), pltpu.SemaphoreType.DMA((n * N_FIRST,)),
                   *[pltpu.HBM(a.shape, a.dtype) for a in arrs], *[pltpu.HBM(z.shape, z.dtype) for z in lands],
                   jax.ShapeDtypeStruct((8, 128), F32)),
        in_specs=[_HBM] * (2 * n) + [pl.BlockSpec(memory_space=pl.ANY)],
        out_specs=(_SEM, _SEM, *[_HBM] * (2 * n), pl.BlockSpec(memory_space=pltpu.VMEM)),
        input_output_aliases={i: 2 + i for i in range(2 * n)},
        compiler_params=pltpu.CompilerParams(has_side_effects=_EFFECT),
    )(*[pltpu.with_memory_space_constraint(a, pltpu.HBM) for a in arrs],
      *[pltpu.with_memory_space_constraint(z, pltpu.HBM) for z in lands], after)
    return outs[0], outs[1], list(outs[2:2 + n]), list(outs[2 + n:2 + 2 * n]), outs[-1]


def _gather_pass_on(first_recv_sems, base, lands, after, name):
    n = len(lands)

    def body(*refs):
        land_refs = refs[:n]
        first_recv = refs[n]
        send_sems, recv_sems = refs[n + 2:n + 4]
        x, y, c = _mesh_position()
        sibling = (x, y, 1 - c)
        for j, chip in enumerate(_other_chips()):
            for a in range(n):
                _gather_copy(None, land_refs[a], (*chip, c), sibling,
                             send_sems.at[a * N_PASSED + j], first_recv.at[(base + a) * N_FIRST + 1 + j]).wait_recv()
                _gather_copy(None, land_refs[a], (*chip, c), sibling,
                             send_sems.at[a * N_PASSED + j], recv_sems.at[a * N_PASSED + j]).start()

    outs = pl.pallas_call(
        body, name=name,
        out_shape=(pltpu.SemaphoreType.DMA((n * N_PASSED,)), pltpu.SemaphoreType.DMA((n * N_PASSED,)),
                   *[pltpu.HBM(z.shape, z.dtype) for z in lands]),
        in_specs=[_HBM] * n + [_SEM, pl.BlockSpec(memory_space=pl.ANY)],
        out_specs=(_SEM, _SEM, *[_HBM] * n),
        input_output_aliases={i: 2 + i for i in range(n)},
        compiler_params=pltpu.CompilerParams(has_side_effects=_EFFECT),
    )(*lands, first_recv_sems, after)
    return outs[0], outs[1], list(outs[2:])


def _gather_finish(first_sems, base, passed_sems, arrs, lands, after, name):
    n = len(lands)

    def body(*refs):
        src_refs, land_refs = refs[:n], refs[n:2 * n]
        first_send, first_recv, passed_send, passed_recv = refs[2 * n:2 * n + 4]
        x, y, c = _mesh_position()
        sibling = (x, y, 1 - c)
        chips = _other_chips()
        for a in range(n):
            _gather_copy(src_refs[a], land_refs[a], sibling, sibling,
                         first_send.at[(base + a) * N_FIRST], first_recv.at[(base + a) * N_FIRST]).wait_recv()
            for j, chip in enumerate(chips):
                _gather_copy(None, land_refs[a], (*chip, 1 - c), sibling,
                             passed_send.at[a * N_PASSED + j], passed_recv.at[a * N_PASSED + j]).wait_recv()
        for a in range(n):
            for k in range(N_FIRST):
                _gather_copy(src_refs[a], land_refs[a], (x, y, c), sibling,
                             first_send.at[(base + a) * N_FIRST + k], first_recv.at[(base + a) * N_FIRST + k]).wait_send()
            for j, chip in enumerate(chips):
                _gather_copy(None, land_refs[a], (*chip, c), sibling,
                             passed_send.at[a * N_PASSED + j], passed_recv.at[a * N_PASSED + j]).wait_send()

    outs = pl.pallas_call(
        body, name=name,
        out_shape=(*[pltpu.HBM(a.shape, a.dtype) for a in arrs], *[pltpu.HBM(z.shape, z.dtype) for z in lands]),
        in_specs=[_HBM] * (2 * n) + [_SEM] * 4 + [pl.BlockSpec(memory_space=pl.ANY)],
        out_specs=[_HBM] * (2 * n),
        input_output_aliases={i: i for i in range(2 * n)},
        compiler_params=pltpu.CompilerParams(has_side_effects=_EFFECT),
    )(*arrs, *lands, *first_sems, *passed_sems, after)
    return list(outs[n:])


def _run_after(deps, body, in_specs, operands):
    n_deps = len(deps)
    if n_deps == 0:
        return body, in_specs, operands

    def body_behind(*refs):
        body(*refs[n_deps:])

    return body_behind, [pl.BlockSpec(memory_space=pl.ANY)] * n_deps + list(in_specs), list(deps) + list(operands)


def _zero_at_first(acc_ref, k, nk):
    if nk > 1:
        @pl.when(k == 0)
        def _():
            acc_ref[...] = jnp.zeros_like(acc_ref)


def _accumulate(acc_ref, product, k, nk, finish):
    if nk == 1:
        finish(product())
        return

    @pl.when(k < nk - 1)
    def _():
        acc_ref[...] += product()

    @pl.when(k == nk - 1)
    def _():
        finish(acc_ref[...] + product())


def _blocks_per_tile(n_blocks, width, pref):
    if width >= pref:
        return 1
    per = min(n_blocks, pref // width)
    assert n_blocks % per == 0
    return per


def _mm_nn(a, b3, *, out_dtype, name, res=None, with_relu2=False, deps=()):
    m, kdim = a.shape
    nb, kb_, nw = b3.shape
    assert kb_ == kdim
    per_tile = _blocks_per_tile(nb, nw, MM_TILE)
    tw = _tile(nw, MM_TILE)
    per_block = nw // tw
    tn = per_tile * tw
    tm, tk = _tile(m, MM_TILE), _tile(kdim, MM_TILE_K)
    nk = kdim // tk
    n_in = 2 + (res is not None)

    def body(*refs):
        a_ref, b_ref = refs[:2]
        r_ref = refs[2] if res is not None else None
        o_ref = refs[n_in]
        s_ref = refs[n_in + 1] if with_relu2 else None
        acc_ref = refs[-1] if nk > 1 else None
        k = pl.program_id(2)
        _zero_at_first(acc_ref, k, nk)
        for b in range(per_tile):
            cols = slice(b * tw, (b + 1) * tw)

            def product(b=b):
                return jnp.dot(a_ref[...], b_ref[b], preferred_element_type=F32)

            def finish(total, cols=cols):
                if r_ref is not None:
                    total = total + r_ref[:, cols]
                o_ref[:, cols] = total.astype(out_dtype)
                if s_ref is not None:
                    r = jnp.maximum(total, 0.0)
                    s_ref[:, cols] = (r * r).astype(BF16)

            _accumulate(acc_ref.at[:, cols] if nk > 1 else None, product, k, nk, finish)

    if per_tile > 1:
        b_spec = pl.BlockSpec((per_tile, tk, nw), lambda i, j, k: (j, k, 0))
    else:
        b_spec = pl.BlockSpec((1, tk, tw), lambda i, j, k: (j // per_block, k, j % per_block))
    in_specs = [pl.BlockSpec((tm, tk), lambda i, j, k: (i, k)), b_spec]
    operands = [a, b3]
    if res is not None:
        in_specs.append(pl.BlockSpec((tm, tn), lambda i, j, k: (i, j)))
        operands.append(res)
    body, in_specs, operands = _run_after(deps, body, in_specs, operands)
    o_spec = pl.BlockSpec((tm, tn), lambda i, j, k: (i, j))
    out_shape = [jax.ShapeDtypeStruct((m, nb * nw), out_dtype)]
    if with_relu2:
        out_shape.append(jax.ShapeDtypeStruct((m, nb * nw), BF16))
    outs = pl.pallas_call(
        body, name=name, grid=(m // tm, (nb * nw) // tn, nk),
        out_shape=out_shape, in_specs=in_specs, out_specs=[o_spec] * len(out_shape),
        scratch_shapes=[pltpu.VMEM((tm, tn), F32)] if nk > 1 else [],
        compiler_params=_params("parallel", "parallel", "arbitrary"),
    )(*operands)
    return outs if with_relu2 else outs[0]


def _mm_nt(a, b3, *, out_dtype, name, relu2_grad_of=None, deps=()):
    m, kdim = a.shape
    kb, n, kw = b3.shape
    assert kb * kw == kdim
    per_step = _blocks_per_tile(kb, kw, MM_TILE_K)
    tw = _tile(kw, MM_TILE_K)
    per_block = kw // tw
    tk = per_step * tw
    tm, tn = _tile(m, MM_TILE), _tile(n, MM_TILE)
    nk = kdim // tk

    def body(*refs):
        if relu2_grad_of is None:
            a_ref, b_ref, o_ref = refs[:3]
            g_ref = None
        else:
            a_ref, b_ref, g_ref, o_ref = refs[:4]
        acc_ref = refs[-1] if nk > 1 else None
        k = pl.program_id(2)
        _zero_at_first(acc_ref, k, nk)
        def product():
            p = None
            for b in range(per_step):
                part = lax.dot_general(a_ref[:, b * tw:(b + 1) * tw], b_ref[b], (((1,), (1,)), ((), ())),
                                       preferred_element_type=F32)
                p = part if p is None else p + part
            return p

        def finish(total):
            if g_ref is not None:
                total = total * (2.0 * jnp.maximum(g_ref[...].astype(F32), 0.0))
            o_ref[...] = total.astype(out_dtype)

        _accumulate(acc_ref, product, k, nk, finish)

    if per_step > 1:
        b_spec = pl.BlockSpec((per_step, tn, kw), lambda i, j, k: (k, j, 0))
    else:
        b_spec = pl.BlockSpec((1, tn, tw), lambda i, j, k: (k // per_block, j, k % per_block))
    in_specs = [pl.BlockSpec((tm, tk), lambda i, j, k: (i, k)), b_spec]
    operands = [a, b3]
    if relu2_grad_of is not None:
        in_specs.append(pl.BlockSpec((tm, tn), lambda i, j, k: (i, j)))
        operands.append(relu2_grad_of)
    body, in_specs, operands = _run_after(deps, body, in_specs, operands)
    return pl.pallas_call(
        body, name=name, grid=(m // tm, n // tn, nk),
        out_shape=jax.ShapeDtypeStruct((m, n), out_dtype),
        in_specs=in_specs, out_specs=pl.BlockSpec((tm, tn), lambda i, j, k: (i, j)),
        scratch_shapes=[pltpu.VMEM((tm, tn), F32)] if nk > 1 else [],
        compiler_params=_params("parallel", "parallel", "arbitrary"),
    )(*operands)


def _mm_tn(a, b, *, n_blocks, name, deps=()):
    t, m = a.shape
    t2, n = b.shape
    assert t == t2 and n % n_blocks == 0
    nw = n // n_blocks
    per_tile = _blocks_per_tile(n_blocks, nw, MM_TILE)
    tw = _tile(nw, MM_TILE)
    per_block = nw // tw
    tn = per_tile * tw
    tm, tk = _tile(m, MM_TILE), _tile(t, MM_TILE_K)
    nk = t // tk

    def body(a_ref, b_ref, o_ref, *scratch):
        acc_ref = scratch[0] if nk > 1 else None
        k = pl.program_id(2)
        _zero_at_first(acc_ref, k, nk)
        def product():
            return lax.dot_general(a_ref[...], b_ref[...], (((0,), (0,)), ((), ())), preferred_element_type=F32)

        def finish(total):
            for b in range(per_tile):
                o_ref[b] = total[:, b * tw:(b + 1) * tw].astype(BF16)

        _accumulate(acc_ref, product, k, nk, finish)

    if per_tile > 1:
        o_spec = pl.BlockSpec((per_tile, tm, nw), lambda i, j, k: (j, i, 0))
    else:
        o_spec = pl.BlockSpec((1, tm, tw), lambda i, j, k: (j // per_block, i, j % per_block))
    in_specs = [pl.BlockSpec((tk, tm), lambda i, j, k: (k, i)), pl.BlockSpec((tk, tn), lambda i, j, k: (k, j))]
    body, in_specs, operands = _run_after(deps, body, in_specs, [a, b])
    return pl.pallas_call(
        body, name=name, grid=(m // tm, n // tn, nk),
        out_shape=jax.ShapeDtypeStruct((n_blocks, m, nw), BF16),
        in_specs=in_specs, out_specs=o_spec,
        scratch_shapes=[pltpu.VMEM((tm, tn), F32)] if nk > 1 else [],
        compiler_params=_params("parallel", "parallel", "arbitrary"),
    )(*operands)


def _normalise(x):
    r = lax.rsqrt(jnp.mean(x * x, axis=-1, keepdims=True) + NORM_EPS)
    return x * r, r


def _rmsnorm_backward(dh, xhat, r, gain):
    dxhat = dh * gain
    return r * (dxhat - xhat * jnp.mean(dxhat * xhat, axis=-1, keepdims=True))


def _rmsnorm(x, gain, name, deps=()):
    t, d = x.shape
    tr = _tile(t, ROW_TILE)

    def body(x_ref, g_ref, o_ref):
        xhat, _ = _normalise(x_ref[...])
        o_ref[...] = (xhat * g_ref[...]).astype(BF16)

    in_specs = [pl.BlockSpec((tr, d), lambda i: (i, 0)), pl.BlockSpec((1, d), lambda i: (0, 0))]
    body, in_specs, operands = _run_after(deps, body, in_specs, [x, gain.reshape(1, d)])
    return pl.pallas_call(
        body, name=name, grid=(t // tr,),
        out_shape=jax.ShapeDtypeStruct((t, d), BF16),
        in_specs=in_specs,
        out_specs=pl.BlockSpec((tr, d), lambda i: (i, 0)),
        compiler_params=_params("parallel"),
    )(*operands)


def _rmsnorm_bwd(dh, x, gain, dres, name):
    t, d = x.shape
    tr = _tile(t, ROW_TILE)

    def body(dh_ref, x_ref, g_ref, r_ref, dx_ref, dxb_ref, dg_ref):
        xhat, r = _normalise(x_ref[...])
        dh_v = dh_ref[...]
        dx = r_ref[...] + _rmsnorm_backward(dh_v, xhat, r, g_ref[...])
        dx_ref[...] = dx
        dxb_ref[...] = dx.astype(BF16)
        part = jnp.sum(dh_v * xhat, axis=0, keepdims=True)

        @pl.when(pl.program_id(0) == 0)
        def _():
            dg_ref[...] = part

        @pl.when(pl.program_id(0) > 0)
        def _():
            dg_ref[...] += part

    row = pl.BlockSpec((tr, d), lambda i: (i, 0))
    vec = pl.BlockSpec((1, d), lambda i: (0, 0))
    return pl.pallas_call(
        body, name=name, grid=(t // tr,),
        out_shape=[jax.ShapeDtypeStruct((t, d), F32), jax.ShapeDtypeStruct((t, d), BF16),
                   jax.ShapeDtypeStruct((1, d), F32)],
        in_specs=[row, row, vec, row], out_specs=[row, row, vec],
        compiler_params=_params("arbitrary"),
    )(dh, x, gain.reshape(1, d), dres)


def _loss_head(x, gain, target, name):
    t, d = x.shape
    tr = _tile(t, ROW_TILE)
    steps = t // tr

    def body(x_ref, g_ref, t_ref, loss_ref, dx_ref, dxb_ref, dg_ref, sq_ref):
        i = pl.program_id(0)
        xhat, r = _normalise(x_ref[...])
        gain_v = g_ref[...]
        diff = xhat * gain_v - t_ref[...]
        dy = diff / float(d)
        dx = _rmsnorm_backward(dy, xhat, r, gain_v)
        dx_ref[...] = dx
        dxb_ref[...] = dx.astype(BF16)
        dg_part = jnp.sum(dy * xhat, axis=0, keepdims=True)
        sq_part = jnp.sum(diff * diff, axis=0, keepdims=True)

        @pl.when(i == 0)
        def _():
            dg_ref[...] = dg_part
            sq_ref[...] = sq_part

        @pl.when(i > 0)
        def _():
            dg_ref[...] += dg_part
            sq_ref[...] += sq_part

        @pl.when(i == steps - 1)
        def _():
            loss_ref[...] = (0.5 / float(d)) * jnp.sum(sq_ref[...], axis=1, keepdims=True)

    row = pl.BlockSpec((tr, d), lambda i: (i, 0))
    vec = pl.BlockSpec((1, d), lambda i: (0, 0))
    return pl.pallas_call(
        body, name=name, grid=(steps,),
        out_shape=[jax.ShapeDtypeStruct((1, 1), F32), jax.ShapeDtypeStruct((t, d), F32),
                   jax.ShapeDtypeStruct((t, d), BF16), jax.ShapeDtypeStruct((1, d), F32)],
        in_specs=[row, vec, row],
        out_specs=[pl.BlockSpec((1, 1), lambda i: (0, 0)), row, row, vec],
        scratch_shapes=[pltpu.VMEM((1, d), F32)],
        compiler_params=_params("arbitrary"),
    )(x, gain.reshape(1, d), target)


def _shift_down(ext, s):
    return pltpu.roll(ext, s, 0)


def _shift_up(ext, s):
    return pltpu.roll(ext, ext.shape[0] - s, 0)


def _window_sum(ext, w, shift):
    s = 1
    while s < w:
        ext = ext + shift(ext, s)
        s *= 2
    return ext


def _window_count(tile_index, rows, cols, w):
    t = tile_index * rows + lax.broadcasted_iota(jnp.int32, (rows, cols), 0)
    return jnp.minimum(t + 1, w).astype(F32)


def _mixer_sizes(proj, conv_w):
    t, e = proj.shape
    dc = conv_w.shape[1]
    dp = e - 3 * dc
    cg = dp // len(POOL_WINDOWS)
    tt = _tile(t, MIXER_TILE)
    assert tt % HALO_ROWS == 0 and tt >= HALO_ROWS
    cw = _tile(dc, cg)
    return t, e, dc, dp, cg, tt, cw


def _mixer_fwd(proj, conv_w, pool_w, pool_scale, name, deps=()):
    t, e, dc, dp, cg, tt, cw = _mixer_sizes(proj, conv_w)
    per_halo = tt // HALO_ROWS

    def body(cur_ref, prev_ref, cw_ref, pw_ref, ps_ref, y_ref):
        i = pl.program_id(0)
        first = i == 0

        def cur(lo, width):
            return cur_ref[:, lo:lo + width].astype(F32)

        def prev(lo, width):
            return jnp.where(first, 0.0, prev_ref[:, lo:lo + width].astype(F32))

        for lo in range(0, dc, cw):
            u = cur(dc + lo, cw) * cur(2 * dc + lo, cw)
            ext = jnp.concatenate([prev(dc + lo, cw) * prev(2 * dc + lo, cw), u], axis=0)
            u1 = _shift_down(ext, 1)[HALO_ROWS:]
            u2 = _shift_down(ext, 2)[HALO_ROWS:]
            conv = cw_ref[0:1, lo:lo + cw] * u2 + cw_ref[1:2, lo:lo + cw] * u1 + cw_ref[2:3, lo:lo + cw] * u
            y_ref[:, lo:lo + cw] = (cur(lo, cw) * conv).astype(BF16)

        for g, w in enumerate(POOL_WINDOWS):
            lo = 3 * dc + g * cg
            v = cur(lo, cg)
            ext = jnp.concatenate([prev(lo, cg), v], axis=0)
            mean = _window_sum(ext, w, _shift_down)[HALO_ROWS:] / _window_count(i, tt, cg, w)
            z = jnp.dot((mean - v).astype(BF16), pw_ref[g], preferred_element_type=F32)
            y_ref[:, dc + g * cg:dc + (g + 1) * cg] = (z * ps_ref[0:1, g * cg:(g + 1) * cg]).astype(BF16)

    in_specs = [pl.BlockSpec((tt, e), lambda i: (i, 0)),
                pl.BlockSpec((HALO_ROWS, e), lambda i: (jnp.maximum(i * per_halo - 1, 0), 0)),
                pl.BlockSpec((3, dc), lambda i: (0, 0)),
                pl.BlockSpec((len(POOL_WINDOWS), cg, cg), lambda i: (0, 0, 0)),
                pl.BlockSpec((1, dp), lambda i: (0, 0))]
    body, in_specs, operands = _run_after(
        deps, body, in_specs, [proj, proj, conv_w, pool_w, pool_scale.reshape(1, dp)])
    return pl.pallas_call(
        body, name=name, grid=(t // tt,),
        out_shape=jax.ShapeDtypeStruct((t, dc + dp), BF16),
        in_specs=in_specs,
        out_specs=pl.BlockSpec((tt, dc + dp), lambda i: (i, 0)),
        compiler_params=_params("parallel"),
    )(*operands)


def _mixer_bwd(proj, dy, conv_w, pool_w, pool_scale, name, deps=()):
    t, e, dc, dp, cg, tt, cw = _mixer_sizes(proj, conv_w)
    per_halo = tt // HALO_ROWS
    steps = t // tt
    n_groups = len(POOL_WINDOWS)

    def body(cur_ref, prev_ref, next_ref, dy_ref, dyn_ref, cw_ref, pw_ref, ps_ref,
             dp_ref, dcw_ref, dpw_ref, dps_ref):
        i = pl.program_id(0)
        first = i == 0
        last = i == steps - 1

        @pl.when(first)
        def _():
            dcw_ref[...] = jnp.zeros_like(dcw_ref)
            dpw_ref[...] = jnp.zeros_like(dpw_ref)
            dps_ref[...] = jnp.zeros_like(dps_ref)

        def cur(lo, width):
            return cur_ref[:, lo:lo + width].astype(F32)

        def prev(lo, width):
            return jnp.where(first, 0.0, prev_ref[:, lo:lo + width].astype(F32))

        def nxt(ref, lo, width):
            return jnp.where(last, 0.0, ref[:, lo:lo + width].astype(F32))

        def colsum(v):
            return jnp.sum(v, axis=0, keepdims=True)

        for lo in range(0, dc, cw):
            cols = slice(lo, lo + cw)
            b, c, xt = cur(lo, cw), cur(dc + lo, cw), cur(2 * dc + lo, cw)
            u = c * xt
            ext = jnp.concatenate([prev(dc + lo, cw) * prev(2 * dc + lo, cw), u], axis=0)
            u1 = _shift_down(ext, 1)[HALO_ROWS:]
            u2 = _shift_down(ext, 2)[HALO_ROWS:]
            w0, w1, w2 = cw_ref[0:1, cols], cw_ref[1:2, cols], cw_ref[2:3, cols]
            dyc = dy_ref[:, cols].astype(F32)
            dp_ref[:, cols] = (dyc * (w0 * u2 + w1 * u1 + w2 * u)).astype(BF16)
            dconv = dyc * b
            dcw_ref[0:1, cols] += colsum(dconv * u2)
            dcw_ref[1:2, cols] += colsum(dconv * u1)
            dcw_ref[2:3, cols] += colsum(dconv * u)
            dext = jnp.concatenate([dconv, nxt(dyn_ref, lo, cw) * nxt(next_ref, lo, cw)], axis=0)
            du = w2 * dconv + w1 * _shift_up(dext, 1)[:tt] + w0 * _shift_up(dext, 2)[:tt]
            dp_ref[:, dc + lo:dc + lo + cw] = (du * xt).astype(BF16)
            dp_ref[:, 2 * dc + lo:2 * dc + lo + cw] = (du * c).astype(BF16)

        for g, w in enumerate(POOL_WINDOWS):
            lo = 3 * dc + g * cg
            ycols = slice(dc + g * cg, dc + (g + 1) * cg)
            pcols = slice(g * cg, (g + 1) * cg)
            v = cur(lo, cg)
            ext = jnp.concatenate([prev(lo, cg), v], axis=0)
            count = _window_count(i, tt, cg, w)
            d = ((_window_sum(ext, w, _shift_down)[HALO_ROWS:] / count) - v).astype(BF16)
            pw = pw_ref[g]
            scale = ps_ref[0:1, pcols]
            dyp = dy_ref[:, ycols].astype(F32)
            z = jnp.dot(d, pw, preferred_element_type=F32)
            dps_ref[0:1, pcols] += colsum(dyp * z)
            dz = (dyp * scale).astype(BF16)
            dpw_ref[g] += lax.dot_general(d, dz, (((0,), (0,)), ((), ())), preferred_element_type=F32)
            dd = lax.dot_general(dz, pw, (((1,), (1,)), ((), ())), preferred_element_type=F32)
            dzn = (nxt(dyn_ref, dc + g * cg, cg) * scale).astype(BF16)
            ddn = lax.dot_general(dzn, pw, (((1,), (1,)), ((), ())), preferred_element_type=F32)
            qext = jnp.concatenate([dd / count, ddn / float(w)], axis=0)
            dp_ref[:, lo:lo + cg] = (_window_sum(qext, w, _shift_up)[:tt] - dd).astype(BF16)

    cur_spec = lambda width: pl.BlockSpec((tt, width), lambda i: (i, 0))
    prev_spec = pl.BlockSpec((HALO_ROWS, e), lambda i: (jnp.maximum(i * per_halo - 1, 0), 0))
    next_spec = lambda width: pl.BlockSpec(
        (HALO_ROWS, width), lambda i: (jnp.minimum((i + 1) * per_halo, t // HALO_ROWS - 1), 0))
    in_specs = [cur_spec(e), prev_spec, next_spec(e), cur_spec(dc + dp), next_spec(dc + dp),
                pl.BlockSpec((3, dc), lambda i: (0, 0)),
                pl.BlockSpec((n_groups, cg, cg), lambda i: (0, 0, 0)),
                pl.BlockSpec((1, dp), lambda i: (0, 0))]
    body, in_specs, operands = _run_after(
        deps, body, in_specs, [proj, proj, proj, dy, dy, conv_w, pool_w, pool_scale.reshape(1, dp)])
    return pl.pallas_call(
        body, name=name, grid=(steps,),
        out_shape=[jax.ShapeDtypeStruct((t, e), BF16), jax.ShapeDtypeStruct((3, dc), F32),
                   jax.ShapeDtypeStruct((n_groups, cg, cg), F32), jax.ShapeDtypeStruct((1, dp), F32)],
        in_specs=in_specs,
        out_specs=[cur_spec(e), pl.BlockSpec((3, dc), lambda i: (0, 0)),
                   pl.BlockSpec((n_groups, cg, cg), lambda i: (0, 0, 0)),
                   pl.BlockSpec((1, dp), lambda i: (0, 0))],
        compiler_params=_params("arbitrary"),
    )(*operands)


def _adamw(partials, w, m, v, me_slot, name, mine=()):
    n_layers, r, c = w.shape
    assert len(partials) == n_layers and len(mine) in (0, n_layers)
    n_mine = len(mine)
    tr = r if r * c <= ADAM_BLOCK_ELEMS else _tile(r, max(16, ADAM_BLOCK_ELEMS // c))

    def body(me_ref, *refs):
        own_refs, p_refs = refs[:n_mine], refs[n_mine:n_mine + n_layers]
        w_ref, m_ref, v_ref, g_out, d_out, m_out, v_out = refs[n_mine + n_layers:]
        for l in range(n_layers):
            g = own_refs[l][0].astype(F32) if n_mine else None
            for s in range(p_refs[l].shape[0]):
                part = p_refs[l][s].astype(F32)
                g = part if g is None else g + part
            m_new = ADAM_B1 * m_ref[l] + (1.0 - ADAM_B1) * g
            v_new = ADAM_B2 * v_ref[l] + (1.0 - ADAM_B2) * (g * g)
            m_hat = m_new / (1.0 - ADAM_B1 ** ADAM_STEP)
            v_hat = v_new / (1.0 - ADAM_B2 ** ADAM_STEP)
            g_out[l] = g
            d_out[l] = -ADAM_LR * (m_hat / (jnp.sqrt(v_hat) + ADAM_EPS) + ADAM_WD * w_ref[l])
            m_out[l] = m_new
            v_out[l] = v_new

    own_spec = pl.BlockSpec((1, tr, c), lambda i, me_ref: (me_ref[0], i, 0))
    p_specs = [pl.BlockSpec((p.shape[0], tr, c), lambda i, me_ref: (0, i, 0)) for p in partials]
    w_spec = pl.BlockSpec((n_layers, tr, c), lambda i, me_ref: (0, i, 0))
    return pl.pallas_call(
        body, name=name,
        grid_spec=pltpu.PrefetchScalarGridSpec(
            num_scalar_prefetch=1, grid=(r // tr,),
            in_specs=[own_spec] * n_mine + p_specs + [w_spec] * 3, out_specs=[w_spec] * 4),
        out_shape=[jax.ShapeDtypeStruct(w.shape, F32)] * 4,
        compiler_params=_params("parallel"),
    )(jnp.reshape(me_slot, (1,)).astype(jnp.int32), *mine, *partials, w, m, v)


def _to_bf16(w, first, last, name, deps=()):
    _, r, c = w.shape
    tr = r if r * c <= CAST_BLOCK_ELEMS else _tile(r, max(16, CAST_BLOCK_ELEMS // c))

    def body(w_ref, o_ref):
        o_ref[...] = w_ref[...].astype(BF16)

    in_specs = [pl.BlockSpec((1, tr, c), lambda l, i: (first + l, i, 0))]
    body, in_specs, operands = _run_after(deps, body, in_specs, [w])
    return pl.pallas_call(
        body, name=name, grid=(last - first, r // tr),
        out_shape=jax.ShapeDtypeStruct((last - first, r, c), BF16),
        in_specs=in_specs, out_specs=pl.BlockSpec((1, tr, c), lambda l, i: (l, i, 0)),
        compiler_params=_params("parallel", "parallel"),
    )(*operands)


def kernel(x, w_in, conv_w, pool_w, pool_scale, w_out, norm_mix, norm_mlp, w_up, w_down, norm_final, loss_target, m_w_in, m_conv_w, m_pool_w, m_pool_scale, m_w_out, m_norm_mix, m_norm_mlp, m_w_up, m_w_down, m_norm_final, v_w_in, v_conv_w, v_pool_w, v_pool_scale, v_w_out, v_norm_mix, v_norm_mlp, v_w_up, v_w_down, v_norm_final):
    n_layers, d, e_shard = w_in.shape
    t = x.shape[1]
    n_groups, cg_shard, cg = pool_w.shape[1:]
    dc_shard = conv_w.shape[2]
    dc, dp = dc_shard * N_DEV, n_groups * cg
    f_shard = w_up.shape[2]
    xs = x.reshape(t, d)
    target = loss_target.reshape(t, d)

    me = _slot(_mesh_position())
    big = (w_in, w_out, w_up, w_down)
    n_kinds = len(big)
    first_layer = [_to_bf16(w, 0, 1, "cast_first_layer")[0] for w in big]
    mixer_blocks = [conv_w.reshape(n_layers * 3, dc_shard), pool_w.reshape(n_layers * n_groups * cg_shard, cg)]
    groups = {}

    def start(tag, blocks, after):
        send_sems, recv_sems, blocks, lands, token = _gather_start(blocks, me, after, f"weights_start_{tag}")
        groups[tag] = (send_sems, recv_sems, blocks, lands)
        return token

    token = start("first", [first_layer[0], *mixer_blocks, *first_layer[1:]], xs)
    if n_layers > 1:
        others = [_to_bf16(w, 1, n_layers, "cast_other_layers", deps=(token,)) for w in big]
        token = start("rest", [others[k][l - 1] for l in range(1, n_layers) for k in range(n_kinds)], token)
    mixer_place = ("first", 1)

    def place(l, k):
        return ("first", 0 if k == 0 else k + len(mixer_blocks)) if l == 0 else ("rest", n_kinds * (l - 1) + k)

    def passed_on(where, n, after, name):
        tag, i = where
        lands = groups[tag][3]
        send_sems, recv_sems, lands[i:i + n] = _gather_pass_on(groups[tag][1], i, lands[i:i + n], after, f"weights_pass_{name}")
        return send_sems, recv_sems

    def gathered(where, n, passed_sems, after, name):
        tag, i = where
        first_send, first_recv, blocks, lands = groups[tag]
        lands[i:i + n] = _gather_finish((first_send, first_recv), i, passed_sems, blocks[i:i + n], lands[i:i + n], after,
                                        f"weights_finish_{name}")
        return lands[i:i + n]

    def landing(where):
        return groups[where[0]][3][where[1]]

    saved, weights = [], []
    xc = xs
    after = token
    for l in range(n_layers):
        if l == 0:
            h1 = _rmsnorm(xc, norm_mix[l], "norm_mix", deps=(after,))
            sems = passed_on(place(l, 0), 1, h1, f"{l}_0")
        else:
            sems = passed_on(place(l, 0), 1, after, f"{l}_0")
            h1 = _rmsnorm(xc, norm_mix[l], "norm_mix", deps=(landing(place(l, 0)),))
        win, = gathered(place(l, 0), 1, sems, h1, f"{l}_0")
        proj = _mm_nn(h1, win, out_dtype=BF16, name="in_proj")
        if l == 0:
            mixer_sems = passed_on(mixer_place, len(mixer_blocks), proj, "mixer")
        sems = passed_on(place(l, 1), 1, proj, f"{l}_1")
        if l == 0:
            conv_g, pool_g = gathered(mixer_place, len(mixer_blocks), mixer_sems, proj, "mixer")
            conv_full = conv_g.reshape(N_DEV, n_layers, 3, dc_shard).transpose(1, 2, 0, 3).reshape(n_layers, 3, dc)
            pool_full = pool_g.reshape(N_DEV, n_layers, n_groups, cg_shard, cg).transpose(1, 2, 0, 3, 4)
            pool_full = pool_full.reshape(n_layers, n_groups, cg, cg).astype(BF16)
        y = _mixer_fwd(proj, conv_full[l], pool_full[l], pool_scale[l], "mixer_fwd", deps=(landing(place(l, 1)),))
        wout = gathered(place(l, 1), 1, sems, y, f"{l}_1")[0].reshape(1, d, d)
        x1 = _mm_nn(y, wout, out_dtype=F32, res=xc, name="out_proj")
        sems = passed_on(place(l, 2), 1, x1, f"{l}_2")
        h2 = _rmsnorm(x1, norm_mlp[l], "norm_mlp", deps=(landing(place(l, 2)),))
        wup, = gathered(place(l, 2), 1, sems, h2, f"{l}_2")
        if l == 0:
            a, s = _mm_nn(h2, wup, out_dtype=BF16, with_relu2=True, name="mlp_up")
            sems = passed_on(place(l, 3), 1, a, f"{l}_3")
        else:
            sems = passed_on(place(l, 3), 1, h2, f"{l}_3")
            a, s = _mm_nn(h2, wup, out_dtype=BF16, with_relu2=True, name="mlp_up", deps=(landing(place(l, 3)),))
        wdown = gathered(place(l, 3), 1, sems, a, f"{l}_3")[0].reshape(1, f_shard * N_DEV, d)
        x2 = _mm_nn(s, wdown, out_dtype=F32, res=x1, name="mlp_down")
        weights.append((win, wout, wup, wdown))
        saved.append((xc, h1, proj, y, x1, h2, a, s))
        xc = after = x2

    loss_part, dx, dxb, g_norm_final = _loss_head(xc, norm_final, target, "loss_head")
    loss = lax.psum(loss_part[0, 0], MESH_AXES)

    in_flight = {}
    received = {}

    def push(l, arrs, keys, tag):
        send_sems, recv_sems, thru, lands, token = _exchange_start(arrs, f"grads_start_{tag}_{l}")
        in_flight.setdefault(l, []).append((send_sems, recv_sems, thru, lands, keys, tag))
        return token

    def land(l, after):
        for send_sems, recv_sems, thru, lands, keys, tag in in_flight.pop(l):
            thru, lands = _exchange_wait(send_sems, recv_sems, thru, lands, after, f"grads_wait_{tag}_{l}")
            for key, own, got in zip(keys, thru, lands):
                received[key, l] = (own, got)

    g_conv, g_scale, g_mix, g_mlp = ([None] * n_layers for _ in range(4))
    for l in reversed(range(n_layers)):
        win, wout, wup, wdown = weights[l]
        x0, h1, proj, y, x1, h2, a, s = saved[l]
        da = _mm_nt(dxb, wdown, out_dtype=BF16, relu2_grad_of=a, name="mlp_down_dx")
        gw_down = _mm_tn(s, dxb, n_blocks=1, name="mlp_down_dw")
        gw_up = _mm_tn(h2, da, n_blocks=N_DEV, name="mlp_up_dw")
        tok_mlp = push(l, [gw_down.reshape(N_DEV, f_shard, d), gw_up], ["w_down", "w_up"], "mlp")
        dh2 = _mm_nt(da, wup, out_dtype=F32, name="mlp_up_dx", deps=(tok_mlp,))
        dx1, dx1b, g_mlp[l] = _rmsnorm_bwd(dh2, x1, norm_mlp[l], dx, "norm_mlp_bwd")
        dy = _mm_nt(dx1b, wout, out_dtype=BF16, name="out_proj_dx")
        gw_out = _mm_tn(y, dx1b, n_blocks=1, name="out_proj_dw")
        tok_out = push(l, [gw_out.reshape(N_DEV, d // N_DEV, d)], ["w_out"], "out")
        dproj, g_conv[l], gw_pool, g_scale[l] = _mixer_bwd(
            proj, dy, conv_full[l], pool_full[l], pool_scale[l], "mixer_bwd", deps=(tok_out,))
        gw_in = _mm_tn(h1, dproj, n_blocks=N_DEV, name="in_proj_dw")
        gw_pool = gw_pool.reshape(n_groups, N_DEV, cg_shard, cg).transpose(1, 0, 2, 3)
        tok_in = push(l, [gw_in, gw_pool.reshape(N_DEV, n_groups * cg_shard, cg).astype(BF16)], ["w_in", "pool_w"], "in")
        dh1 = _mm_nt(dproj, win, out_dtype=F32, name="in_proj_dx", deps=(tok_in,))
        dx, dxb, g_mix[l] = _rmsnorm_bwd(dh1, x0, norm_mix[l], dx1, "norm_mix_bwd")
        if l + 1 < n_layers:
            land(l + 1, dx)
    land(0, dx)
    grad_x = dx.reshape(x.shape)

    small = _all_gather(
        [jnp.stack(g_conv).reshape(n_layers * 3, dc), jnp.concatenate(g_scale, axis=0),
         jnp.concatenate(g_mix, axis=0), jnp.concatenate(g_mlp, axis=0), g_norm_final], "gather_small_grads")
    conv_parts = lax.dynamic_slice_in_dim(small[0], me * dc_shard, dc_shard, axis=2)

    def update(partials, w, m, v, name, mine=()):
        shape = w.shape
        rc = (shape[0], -1, shape[-1]) if w.ndim > 2 else (1, *shape) if w.ndim == 2 else (1, 1, *shape)
        outs = _adamw(partials, w.reshape(rc), m.reshape(rc), v.reshape(rc), me, name, mine=mine)
        return [o.reshape(shape) for o in outs]

    def exchanged(key, w, m, v):
        return update([received[key, l][1] for l in range(n_layers)], w, m, v, f"adamw_{key}",
                      mine=[received[key, l][0] for l in range(n_layers)])

    results = {
        "w_in": exchanged("w_in", w_in, m_w_in, v_w_in),
        "conv_w": update([conv_parts], conv_w.reshape(n_layers * 3, dc_shard), m_conv_w.reshape(n_layers * 3, dc_shard),
                         v_conv_w.reshape(n_layers * 3, dc_shard), "adamw_conv_w"),
        "pool_w": exchanged("pool_w", pool_w, m_pool_w, v_pool_w),
        "pool_scale": update([small[1]], pool_scale, m_pool_scale, v_pool_scale, "adamw_pool_scale"),
        "w_out": exchanged("w_out", w_out, m_w_out, v_w_out),
        "norm_mix": update([small[2]], norm_mix, m_norm_mix, v_norm_mix, "adamw_norm_mix"),
        "norm_mlp": update([small[3]], norm_mlp, m_norm_mlp, v_norm_mlp, "adamw_norm_mlp"),
        "w_up": exchanged("w_up", w_up, m_w_up, v_w_up),
        "w_down": exchanged("w_down", w_down, m_w_down, v_w_down),
        "norm_final": update([small[4]], norm_final, m_norm_final, v_norm_final, "adamw_norm_final"),
    }
    results["conv_w"] = [o.reshape(conv_w.shape) for o in results["conv_w"]]
    order = ("w_in", "conv_w", "pool_w", "pool_scale", "w_out", "norm_mix", "norm_mlp", "w_up", "w_down", "norm_final")
    return (loss, grad_x, *[results[k][0] for k in order], *[results[k][1] for k in order],
            *[results[k][2] for k in order], *[results[k][3] for k in order])
```

```python
import functools

import jax
import jax.numpy as jnp
from jax import lax
from jax.experimental import pallas as pl
from jax.experimental.pallas import tpu as pltpu

F32 = jnp.float32
BF16 = jnp.bfloat16

N_DEV = 8
MESH_AXES = ("x", "y", "c")
NORM_EPS = 1e-6
POOL_WINDOWS = (2, 4, 8, 16)
HALO_ROWS = 16

ADAM_LR = 0.001
ADAM_B1 = 0.9
ADAM_B2 = 0.999
ADAM_EPS = 1e-08
ADAM_WD = 0.01
ADAM_STEP = 10

VMEM_BYTES_V7X = 64 * 1024 * 1024
VMEM_LIMIT = (VMEM_BYTES_V7X * 3) // 4

MM_TILE = 1024
MM_TILE_K = 2048
ROW_TILE = 256
MIXER_TILE = 512
ADAM_BLOCK_ELEMS = 64 * 1024
CAST_BLOCK_ELEMS = 512 * 1024


def _params(*semantics):
    return pltpu.CompilerParams(dimension_semantics=semantics, vmem_limit_bytes=VMEM_LIMIT)


def _tile(dim, pref):
    t = min(dim, pref)
    assert dim % t == 0, (dim, pref)
    return t


def _mesh_position():
    return lax.axis_index("x"), lax.axis_index("y"), lax.axis_index("c")


def _slot(p):
    return 4 * p[0] + 2 * p[1] + p[2]


def _all_gather(arrs, name):
    n = len(arrs)

    def body(*refs):
        ins, outs = refs[:n], refs[n:2 * n]
        send_sems, recv_sems, local_sems = refs[2 * n:]
        x, y, c = _mesh_position()
        me, sibling = (x, y, c), (x, y, 1 - c)
        chips = [(1 - x, y), (x, 1 - y), (1 - x, 1 - y)]

        def copy(a, k, block, to, src=None):
            dst = outs[a].at[_slot(block)]
            return pltpu.make_async_remote_copy(
                src_ref=dst if src is None else src, dst_ref=dst,
                send_sem=send_sems.at[a, k], recv_sem=recv_sems.at[a, k],
                device_id=to, device_id_type=pl.DeviceIdType.MESH)

        mine = [pltpu.make_async_copy(ins[a], outs[a].at[_slot(me)], local_sems.at[a]) for a in range(n)]
        for cp in mine:
            cp.start()
        first = []
        for a in range(n):
            first.append(copy(a, 0, me, sibling, src=ins[a]))
            first += [copy(a, 1 + j, me, (*chip, c), src=ins[a]) for j, chip in enumerate(chips)]
        for cp in first:
            cp.start()
        passed = []
        for j, chip in enumerate(chips):
            for a in range(n):
                copy(a, 1 + j, (*chip, c), me).wait_recv()
                cp = copy(a, 4 + j, (*chip, c), sibling)
                cp.start()
                passed.append(cp)
        for a in range(n):
            copy(a, 0, sibling, me).wait_recv()
            for j, chip in enumerate(chips):
                copy(a, 4 + j, (*chip, 1 - c), me).wait_recv()
        for cp in first + passed:
            cp.wait_send()
        for cp in mine:
            cp.wait()

    any_spec = pl.BlockSpec(memory_space=pl.ANY)
    return pl.pallas_call(
        body, name=name,
        out_shape=[jax.ShapeDtypeStruct((N_DEV, *a.shape), a.dtype) for a in arrs],
        in_specs=[any_spec] * n, out_specs=[any_spec] * n,
        scratch_shapes=[pltpu.SemaphoreType.DMA((n, 7)), pltpu.SemaphoreType.DMA((n, 7)),
                        pltpu.SemaphoreType.DMA((n,))],
    )(*arrs)


def _peer(k):
    x, y, c = _mesh_position()
    return (1 - x if k & 4 else x, 1 - y if k & 2 else y, 1 - c if k & 1 else c)


_HBM = pl.BlockSpec(memory_space=pltpu.HBM)
_SEM = pl.BlockSpec(memory_space=pltpu.SEMAPHORE)
_EFFECT = pltpu.SideEffectType.DATAFLOW_SIDE_EFFECTING


def _exchange_copy(g_ref, land_ref, send_sems, recv_sems, a, k):
    return pltpu.make_async_remote_copy(
        src_ref=g_ref.at[_slot(_peer(k))], dst_ref=land_ref.at[k - 1],
        send_sem=send_sems.at[a * (N_DEV - 1) + k - 1], recv_sem=recv_sems.at[a * (N_DEV - 1) + k - 1],
        device_id=_peer(k), device_id_type=pl.DeviceIdType.MESH)


def _exchange_start(arrs, name):
    n = len(arrs)

    def body(*refs):
        g_refs, land_refs = refs[:n], refs[n:2 * n]
        send_sems, recv_sems = refs[2 * n:2 * n + 2]
        token = refs[-1]
        for k in range(1, N_DEV):
            for a in range(n):
                _exchange_copy(g_refs[a], land_refs[a], send_sems, recv_sems, a, k).start()
        token[...] = jnp.zeros_like(token)

    lands = [lax.empty((N_DEV - 1, *g.shape[1:]), g.dtype) for g in arrs]
    outs = pl.pallas_call(
        body, name=name,
        out_shape=(pltpu.SemaphoreType.DMA((n * (N_DEV - 1),)), pltpu.SemaphoreType.DMA((n * (N_DEV - 1),)),
                   *[pltpu.HBM(g.shape, g.dtype) for g in arrs], *[pltpu.HBM(z.shape, z.dtype) for z in lands],
                   jax.ShapeDtypeStruct((8, 128), F32)),
        in_specs=[_HBM] * (2 * n),
        out_specs=(_SEM, _SEM, *[_HBM] * (2 * n), pl.BlockSpec(memory_space=pltpu.VMEM)),
        input_output_aliases={i: 2 + i for i in range(2 * n)},
        compiler_params=pltpu.CompilerParams(has_side_effects=_EFFECT),
    )(*[pltpu.with_memory_space_constraint(g, pltpu.HBM) for g in arrs],
      *[pltpu.with_memory_space_constraint(z, pltpu.HBM) for z in lands])
    return outs[0], outs[1], list(outs[2:2 + n]), list(outs[2 + n:2 + 2 * n]), outs[-1]


def _exchange_wait(send_sems, recv_sems, arrs, lands, after, name):
    n = len(arrs)

    def body(*refs):
        g_refs, land_refs = refs[:n], refs[n:2 * n]
        send_sems_ref, recv_sems_ref = refs[2 * n:2 * n + 2]
        for k in range(1, N_DEV):
            for a in range(n):
                cp = _exchange_copy(g_refs[a], land_refs[a], send_sems_ref, recv_sems_ref, a, k)
                cp.wait_send()
                cp.wait_recv()

    outs = pl.pallas_call(
        body, name=name,
        out_shape=(*[pltpu.HBM(g.shape, g.dtype) for g in arrs], *[pltpu.HBM(z.shape, z.dtype) for z in lands]),
        in_specs=[_HBM] * (2 * n) + [_SEM, _SEM, pl.BlockSpec(memory_space=pl.ANY)],
        out_specs=[_HBM] * (2 * n),
        input_output_aliases={i: i for i in range(2 * n)},
        compiler_params=pltpu.CompilerParams(has_side_effects=_EFFECT),
    )(*arrs, *lands, send_sems, recv_sems, after)
    return list(outs[:n]), list(outs[n:])


N_FIRST = 4
N_PASSED = 3


def _other_chips():
    x, y, _ = _mesh_position()
    return [(1 - x, y), (x, 1 - y), (1 - x, 1 - y)]


def _gather_copy(src_ref, land_ref, block, to, send_sem, recv_sem):
    rows = land_ref.at[_slot(block)]
    return pltpu.make_async_remote_copy(
        src_ref=rows if src_ref is None else src_ref, dst_ref=rows, send_sem=send_sem, recv_sem=recv_sem,
        device_id=to, device_id_type=pl.DeviceIdType.MESH)


def _gather_start(arrs, me_slot, after, name):
    n = len(arrs)

    def body(*refs):
        src_refs, land_refs = refs[:n], refs[n:2 * n]
        send_sems, recv_sems = refs[2 * n + 1:2 * n + 3]
        token = refs[-1]
        x, y, c = _mesh_position()
        targets = [(x, y, 1 - c)] + [(*chip, c) for chip in _other_chips()]
        for a in range(n):
            for k, to in enumerate(targets):
                _gather_copy(src_refs[a], land_refs[a], (x, y, c), to,
                             send_sems.at[a * N_FIRST + k], recv_sems.at[a * N_FIRST + k]).start()
        token[...] = jnp.zeros_like(token)

    lands = [lax.dynamic_update_slice_in_dim(lax.empty((N_DEV, *a.shape), a.dtype), a[None], me_slot, axis=0)
             for a in arrs]
    outs = pl.pallas_call(
        body, name=name,
        out_shape=(pltpu.SemaphoreType.DMA((n * N_FIRST,)), pltpu.SemaphoreType.DMA((n * N_FIRST,)),
                   *[pltpu.HBM(a.shape, a.dtype) for a in arrs], *[pltpu.HBM(z.shape, z.dtype) for z in lands],
                   jax.ShapeDtypeStruct((8, 128), F32)),
        in_specs=[_HBM] * (2 * n) + [pl.BlockSpec(memory_space=pl.ANY)],
        out_specs=(_SEM, _SEM, *[_HBM] * (2 * n), pl.BlockSpec(memory_space=pltpu.VMEM)),
        input_output_aliases={i: 2 + i for i in range(2 * n)},
        compiler_params=pltpu.CompilerParams(has_side_effects=_EFFECT),
    )(*[pltpu.with_memory_space_constraint(a, pltpu.HBM) for a in arrs],
      *[pltpu.with_memory_space_constraint(z, pltpu.HBM) for z in lands], after)
    return outs[0], outs[1], list(outs[2:2 + n]), list(outs[2 + n:2 + 2 * n]), outs[-1]


def _gather_pass_on(first_recv_sems, base, lands, after, name):
    n = len(lands)

    def body(*refs):
        land_refs = refs[:n]
        first_recv = refs[n]
        send_sems, recv_sems = refs[n + 2:n + 4]
        x, y, c = _mesh_position()
        sibling = (x, y, 1 - c)
        for j, chip in enumerate(_other_chips()):
            for a in range(n):
                _gather_copy(None, land_refs[a], (*chip, c), sibling,
                             send_sems.at[a * N_PASSED + j], first_recv.at[(base + a) * N_FIRST + 1 + j]).wait_recv()
                _gather_copy(None, land_refs[a], (*chip, c), sibling,
                             send_sems.at[a * N_PASSED + j], recv_sems.at[a * N_PASSED + j]).start()

    outs = pl.pallas_call(
        body, name=name,
        out_shape=(pltpu.SemaphoreType.DMA((n * N_PASSED,)), pltpu.SemaphoreType.DMA((n * N_PASSED,)),
                   *[pltpu.HBM(z.shape, z.dtype) for z in lands]),
        in_specs=[_HBM] * n + [_SEM, pl.BlockSpec(memory_space=pl.ANY)],
        out_specs=(_SEM, _SEM, *[_HBM] * n),
        input_output_aliases={i: 2 + i for i in range(n)},
        compiler_params=pltpu.CompilerParams(has_side_effects=_EFFECT),
    )(*lands, first_recv_sems, after)
    return outs[0], outs[1], list(outs[2:])


def _gather_finish(first_sems, base, passed_sems, arrs, lands, after, name):
    n = len(lands)

    def body(*refs):
        src_refs, land_refs = refs[:n], refs[n:2 * n]
        first_send, first_recv, passed_send, passed_recv = refs[2 * n:2 * n + 4]
        x, y, c = _mesh_position()
        sibling = (x, y, 1 - c)
        chips = _other_chips()
        for a in range(n):
            _gather_copy(src_refs[a], land_refs[a], sibling, sibling,
                         first_send.at[(base + a) * N_FIRST], first_recv.at[(base + a) * N_FIRST]).wait_recv()
            for j, chip in enumerate(chips):
                _gather_copy(None, land_refs[a], (*chip, 1 - c), sibling,
                             passed_send.at[a * N_PASSED + j], passed_recv.at[a * N_PASSED + j]).wait_recv()
        for a in range(n):
            for k in range(N_FIRST):
                _gather_copy(src_refs[a], land_refs[a], (x, y, c), sibling,
                             first_send.at[(base + a) * N_FIRST + k], first_recv.at[(base + a) * N_FIRST + k]).wait_send()
            for j, chip in enumerate(chips):
                _gather_copy(None, land_refs[a], (*chip, c), sibling,
                             passed_send.at[a * N_PASSED + j], passed_recv.at[a * N_PASSED + j]).wait_send()

    outs = pl.pallas_call(
        body, name=name,
        out_shape=(*[pltpu.HBM(a.shape, a.dtype) for a in arrs], *[pltpu.HBM(z.shape, z.dtype) for z in lands]),
        in_specs=[_HBM] * (2 * n) + [_SEM] * 4 + [pl.BlockSpec(memory_space=pl.ANY)],
        out_specs=[_HBM] * (2 * n),
        input_output_aliases={i: i for i in range(2 * n)},
        compiler_params=pltpu.CompilerParams(has_side_effects=_EFFECT),
    )(*arrs, *lands, *first_sems, *passed_sems, after)
    return list(outs[n:])


def _run_after(deps, body, in_specs, operands):
    n_deps = len(deps)
    if n_deps == 0:
        return body, in_specs, operands

    def body_behind(*refs):
        body(*refs[n_deps:])

    return body_behind, [pl.BlockSpec(memory_space=pl.ANY)] * n_deps + list(in_specs), list(deps) + list(operands)


def _zero_at_first(acc_ref, k, nk):
    if nk > 1:
        @pl.when(k == 0)
        def _():
            acc_ref[...] = jnp.zeros_like(acc_ref)


def _accumulate(acc_ref, product, k, nk, finish):
    if nk == 1:
        finish(product())
        return

    @pl.when(k < nk - 1)
    def _():
        acc_ref[...] += product()

    @pl.when(k == nk - 1)
    def _():
        finish(acc_ref[...] + product())


def _blocks_per_tile(n_blocks, width, pref):
    if width >= pref:
        return 1
    per = min(n_blocks, pref // width)
    assert n_blocks % per == 0
    return per


def _mm_nn(a, b3, *, out_dtype, name, res=None, with_relu2=False, deps=()):
    m, kdim = a.shape
    nb, kb_, nw = b3.shape
    assert kb_ == kdim
    per_tile = _blocks_per_tile(nb, nw, MM_TILE)
    tw = _tile(nw, MM_TILE)
    per_block = nw // tw
    tn = per_tile * tw
    tm, tk = _tile(m, MM_TILE), _tile(kdim, MM_TILE_K)
    nk = kdim // tk
    n_in = 2 + (res is not None)

    def body(*refs):
        a_ref, b_ref = refs[:2]
        r_ref = refs[2] if res is not None else None
        o_ref = refs[n_in]
        s_ref = refs[n_in + 1] if with_relu2 else None
        acc_ref = refs[-1] if nk > 1 else None
        k = pl.program_id(2)
        _zero_at_first(acc_ref, k, nk)
        for b in range(per_tile):
            cols = slice(b * tw, (b + 1) * tw)

            def product(b=b):
                return jnp.dot(a_ref[...], b_ref[b], preferred_element_type=F32)

            def finish(total, cols=cols):
                if r_ref is not None:
                    total = total + r_ref[:, cols]
                o_ref[:, cols] = total.astype(out_dtype)
                if s_ref is not None:
                    r = jnp.maximum(total, 0.0)
                    s_ref[:, cols] = (r * r).astype(BF16)

            _accumulate(acc_ref.at[:, cols] if nk > 1 else None, product, k, nk, finish)

    if per_tile > 1:
        b_spec = pl.BlockSpec((per_tile, tk, nw), lambda i, j, k: (j, k, 0))
    else:
        b_spec = pl.BlockSpec((1, tk, tw), lambda i, j, k: (j // per_block, k, j % per_block))
    in_specs = [pl.BlockSpec((tm, tk), lambda i, j, k: (i, k)), b_spec]
    operands = [a, b3]
    if res is not None:
        in_specs.append(pl.BlockSpec((tm, tn), lambda i, j, k: (i, j)))
        operands.append(res)
    body, in_specs, operands = _run_after(deps, body, in_specs, operands)
    o_spec = pl.BlockSpec((tm, tn), lambda i, j, k: (i, j))
    out_shape = [jax.ShapeDtypeStruct((m, nb * nw), out_dtype)]
    if with_relu2:
        out_shape.append(jax.ShapeDtypeStruct((m, nb * nw), BF16))
    outs = pl.pallas_call(
        body, name=name, grid=(m // tm, (nb * nw) // tn, nk),
        out_shape=out_shape, in_specs=in_specs, out_specs=[o_spec] * len(out_shape),
        scratch_shapes=[pltpu.VMEM((tm, tn), F32)] if nk > 1 else [],
        compiler_params=_params("parallel", "parallel", "arbitrary"),
    )(*operands)
    return outs if with_relu2 else outs[0]


def _mm_nt(a, b3, *, out_dtype, name, relu2_grad_of=None, deps=()):
    m, kdim = a.shape
    kb, n, kw = b3.shape
    assert kb * kw == kdim
    per_step = _blocks_per_tile(kb, kw, MM_TILE_K)
    tw = _tile(kw, MM_TILE_K)
    per_block = kw // tw
    tk = per_step * tw
    tm, tn = _tile(m, MM_TILE), _tile(n, MM_TILE)
    nk = kdim // tk

    def body(*refs):
        if relu2_grad_of is None:
            a_ref, b_ref, o_ref = refs[:3]
            g_ref = None
        else:
            a_ref, b_ref, g_ref, o_ref = refs[:4]
        acc_ref = refs[-1] if nk > 1 else None
        k = pl.program_id(2)
        _zero_at_first(acc_ref, k, nk)
        def product():
            p = None
            for b in range(per_step):
                part = lax.dot_general(a_ref[:, b * tw:(b + 1) * tw], b_ref[b], (((1,), (1,)), ((), ())),
                                       preferred_element_type=F32)
                p = part if p is None else p + part
            return p

        def finish(total):
            if g_ref is not None:
                total = total * (2.0 * jnp.maximum(g_ref[...].astype(F32), 0.0))
            o_ref[...] = total.astype(out_dtype)

        _accumulate(acc_ref, product, k, nk, finish)

    if per_step > 1:
        b_spec = pl.BlockSpec((per_step, tn, kw), lambda i, j, k: (k, j, 0))
    else:
        b_spec = pl.BlockSpec((1, tn, tw), lambda i, j, k: (k // per_block, j, k % per_block))
    in_specs = [pl.BlockSpec((tm, tk), lambda i, j, k: (i, k)), b_spec]
    operands = [a, b3]
    if relu2_grad_of is not None:
        in_specs.append(pl.BlockSpec((tm, tn), lambda i, j, k: (i, j)))
        operands.append(relu2_grad_of)
    body, in_specs, operands = _run_after(deps, body, in_specs, operands)
    return pl.pallas_call(
        body, name=name, grid=(m // tm, n // tn, nk),
        out_shape=jax.ShapeDtypeStruct((m, n), out_dtype),
        in_specs=in_specs, out_specs=pl.BlockSpec((tm, tn), lambda i, j, k: (i, j)),
        scratch_shapes=[pltpu.VMEM((tm, tn), F32)] if nk > 1 else [],
        compiler_params=_params("parallel", "parallel", "arbitrary"),
    )(*operands)


def _mm_tn(a, b, *, n_blocks, name, deps=()):
    t, m = a.shape
    t2, n = b.shape
    assert t == t2 and n % n_blocks == 0
    nw = n // n_blocks
    per_tile = _blocks_per_tile(n_blocks, nw, MM_TILE)
    tw = _tile(nw, MM_TILE)
    per_block = nw // tw
    tn = per_tile * tw
    tm, tk = _tile(m, MM_TILE), _tile(t, MM_TILE_K)
    nk = t // tk

    def body(a_ref, b_ref, o_ref, *scratch):
        acc_ref = scratch[0] if nk > 1 else None
        k = pl.program_id(2)
        _zero_at_first(acc_ref, k, nk)
        def product():
            return lax.dot_general(a_ref[...], b_ref[...], (((0,), (0,)), ((), ())), preferred_element_type=F32)

        def finish(total):
            for b in range(per_tile):
                o_ref[b] = total[:, b * tw:(b + 1) * tw].astype(BF16)

        _accumulate(acc_ref, product, k, nk, finish)

    if per_tile > 1:
        o_spec = pl.BlockSpec((per_tile, tm, nw), lambda i, j, k: (j, i, 0))
    else:
        o_spec = pl.BlockSpec((1, tm, tw), lambda i, j, k: (j // per_block, i, j % per_block))
    in_specs = [pl.BlockSpec((tk, tm), lambda i, j, k: (k, i)), pl.BlockSpec((tk, tn), lambda i, j, k: (k, j))]
    body, in_specs, operands = _run_after(deps, body, in_specs, [a, b])
    return pl.pallas_call(
        body, name=name, grid=(m // tm, n // tn, nk),
        out_shape=jax.ShapeDtypeStruct((n_blocks, m, nw), BF16),
        in_specs=in_specs, out_specs=o_spec,
        scratch_shapes=[pltpu.VMEM((tm, tn), F32)] if nk > 1 else [],
        compiler_params=_params("parallel", "parallel", "arbitrary"),
    )(*operands)


def _normalise(x):
    r = lax.rsqrt(jnp.mean(x * x, axis=-1, keepdims=True) + NORM_EPS)
    return x * r, r


def _rmsnorm_backward(dh, xhat, r, gain):
    dxhat = dh * gain
    return r * (dxhat - xhat * jnp.mean(dxhat * xhat, axis=-1, keepdims=True))


def _rmsnorm(x, gain, name, deps=()):
    t, d = x.shape
    tr = _tile(t, ROW_TILE)

    def body(x_ref, g_ref, o_ref):
        xhat, _ = _normalise(x_ref[...])
        o_ref[...] = (xhat * g_ref[...]).astype(BF16)

    in_specs = [pl.BlockSpec((tr, d), lambda i: (i, 0)), pl.BlockSpec((1, d), lambda i: (0, 0))]
    body, in_specs, operands = _run_after(deps, body, in_specs, [x, gain.reshape(1, d)])
    return pl.pallas_call(
        body, name=name, grid=(t // tr,),
        out_shape=jax.ShapeDtypeStruct((t, d), BF16),
        in_specs=in_specs,
        out_specs=pl.BlockSpec((tr, d), lambda i: (i, 0)),
        compiler_params=_params("parallel"),
    )(*operands)


def _rmsnorm_bwd(dh, x, gain, dres, name):
    t, d = x.shape
    tr = _tile(t, ROW_TILE)

    def body(dh_ref, x_ref, g_ref, r_ref, dx_ref, dxb_ref, dg_ref):
        xhat, r = _normalise(x_ref[...])
        dh_v = dh_ref[...]
        dx = r_ref[...] + _rmsnorm_backward(dh_v, xhat, r, g_ref[...])
        dx_ref[...] = dx
        dxb_ref[...] = dx.astype(BF16)
        part = jnp.sum(dh_v * xhat, axis=0, keepdims=True)

        @pl.when(pl.program_id(0) == 0)
        def _():
            dg_ref[...] = part

        @pl.when(pl.program_id(0) > 0)
        def _():
            dg_ref[...] += part

    row = pl.BlockSpec((tr, d), lambda i: (i, 0))
    vec = pl.BlockSpec((1, d), lambda i: (0, 0))
    return pl.pallas_call(
        body, name=name, grid=(t // tr,),
        out_shape=[jax.ShapeDtypeStruct((t, d), F32), jax.ShapeDtypeStruct((t, d), BF16),
                   jax.ShapeDtypeStruct((1, d), F32)],
        in_specs=[row, row, vec, row], out_specs=[row, row, vec],
        compiler_params=_params("arbitrary"),
    )(dh, x, gain.reshape(1, d), dres)


def _loss_head(x, gain, target, name):
    t, d = x.shape
    tr = _tile(t, ROW_TILE)
    steps = t // tr

    def body(x_ref, g_ref, t_ref, loss_ref, dx_ref, dxb_ref, dg_ref, sq_ref):
        i = pl.program_id(0)
        xhat, r = _normalise(x_ref[...])
        gain_v = g_ref[...]
        diff = xhat * gain_v - t_ref[...]
        dy = diff / float(d)
        dx = _rmsnorm_backward(dy, xhat, r, gain_v)
        dx_ref[...] = dx
        dxb_ref[...] = dx.astype(BF16)
        dg_part = jnp.sum(dy * xhat, axis=0, keepdims=True)
        sq_part = jnp.sum(diff * diff, axis=0, keepdims=True)

        @pl.when(i == 0)
        def _():
            dg_ref[...] = dg_part
            sq_ref[...] = sq_part

        @pl.when(i > 0)
        def _():
            dg_ref[...] += dg_part
            sq_ref[...] += sq_part

        @pl.when(i == steps - 1)
        def _():
            loss_ref[...] = (0.5 / float(d)) * jnp.sum(sq_ref[...], axis=1, keepdims=True)

    row = pl.BlockSpec((tr, d), lambda i: (i, 0))
    vec = pl.BlockSpec((1, d), lambda i: (0, 0))
    return pl.pallas_call(
        body, name=name, grid=(steps,),
        out_shape=[jax.ShapeDtypeStruct((1, 1), F32), jax.ShapeDtypeStruct((t, d), F32),
                   jax.ShapeDtypeStruct((t, d), BF16), jax.ShapeDtypeStruct((1, d), F32)],
        in_specs=[row, vec, row],
        out_specs=[pl.BlockSpec((1, 1), lambda i: (0, 0)), row, row, vec],
        scratch_shapes=[pltpu.VMEM((1, d), F32)],
        compiler_params=_params("arbitrary"),
    )(x, gain.reshape(1, d), target)


def _shift_down(ext, s):
    return pltpu.roll(ext, s, 0)


def _shift_up(ext, s):
    return pltpu.roll(ext, ext.shape[0] - s, 0)


def _window_sum(ext, w, shift):
    s = 1
    while s < w:
        ext = ext + shift(ext, s)
        s *= 2
    return ext


def _window_count(tile_index, rows, cols, w):
    t = tile_index * rows + lax.broadcasted_iota(jnp.int32, (rows, cols), 0)
    return jnp.minimum(t + 1, w).astype(F32)


def _mixer_sizes(proj, conv_w):
    t, e = proj.shape
    dc = conv_w.shape[1]
    dp = e - 3 * dc
    cg = dp // len(POOL_WINDOWS)
    tt = _tile(t, MIXER_TILE)
    assert tt % HALO_ROWS == 0 and tt >= HALO_ROWS
    cw = _tile(dc, cg)
    return t, e, dc, dp, cg, tt, cw


def _mixer_fwd(proj, conv_w, pool_w, pool_scale, name, deps=()):
    t, e, dc, dp, cg, tt, cw = _mixer_sizes(proj, conv_w)
    per_halo = tt // HALO_ROWS

    def body(cur_ref, prev_ref, cw_ref, pw_ref, ps_ref, y_ref):
        i = pl.program_id(0)
        first = i == 0

        def cur(lo, width):
            return cur_ref[:, lo:lo + width].astype(F32)

        def prev(lo, width):
            return jnp.where(first, 0.0, prev_ref[:, lo:lo + width].astype(F32))

        for lo in range(0, dc, cw):
            u = cur(dc + lo, cw) * cur(2 * dc + lo, cw)
            ext = jnp.concatenate([prev(dc + lo, cw) * prev(2 * dc + lo, cw), u], axis=0)
            u1 = _shift_down(ext, 1)[HALO_ROWS:]
            u2 = _shift_down(ext, 2)[HALO_ROWS:]
            conv = cw_ref[0:1, lo:lo + cw] * u2 + cw_ref[1:2, lo:lo + cw] * u1 + cw_ref[2:3, lo:lo + cw] * u
            y_ref[:, lo:lo + cw] = (cur(lo, cw) * conv).astype(BF16)

        for g, w in enumerate(POOL_WINDOWS):
            lo = 3 * dc + g * cg
            v = cur(lo, cg)
            ext = jnp.concatenate([prev(lo, cg), v], axis=0)
            mean = _window_sum(ext, w, _shift_down)[HALO_ROWS:] / _window_count(i, tt, cg, w)
            z = jnp.dot((mean - v).astype(BF16), pw_ref[g], preferred_element_type=F32)
            y_ref[:, dc + g * cg:dc + (g + 1) * cg] = (z * ps_ref[0:1, g * cg:(g + 1) * cg]).astype(BF16)

    in_specs = [pl.BlockSpec((tt, e), lambda i: (i, 0)),
                pl.BlockSpec((HALO_ROWS, e), lambda i: (jnp.maximum(i * per_halo - 1, 0), 0)),
                pl.BlockSpec((3, dc), lambda i: (0, 0)),
                pl.BlockSpec((len(POOL_WINDOWS), cg, cg), lambda i: (0, 0, 0)),
                pl.BlockSpec((1, dp), lambda i: (0, 0))]
    body, in_specs, operands = _run_after(
        deps, body, in_specs, [proj, proj, conv_w, pool_w, pool_scale.reshape(1, dp)])
    return pl.pallas_call(
        body, name=name, grid=(t // tt,),
        out_shape=jax.ShapeDtypeStruct((t, dc + dp), BF16),
        in_specs=in_specs,
        out_specs=pl.BlockSpec((tt, dc + dp), lambda i: (i, 0)),
        compiler_params=_params("parallel"),
    )(*operands)


def _mixer_bwd(proj, dy, conv_w, pool_w, pool_scale, name, deps=()):
    t, e, dc, dp, cg, tt, cw = _mixer_sizes(proj, conv_w)
    per_halo = tt // HALO_ROWS
    steps = t // tt
    n_groups = len(POOL_WINDOWS)

    def body(cur_ref, prev_ref, next_ref, dy_ref, dyn_ref, cw_ref, pw_ref, ps_ref,
             dp_ref, dcw_ref, dpw_ref, dps_ref):
        i = pl.program_id(0)
        first = i == 0
        last = i == steps - 1

        @pl.when(first)
        def _():
            dcw_ref[...] = jnp.zeros_like(dcw_ref)
            dpw_ref[...] = jnp.zeros_like(dpw_ref)
            dps_ref[...] = jnp.zeros_like(dps_ref)

        def cur(lo, width):
            return cur_ref[:, lo:lo + width].astype(F32)

        def prev(lo, width):
            return jnp.where(first, 0.0, prev_ref[:, lo:lo + width].astype(F32))

        def nxt(ref, lo, width):
            return jnp.where(last, 0.0, ref[:, lo:lo + width].astype(F32))

        def colsum(v):
            return jnp.sum(v, axis=0, keepdims=True)

        for lo in range(0, dc, cw):
            cols = slice(lo, lo + cw)
            b, c, xt = cur(lo, cw), cur(dc + lo, cw), cur(2 * dc + lo, cw)
            u = c * xt
            ext = jnp.concatenate([prev(dc + lo, cw) * prev(2 * dc + lo, cw), u], axis=0)
            u1 = _shift_down(ext, 1)[HALO_ROWS:]
            u2 = _shift_down(ext, 2)[HALO_ROWS:]
            w0, w1, w2 = cw_ref[0:1, cols], cw_ref[1:2, cols], cw_ref[2:3, cols]
            dyc = dy_ref[:, cols].astype(F32)
            dp_ref[:, cols] = (dyc * (w0 * u2 + w1 * u1 + w2 * u)).astype(BF16)
            dconv = dyc * b
            dcw_ref[0:1, cols] += colsum(dconv * u2)
            dcw_ref[1:2, cols] += colsum(dconv * u1)
            dcw_ref[2:3, cols] += colsum(dconv * u)
            dext = jnp.concatenate([dconv, nxt(dyn_ref, lo, cw) * nxt(next_ref, lo, cw)], axis=0)
            du = w2 * dconv + w1 * _shift_up(dext, 1)[:tt] + w0 * _shift_up(dext, 2)[:tt]
            dp_ref[:, dc + lo:dc + lo + cw] = (du * xt).astype(BF16)
            dp_ref[:, 2 * dc + lo:2 * dc + lo + cw] = (du * c).astype(BF16)

        for g, w in enumerate(POOL_WINDOWS):
            lo = 3 * dc + g * cg
            ycols = slice(dc + g * cg, dc + (g + 1) * cg)
            pcols = slice(g * cg, (g + 1) * cg)
            v = cur(lo, cg)
            ext = jnp.concatenate([prev(lo, cg), v], axis=0)
            count = _window_count(i, tt, cg, w)
            d = ((_window_sum(ext, w, _shift_down)[HALO_ROWS:] / count) - v).astype(BF16)
            pw = pw_ref[g]
            scale = ps_ref[0:1, pcols]
            dyp = dy_ref[:, ycols].astype(F32)
            z = jnp.dot(d, pw, preferred_element_type=F32)
            dps_ref[0:1, pcols] += colsum(dyp * z)
            dz = (dyp * scale).astype(BF16)
            dpw_ref[g] += lax.dot_general(d, dz, (((0,), (0,)), ((), ())), preferred_element_type=F32)
            dd = lax.dot_general(dz, pw, (((1,), (1,)), ((), ())), preferred_element_type=F32)
            dzn = (nxt(dyn_ref, dc + g * cg, cg) * scale).astype(BF16)
            ddn = lax.dot_general(dzn, pw, (((1,), (1,)), ((), ())), preferred_element_type=F32)
            qext = jnp.concatenate([dd / count, ddn / float(w)], axis=0)
            dp_ref[:, lo:lo + cg] = (_window_sum(qext, w, _shift_up)[:tt] - dd).astype(BF16)

    cur_spec = lambda width: pl.BlockSpec((tt, width), lambda i: (i, 0))
    prev_spec = pl.BlockSpec((HALO_ROWS, e), lambda i: (jnp.maximum(i * per_halo - 1, 0), 0))
    next_spec = lambda width: pl.BlockSpec(
        (HALO_ROWS, width), lambda i: (jnp.minimum((i + 1) * per_halo, t // HALO_ROWS - 1), 0))
    in_specs = [cur_spec(e), prev_spec, next_spec(e), cur_spec(dc + dp), next_spec(dc + dp),
                pl.BlockSpec((3, dc), lambda i: (0, 0)),
                pl.BlockSpec((n_groups, cg, cg), lambda i: (0, 0, 0)),
                pl.BlockSpec((1, dp), lambda i: (0, 0))]
    body, in_specs, operands = _run_after(
        deps, body, in_specs, [proj, proj, proj, dy, dy, conv_w, pool_w, pool_scale.reshape(1, dp)])
    return pl.pallas_call(
        body, name=name, grid=(steps,),
        out_shape=[jax.ShapeDtypeStruct((t, e), BF16), jax.ShapeDtypeStruct((3, dc), F32),
                   jax.ShapeDtypeStruct((n_groups, cg, cg), F32), jax.ShapeDtypeStruct((1, dp), F32)],
        in_specs=in_specs,
        out_specs=[cur_spec(e), pl.BlockSpec((3, dc), lambda i: (0, 0)),
                   pl.BlockSpec((n_groups, cg, cg), lambda i: (0, 0, 0)),
                   pl.BlockSpec((1, dp), lambda i: (0, 0))],
        compiler_params=_params("arbitrary"),
    )(*operands)


def _adamw(partials, w, m, v, me_slot, name, mine=()):
    n_layers, r, c = w.shape
    assert len(partials) == n_layers and len(mine) in (0, n_layers)
    n_mine = len(mine)
    tr = r if r * c <= ADAM_BLOCK_ELEMS else _tile(r, max(16, ADAM_BLOCK_ELEMS // c))

    def body(me_ref, *refs):
        own_refs, p_refs = refs[:n_mine], refs[n_mine:n_mine + n_layers]
        w_ref, m_ref, v_ref, g_out, d_out, m_out, v_out = refs[n_mine + n_layers:]
        for l in range(n_layers):
            g = own_refs[l][0].astype(F32) if n_mine else None
            for s in range(p_refs[l].shape[0]):
                part = p_refs[l][s].astype(F32)
                g = part if g is None else g + part
            m_new = ADAM_B1 * m_ref[l] + (1.0 - ADAM_B1) * g
            v_new = ADAM_B2 * v_ref[l] + (1.0 - ADAM_B2) * (g * g)
            m_hat = m_new / (1.0 - ADAM_B1 ** ADAM_STEP)
            v_hat = v_new / (1.0 - ADAM_B2 ** ADAM_STEP)
            g_out[l] = g
            d_out[l] = -ADAM_LR * (m_hat / (jnp.sqrt(v_hat) + ADAM_EPS) + ADAM_WD * w_ref[l])
            m_out[l] = m_new
            v_out[l] = v_new

    own_spec = pl.BlockSpec((1, tr, c), lambda i, me_ref: (me_ref[0], i, 0))
    p_specs = [pl.BlockSpec((p.shape[0], tr, c), lambda i, me_ref: (0, i, 0)) for p in partials]
    w_spec = pl.BlockSpec((n_layers, tr, c), lambda i, me_ref: (0, i, 0))
    return pl.pallas_call(
        body, name=name,
        grid_spec=pltpu.PrefetchScalarGridSpec(
            num_scalar_prefetch=1, grid=(r // tr,),
            in_specs=[own_spec] * n_mine + p_specs + [w_spec] * 3, out_specs=[w_spec] * 4),
        out_shape=[jax.ShapeDtypeStruct(w.shape, F32)] * 4,
        compiler_params=_params("parallel"),
    )(jnp.reshape(me_slot, (1,)).astype(jnp.int32), *mine, *partials, w, m, v)


def _to_bf16(w, first, last, name, deps=()):
    _, r, c = w.shape
    tr = r if r * c <= CAST_BLOCK_ELEMS else _tile(r, max(16, CAST_BLOCK_ELEMS // c))

    def body(w_ref, o_ref):
        o_ref[...] = w_ref[...].astype(BF16)

    in_specs = [pl.BlockSpec((1, tr, c), lambda l, i: (first + l, i, 0))]
    body, in_specs, operands = _run_after(deps, body, in_specs, [w])
    return pl.pallas_call(
        body, name=name, grid=(last - first, r // tr),
        out_shape=jax.ShapeDtypeStruct((last - first, r, c), BF16),
        in_specs=in_specs, out_specs=pl.BlockSpec((1, tr, c), lambda l, i: (l, i, 0)),
        compiler_params=_params("parallel", "parallel"),
    )(*operands)


def kernel(x, w_in, conv_w, pool_w, pool_scale, w_out, norm_mix, norm_mlp, w_up, w_down, norm_final, loss_target, m_w_in, m_conv_w, m_pool_w, m_pool_scale, m_w_out, m_norm_mix, m_norm_mlp, m_w_up, m_w_down, m_norm_final, v_w_in, v_conv_w, v_pool_w, v_pool_scale, v_w_out, v_norm_mix, v_norm_mlp, v_w_up, v_w_down, v_norm_final):
    n_layers, d, e_shard = w_in.shape
    t = x.shape[1]
    n_groups, cg_shard, cg = pool_w.shape[1:]
    dc_shard = conv_w.shape[2]
    dc, dp = dc_shard * N_DEV, n_groups * cg
    f_shard = w_up.shape[2]
    xs = x.reshape(t, d)
    target = loss_target.reshape(t, d)

    me = _slot(_mesh_position())
    big = (w_in, w_out, w_up, w_down)
    n_kinds = len(big)
    first_layer = [_to_bf16(w, 0, 1, "cast_first_layer")[0] for w in big]
    mixer_blocks = [conv_w.reshape(n_layers * 3, dc_shard), pool_w.reshape(n_layers * n_groups * cg_shard, cg)]
    groups = {}

    def start(tag, blocks, after):
        send_sems, recv_sems, blocks, lands, token = _gather_start(blocks, me, after, f"weights_start_{tag}")
        groups[tag] = (send_sems, recv_sems, blocks, lands)
        return token

    token = start("first", [first_layer[0], *mixer_blocks, *first_layer[1:]], xs)
    if n_layers > 1:
        others = [_to_bf16(w, 1, n_layers, "cast_other_layers", deps=(token,)) for w in big]
        token = start("rest", [others[k][l - 1] for l in range(1, n_layers) for k in range(n_kinds)], token)
    mixer_place = ("first", 1)

    def place(l, k):
        return ("first", 0 if k == 0 else k + len(mixer_blocks)) if l == 0 else ("rest", n_kinds * (l - 1) + k)

    def passed_on(where, n, after, name):
        tag, i = where
        lands = groups[tag][3]
        send_sems, recv_sems, lands[i:i + n] = _gather_pass_on(groups[tag][1], i, lands[i:i + n], after, f"weights_pass_{name}")
        return send_sems, recv_sems

    def gathered(where, n, passed_sems, after, name):
        tag, i = where
        first_send, first_recv, blocks, lands = groups[tag]
        lands[i:i + n] = _gather_finish((first_send, first_recv), i, passed_sems, blocks[i:i + n], lands[i:i + n], after,
                                        f"weights_finish_{name}")
        return lands[i:i + n]

    def landing(where):
        return groups[where[0]][3][where[1]]

    saved, weights = [], []
    xc = xs
    after = token
    for l in range(n_layers):
        if l == 0:
            h1 = _rmsnorm(xc, norm_mix[l], "norm_mix", deps=(after,))
            sems = passed_on(place(l, 0), 1, h1, f"{l}_0")
        else:
            sems = passed_on(place(l, 0), 1, after, f"{l}_0")
            h1 = _rmsnorm(xc, norm_mix[l], "norm_mix", deps=(landing(place(l, 0)),))
        win, = gathered(place(l, 0), 1, sems, h1, f"{l}_0")
        proj = _mm_nn(h1, win, out_dtype=BF16, name="in_proj")
        if l == 0:
            mixer_sems = passed_on(mixer_place, len(mixer_blocks), proj, "mixer")
        sems = passed_on(place(l, 1), 1, proj, f"{l}_1")
        if l == 0:
            conv_g, pool_g = gathered(mixer_place, len(mixer_blocks), mixer_sems, proj, "mixer")
            conv_full = conv_g.reshape(N_DEV, n_layers, 3, dc_shard).transpose(1, 2, 0, 3).reshape(n_layers, 3, dc)
            pool_full = pool_g.reshape(N_DEV, n_layers, n_groups, cg_shard, cg).transpose(1, 2, 0, 3, 4)
            pool_full = pool_full.reshape(n_layers, n_groups, cg, cg).astype(BF16)
        y = _mixer_fwd(proj, conv_full[l], pool_full[l], pool_scale[l], "mixer_fwd", deps=(landing(place(l, 1)),))
        wout = gathered(place(l, 1), 1, sems, y, f"{l}_1")[0].reshape(1, d, d)
        x1 = _mm_nn(y, wout, out_dtype=F32, res=xc, name="out_proj")
        sems = passed_on(place(l, 2), 1, x1, f"{l}_2")
        h2 = _rmsnorm(x1, norm_mlp[l], "norm_mlp", deps=(landing(place(l, 2)),))
        wup, = gathered(place(l, 2), 1, sems, h2, f"{l}_2")
        if l < 2:
            a, s = _mm_nn(h2, wup, out_dtype=BF16, with_relu2=True, name="mlp_up")
            sems = passed_on(place(l, 3), 1, a, f"{l}_3")
        else:
            sems = passed_on(place(l, 3), 1, h2, f"{l}_3")
            a, s = _mm_nn(h2, wup, out_dtype=BF16, with_relu2=True, name="mlp_up", deps=(landing(place(l, 3)),))
        wdown = gathered(place(l, 3), 1, sems, a, f"{l}_3")[0].reshape(1, f_shard * N_DEV, d)
        x2 = _mm_nn(s, wdown, out_dtype=F32, res=x1, name="mlp_down")
        weights.append((win, wout, wup, wdown))
        saved.append((xc, h1, proj, y, x1, h2, a, s))
        xc = after = x2

    loss_part, dx, dxb, g_norm_final = _loss_head(xc, norm_final, target, "loss_head")
    loss = lax.psum(loss_part[0, 0], MESH_AXES)

    in_flight = {}
    received = {}

    def push(l, arrs, keys, tag):
        send_sems, recv_sems, thru, lands, token = _exchange_start(arrs, f"grads_start_{tag}_{l}")
        in_flight.setdefault(l, []).append((send_sems, recv_sems, thru, lands, keys, tag))
        return token

    def land(l, after):
        for send_sems, recv_sems, thru, lands, keys, tag in in_flight.pop(l):
            thru, lands = _exchange_wait(send_sems, recv_sems, thru, lands, after, f"grads_wait_{tag}_{l}")
            for key, own, got in zip(keys, thru, lands):
                received[key, l] = (own, got)

    g_conv, g_scale, g_mix, g_mlp = ([None] * n_layers for _ in range(4))
    for l in reversed(range(n_layers)):
        win, wout, wup, wdown = weights[l]
        x0, h1, proj, y, x1, h2, a, s = saved[l]
        gw_down = _mm_tn(s, dxb, n_blocks=1, name="mlp_down_dw")
        tok_down = push(l, [gw_down.reshape(N_DEV, f_shard, d)], ["w_down"], "down")
        da = _mm_nt(dxb, wdown, out_dtype=BF16, relu2_grad_of=a, name="mlp_down_dx", deps=(tok_down,))
        gw_up = _mm_tn(h2, da, n_blocks=N_DEV, name="mlp_up_dw")
        tok_up = push(l, [gw_up], ["w_up"], "up")
        dh2 = _mm_nt(da, wup, out_dtype=F32, name="mlp_up_dx", deps=(tok_up,))
        dx1, dx1b, g_mlp[l] = _rmsnorm_bwd(dh2, x1, norm_mlp[l], dx, "norm_mlp_bwd")
        dy = _mm_nt(dx1b, wout, out_dtype=BF16, name="out_proj_dx")
        gw_out = _mm_tn(y, dx1b, n_blocks=1, name="out_proj_dw")
        tok_out = push(l, [gw_out.reshape(N_DEV, d // N_DEV, d)], ["w_out"], "out")
        dproj, g_conv[l], gw_pool, g_scale[l] = _mixer_bwd(
            proj, dy, conv_full[l], pool_full[l], pool_scale[l], "mixer_bwd", deps=(tok_out,))
        gw_in = _mm_tn(h1, dproj, n_blocks=N_DEV, name="in_proj_dw")
        gw_pool = gw_pool.reshape(n_groups, N_DEV, cg_shard, cg).transpose(1, 0, 2, 3)
        tok_in = push(l, [gw_in, gw_pool.reshape(N_DEV, n_groups * cg_shard, cg).astype(BF16)], ["w_in", "pool_w"], "in")
        dh1 = _mm_nt(dproj, win, out_dtype=F32, name="in_proj_dx", deps=(tok_in,))
        dx, dxb, g_mix[l] = _rmsnorm_bwd(dh1, x0, norm_mix[l], dx1, "norm_mix_bwd")
        if l + 1 < n_layers:
            land(l + 1, dx)
    land(0, dx)
    grad_x = dx.reshape(x.shape)

    small = _all_gather(
        [jnp.stack(g_conv).reshape(n_layers * 3, dc), jnp.concatenate(g_scale, axis=0),
         jnp.concatenate(g_mix, axis=0), jnp.concatenate(g_mlp, axis=0), g_norm_final], "gather_small_grads")
    conv_parts = lax.dynamic_slice_in_dim(small[0], me * dc_shard, dc_shard, axis=2)

    def update(partials, w, m, v, name, mine=()):
        shape = w.shape
        rc = (shape[0], -1, shape[-1]) if w.ndim > 2 else (1, *shape) if w.ndim == 2 else (1, 1, *shape)
        outs = _adamw(partials, w.reshape(rc), m.reshape(rc), v.reshape(rc), me, name, mine=mine)
        return [o.reshape(shape) for o in outs]

    def exchanged(key, w, m, v):
        return update([received[key, l][1] for l in range(n_layers)], w, m, v, f"adamw_{key}",
                      mine=[received[key, l][0] for l in range(n_layers)])

    results = {
        "w_in": exchanged("w_in", w_in, m_w_in, v_w_in),
        "conv_w": update([conv_parts], conv_w.reshape(n_layers * 3, dc_shard), m_conv_w.reshape(n_layers * 3, dc_shard),
                         v_conv_w.reshape(n_layers * 3, dc_shard), "adamw_conv_w"),
        "pool_w": exchanged("pool_w", pool_w, m_pool_w, v_pool_w),
        "pool_scale": update([small[1]], pool_scale, m_pool_scale, v_pool_scale, "adamw_pool_scale"),
        "w_out": exchanged("w_out", w_out, m_w_out, v_w_out),
        "norm_mix": update([small[2]], norm_mix, m_norm_mix, v_norm_mix, "adamw_norm_mix"),
        "norm_mlp": update([small[3]], norm_mlp, m_norm_mlp, v_norm_mlp, "adamw_norm_mlp"),
        "w_up": exchanged("w_up", w_up, m_w_up, v_w_up),
        "w_down": exchanged("w_down", w_down, m_w_down, v_w_down),
        "norm_final": update([small[4]], norm_final, m_norm_final, v_norm_final, "adamw_norm_final"),
    }
    results["conv_w"] = [o.reshape(conv_w.shape) for o in results["conv_w"]]
    order = ("w_in", "conv_w", "pool_w", "pool_scale", "w_out", "norm_mix", "norm_mlp", "w_up", "w_down", "norm_final")
    return (loss, grad_x, *[results[k][0] for k in order], *[results[k][1] for k in order],
            *[results[k][2] for k in order], *[results[k][3] for k in order])
```

```python
import functools

import jax
import jax.numpy as jnp
from jax import lax
from jax.experimental import pallas as pl
from jax.experimental.pallas import tpu as pltpu

F32 = jnp.float32
BF16 = jnp.bfloat16

N_DEV = 8
MESH_AXES = ("x", "y", "c")
NORM_EPS = 1e-6
POOL_WINDOWS = (2, 4, 8, 16)
HALO_ROWS = 16

ADAM_LR = 0.001
ADAM_B1 = 0.9
ADAM_B2 = 0.999
ADAM_EPS = 1e-08
ADAM_WD = 0.01
ADAM_STEP = 10

VMEM_BYTES_V7X = 64 * 1024 * 1024
VMEM_LIMIT = (VMEM_BYTES_V7X * 3) // 4

MM_TILE = 1024
MM_TILE_K = 2048
ROW_TILE = 256
NORM_BWD_TILE = 512
MIXER_TILE = 512
ADAM_BLOCK_ELEMS = 64 * 1024
CAST_BLOCK_ELEMS = 512 * 1024


def _params(*semantics):
    return pltpu.CompilerParams(dimension_semantics=semantics, vmem_limit_bytes=VMEM_LIMIT)


def _tile(dim, pref):
    t = min(dim, pref)
    assert dim % t == 0, (dim, pref)
    return t


def _mesh_position():
    return lax.axis_index("x"), lax.axis_index("y"), lax.axis_index("c")


def _slot(p):
    return 4 * p[0] + 2 * p[1] + p[2]


def _all_gather(arrs, name):
    n = len(arrs)

    def body(*refs):
        ins, outs = refs[:n], refs[n:2 * n]
        send_sems, recv_sems, local_sems = refs[2 * n:]
        x, y, c = _mesh_position()
        me, sibling = (x, y, c), (x, y, 1 - c)
        chips = [(1 - x, y), (x, 1 - y), (1 - x, 1 - y)]

        def copy(a, k, block, to, src=None):
            dst = outs[a].at[_slot(block)]
            return pltpu.make_async_remote_copy(
                src_ref=dst if src is None else src, dst_ref=dst,
                send_sem=send_sems.at[a, k], recv_sem=recv_sems.at[a, k],
                device_id=to, device_id_type=pl.DeviceIdType.MESH)

        mine = [pltpu.make_async_copy(ins[a], outs[a].at[_slot(me)], local_sems.at[a]) for a in range(n)]
        for cp in mine:
            cp.start()
        first = []
        for a in range(n):
            first.append(copy(a, 0, me, sibling, src=ins[a]))
            first += [copy(a, 1 + j, me, (*chip, c), src=ins[a]) for j, chip in enumerate(chips)]
        for cp in first:
            cp.start()
        passed = []
        for j, chip in enumerate(chips):
            for a in range(n):
                copy(a, 1 + j, (*chip, c), me).wait_recv()
                cp = copy(a, 4 + j, (*chip, c), sibling)
                cp.start()
                passed.append(cp)
        for a in range(n):
            copy(a, 0, sibling, me).wait_recv()
            for j, chip in enumerate(chips):
                copy(a, 4 + j, (*chip, 1 - c), me).wait_recv()
        for cp in first + passed:
            cp.wait_send()
        for cp in mine:
            cp.wait()

    any_spec = pl.BlockSpec(memory_space=pl.ANY)
    return pl.pallas_call(
        body, name=name,
        out_shape=[jax.ShapeDtypeStruct((N_DEV, *a.shape), a.dtype) for a in arrs],
        in_specs=[any_spec] * n, out_specs=[any_spec] * n,
        scratch_shapes=[pltpu.SemaphoreType.DMA((n, 7)), pltpu.SemaphoreType.DMA((n, 7)),
                        pltpu.SemaphoreType.DMA((n,))],
    )(*arrs)


def _peer(k):
    x, y, c = _mesh_position()
    return (1 - x if k & 4 else x, 1 - y if k & 2 else y, 1 - c if k & 1 else c)


_HBM = pl.BlockSpec(memory_space=pltpu.HBM)
_SEM = pl.BlockSpec(memory_space=pltpu.SEMAPHORE)
_EFFECT = pltpu.SideEffectType.DATAFLOW_SIDE_EFFECTING


def _exchange_copy(g_ref, land_ref, send_sems, recv_sems, a, k):
    return pltpu.make_async_remote_copy(
        src_ref=g_ref.at[_slot(_peer(k))], dst_ref=land_ref.at[k - 1],
        send_sem=send_sems.at[a * (N_DEV - 1) + k - 1], recv_sem=recv_sems.at[a * (N_DEV - 1) + k - 1],
        device_id=_peer(k), device_id_type=pl.DeviceIdType.MESH)


def _exchange_start(arrs, name):
    n = len(arrs)

    def body(*refs):
        g_refs, land_refs = refs[:n], refs[n:2 * n]
        send_sems, recv_sems = refs[2 * n:2 * n + 2]
        token = refs[-1]
        for k in range(1, N_DEV):
            for a in range(n):
                _exchange_copy(g_refs[a], land_refs[a], send_sems, recv_sems, a, k).start()
        token[...] = jnp.zeros_like(token)

    lands = [lax.empty((N_DEV - 1, *g.shape[1:]), g.dtype) for g in arrs]
    outs = pl.pallas_call(
        body, name=name,
        out_shape=(pltpu.SemaphoreType.DMA((n * (N_DEV - 1),)), pltpu.SemaphoreType.DMA((n * (N_DEV - 1),)),
                   *[pltpu.HBM(g.shape, g.dtype) for g in arrs], *[pltpu.HBM(z.shape, z.dtype) for z in lands],
                   jax.ShapeDtypeStruct((8, 128), F32)),
        in_specs=[_HBM] * (2 * n),
        out_specs=(_SEM, _SEM, *[_HBM] * (2 * n), pl.BlockSpec(memory_space=pltpu.VMEM)),
        input_output_aliases={i: 2 + i for i in range(2 * n)},
        compiler_params=pltpu.CompilerParams(has_side_effects=_EFFECT),
    )(*[pltpu.with_memory_space_constraint(g, pltpu.HBM) for g in arrs],
      *[pltpu.with_memory_space_constraint(z, pltpu.HBM) for z in lands])
    return outs[0], outs[1], list(outs[2:2 + n]), list(outs[2 + n:2 + 2 * n]), outs[-1]


def _exchange_wait(send_sems, recv_sems, arrs, lands, after, name):
    n = len(arrs)

    def body(*refs):
        g_refs, land_refs = refs[:n], refs[n:2 * n]
        send_sems_ref, recv_sems_ref = refs[2 * n:2 * n + 2]
        for k in range(1, N_DEV):
            for a in range(n):
                cp = _exchange_copy(g_refs[a], land_refs[a], send_sems_ref, recv_sems_ref, a, k)
                cp.wait_send()
                cp.wait_recv()

    outs = pl.pallas_call(
        body, name=name,
        out_shape=(*[pltpu.HBM(g.shape, g.dtype) for g in arrs], *[pltpu.HBM(z.shape, z.dtype) for z in lands]),
        in_specs=[_HBM] * (2 * n) + [_SEM, _SEM, pl.BlockSpec(memory_space=pl.ANY)],
        out_specs=[_HBM] * (2 * n),
        input_output_aliases={i: i for i in range(2 * n)},
        compiler_params=pltpu.CompilerParams(has_side_effects=_EFFECT),
    )(*arrs, *lands, send_sems, recv_sems, after)
    return list(outs[:n]), list(outs[n:])


N_FIRST = 4
N_PASSED = 3


def _other_chips():
    x, y, _ = _mesh_position()
    return [(1 - x, y), (x, 1 - y), (1 - x, 1 - y)]


def _gather_copy(src_ref, land_ref, block, to, send_sem, recv_sem):
    rows = land_ref.at[_slot(block)]
    return pltpu.make_async_remote_copy(
        src_ref=rows if src_ref is None else src_ref, dst_ref=rows, send_sem=send_sem, recv_sem=recv_sem,
        device_id=to, device_id_type=pl.DeviceIdType.MESH)


def _gather_start(arrs, me_slot, after, name):
    n = len(arrs)

    def body(*refs):
        src_refs, land_refs = refs[:n], refs[n:2 * n]
        send_sems, recv_sems = refs[2 * n + 1:2 * n + 3]
        token = refs[-1]
        x, y, c = _mesh_position()
        targets = [(x, y, 1 - c)] + [(*chip, c) for chip in _other_chips()]
        for a in range(n):
            for k, to in enumerate(targets):
                _gather_copy(src_refs[a], land_refs[a], (x, y, c), to,
                             send_sems.at[a * N_FIRST + k], recv_sems.at[a * N_FIRST + k]).start()
        token[...] = jnp.zeros_like(token)

    lands = [lax.dynamic_update_slice_in_dim(lax.empty((N_DEV, *a.shape), a.dtype), a[None], me_slot, axis=0)
             for a in arrs]
    outs = pl.pallas_call(
        body, name=name,
        out_shape=(pltpu.SemaphoreType.DMA((n * N_FIRST,)), pltpu.SemaphoreType.DMA((n * N_FIRST,)),
                   *[pltpu.HBM(a.shape, a.dtype) for a in arrs], *[pltpu.HBM(z.shape, z.dtype) for z in lands],
                   jax.ShapeDtypeStruct((8, 128), F32)),
        in_specs=[_HBM] * (2 * n) + [pl.BlockSpec(memory_space=pl.ANY)],
        out_specs=(_SEM, _SEM, *[_HBM] * (2 * n), pl.BlockSpec(memory_space=pltpu.VMEM)),
        input_output_aliases={i: 2 + i for i in range(2 * n)},
        compiler_params=pltpu.CompilerParams(has_side_effects=_EFFECT),
    )(*[pltpu.with_memory_space_constraint(a, pltpu.HBM) for a in arrs],
      *[pltpu.with_memory_space_constraint(z, pltpu.HBM) for z in lands], after)
    return outs[0], outs[1], list(outs[2:2 + n]), list(outs[2 + n:2 + 2 * n]), outs[-1]


def _gather_pass_on(first_recv_sems, base, lands, after, name):
    n = len(lands)

    def body(*refs):
        land_refs = refs[:n]
        first_recv = refs[n]
        send_sems, recv_sems = refs[n + 2:n + 4]
        x, y, c = _mesh_position()
        sibling = (x, y, 1 - c)
        for j, chip in enumerate(_other_chips()):
            for a in range(n):
                _gather_copy(None, land_refs[a], (*chip, c), sibling,
                             send_sems.at[a * N_PASSED + j], first_recv.at[(base + a) * N_FIRST + 1 + j]).wait_recv()
                _gather_copy(None, land_refs[a], (*chip, c), sibling,
                             send_sems.at[a * N_PASSED + j], recv_sems.at[a * N_PASSED + j]).start()

    outs = pl.pallas_call(
        body, name=name,
        out_shape=(pltpu.SemaphoreType.DMA((n * N_PASSED,)), pltpu.SemaphoreType.DMA((n * N_PASSED,)),
                   *[pltpu.HBM(z.shape, z.dtype) for z in lands]),
        in_specs=[_HBM] * n + [_SEM, pl.BlockSpec(memory_space=pl.ANY)],
        out_specs=(_SEM, _SEM, *[_HBM] * n),
        input_output_aliases={i: 2 + i for i in range(n)},
        compiler_params=pltpu.CompilerParams(has_side_effects=_EFFECT),
    )(*lands, first_recv_sems, after)
    return outs[0], outs[1], list(outs[2:])


def _gather_finish(first_sems, base, passed_sems, arrs, lands, after, name):
    n = len(lands)

    def body(*refs):
        src_refs, land_refs = refs[:n], refs[n:2 * n]
        first_send, first_recv, passed_send, passed_recv = refs[2 * n:2 * n + 4]
        x, y, c = _mesh_position()
        sibling = (x, y, 1 - c)
        chips = _other_chips()
        for a in range(n):
            _gather_copy(src_refs[a], land_refs[a], sibling, sibling,
                         first_send.at[(base + a) * N_FIRST], first_recv.at[(base + a) * N_FIRST]).wait_recv()
            for j, chip in enumerate(chips):
                _gather_copy(None, land_refs[a], (*chip, 1 - c), sibling,
                             passed_send.at[a * N_PASSED + j], passed_recv.at[a * N_PASSED + j]).wait_recv()
        for a in range(n):
            for k in range(N_FIRST):
                _gather_copy(src_refs[a], land_refs[a], (x, y, c), sibling,
                             first_send.at[(base + a) * N_FIRST + k], first_recv.at[(base + a) * N_FIRST + k]).wait_send()
            for j, chip in enumerate(chips):
                _gather_copy(None, land_refs[a], (*chip, c), sibling,
                             passed_send.at[a * N_PASSED + j], passed_recv.at[a * N_PASSED + j]).wait_send()

    outs = pl.pallas_call(
        body, name=name,
        out_shape=(*[pltpu.HBM(a.shape, a.dtype) for a in arrs], *[pltpu.HBM(z.shape, z.dtype) for z in lands]),
        in_specs=[_HBM] * (2 * n) + [_SEM] * 4 + [pl.BlockSpec(memory_space=pl.ANY)],
        out_specs=[_HBM] * (2 * n),
        input_output_aliases={i: i for i in range(2 * n)},
        compiler_params=pltpu.CompilerParams(has_side_effects=_EFFECT),
    )(*arrs, *lands, *first_sems, *passed_sems, after)
    return list(outs[n:])


def _run_after(deps, body, in_specs, operands):
    n_deps = len(deps)
    if n_deps == 0:
        return body, in_specs, operands

    def body_behind(*refs):
        body(*refs[n_deps:])

    return body_behind, [pl.BlockSpec(memory_space=pl.ANY)] * n_deps + list(in_specs), list(deps) + list(operands)


def _zero_at_first(acc_ref, k, nk):
    if nk > 1:
        @pl.when(k == 0)
        def _():
            acc_ref[...] = jnp.zeros_like(acc_ref)


def _accumulate(acc_ref, product, k, nk, finish):
    if nk == 1:
        finish(product())
        return

    @pl.when(k < nk - 1)
    def _():
        acc_ref[...] += product()

    @pl.when(k == nk - 1)
    def _():
        finish(acc_ref[...] + product())


def _blocks_per_tile(n_blocks, width, pref):
    if width >= pref:
        return 1
    per = min(n_blocks, pref // width)
    assert n_blocks % per == 0
    return per


def _mm_nn(a, b3, *, out_dtype, name, res=None, with_relu2=False, deps=()):
    m, kdim = a.shape
    nb, kb_, nw = b3.shape
    assert kb_ == kdim
    per_tile = _blocks_per_tile(nb, nw, MM_TILE)
    tw = _tile(nw, MM_TILE)
    per_block = nw // tw
    tn = per_tile * tw
    tm, tk = _tile(m, MM_TILE), _tile(kdim, MM_TILE_K)
    nk = kdim // tk
    n_in = 2 + (res is not None)

    def body(*refs):
        a_ref, b_ref = refs[:2]
        r_ref = refs[2] if res is not None else None
        o_ref = refs[n_in]
        s_ref = refs[n_in + 1] if with_relu2 else None
        acc_ref = refs[-1] if nk > 1 else None
        k = pl.program_id(2)
        _zero_at_first(acc_ref, k, nk)
        for b in range(per_tile):
            cols = slice(b * tw, (b + 1) * tw)

            def product(b=b):
                return jnp.dot(a_ref[...], b_ref[b], preferred_element_type=F32)

            def finish(total, cols=cols):
                if r_ref is not None:
                    total = total + r_ref[:, cols]
                o_ref[:, cols] = total.astype(out_dtype)
                if s_ref is not None:
                    r = jnp.maximum(total, 0.0)
                    s_ref[:, cols] = (r * r).astype(BF16)

            _accumulate(acc_ref.at[:, cols] if nk > 1 else None, product, k, nk, finish)

    if per_tile > 1:
        b_spec = pl.BlockSpec((per_tile, tk, nw), lambda i, j, k: (j, k, 0))
    else:
        b_spec = pl.BlockSpec((1, tk, tw), lambda i, j, k: (j // per_block, k, j % per_block))
    in_specs = [pl.BlockSpec((tm, tk), lambda i, j, k: (i, k)), b_spec]
    operands = [a, b3]
    if res is not None:
        in_specs.append(pl.BlockSpec((tm, tn), lambda i, j, k: (i, j)))
        operands.append(res)
    body, in_specs, operands = _run_after(deps, body, in_specs, operands)
    o_spec = pl.BlockSpec((tm, tn), lambda i, j, k: (i, j))
    out_shape = [jax.ShapeDtypeStruct((m, nb * nw), out_dtype)]
    if with_relu2:
        out_shape.append(jax.ShapeDtypeStruct((m, nb * nw), BF16))
    outs = pl.pallas_call(
        body, name=name, grid=(m // tm, (nb * nw) // tn, nk),
        out_shape=out_shape, in_specs=in_specs, out_specs=[o_spec] * len(out_shape),
        scratch_shapes=[pltpu.VMEM((tm, tn), F32)] if nk > 1 else [],
        compiler_params=_params("parallel", "parallel", "arbitrary"),
    )(*operands)
    return outs if with_relu2 else outs[0]


def _mm_nt(a, b3, *, out_dtype, name, relu2_grad_of=None, deps=()):
    m, kdim = a.shape
    kb, n, kw = b3.shape
    assert kb * kw == kdim
    per_step = _blocks_per_tile(kb, kw, MM_TILE_K)
    tw = _tile(kw, MM_TILE_K)
    per_block = kw // tw
    tk = per_step * tw
    tm, tn = _tile(m, MM_TILE), _tile(n, MM_TILE)
    nk = kdim // tk

    def body(*refs):
        if relu2_grad_of is None:
            a_ref, b_ref, o_ref = refs[:3]
            g_ref = None
        else:
            a_ref, b_ref, g_ref, o_ref = refs[:4]
        acc_ref = refs[-1] if nk > 1 else None
        k = pl.program_id(2)
        _zero_at_first(acc_ref, k, nk)
        def product():
            p = None
            for b in range(per_step):
                part = lax.dot_general(a_ref[:, b * tw:(b + 1) * tw], b_ref[b], (((1,), (1,)), ((), ())),
                                       preferred_element_type=F32)
                p = part if p is None else p + part
            return p

        def finish(total):
            if g_ref is not None:
                total = total * (2.0 * jnp.maximum(g_ref[...].astype(F32), 0.0))
            o_ref[...] = total.astype(out_dtype)

        _accumulate(acc_ref, product, k, nk, finish)

    if per_step > 1:
        b_spec = pl.BlockSpec((per_step, tn, kw), lambda i, j, k: (k, j, 0))
    else:
        b_spec = pl.BlockSpec((1, tn, tw), lambda i, j, k: (k // per_block, j, k % per_block))
    in_specs = [pl.BlockSpec((tm, tk), lambda i, j, k: (i, k)), b_spec]
    operands = [a, b3]
    if relu2_grad_of is not None:
        in_specs.append(pl.BlockSpec((tm, tn), lambda i, j, k: (i, j)))
        operands.append(relu2_grad_of)
    body, in_specs, operands = _run_after(deps, body, in_specs, operands)
    return pl.pallas_call(
        body, name=name, grid=(m // tm, n // tn, nk),
        out_shape=jax.ShapeDtypeStruct((m, n), out_dtype),
        in_specs=in_specs, out_specs=pl.BlockSpec((tm, tn), lambda i, j, k: (i, j)),
        scratch_shapes=[pltpu.VMEM((tm, tn), F32)] if nk > 1 else [],
        compiler_params=_params("parallel", "parallel", "arbitrary"),
    )(*operands)


def _mm_tn(a, b, *, n_blocks, name, deps=()):
    t, m = a.shape
    t2, n = b.shape
    assert t == t2 and n % n_blocks == 0
    nw = n // n_blocks
    per_tile = _blocks_per_tile(n_blocks, nw, MM_TILE)
    tw = _tile(nw, MM_TILE)
    per_block = nw // tw
    tn = per_tile * tw
    tm, tk = _tile(m, MM_TILE), _tile(t, MM_TILE_K)
    nk = t // tk

    def body(a_ref, b_ref, o_ref, *scratch):
        acc_ref = scratch[0] if nk > 1 else None
        k = pl.program_id(2)
        _zero_at_first(acc_ref, k, nk)
        def product():
            return lax.dot_general(a_ref[...], b_ref[...], (((0,), (0,)), ((), ())), preferred_element_type=F32)

        def finish(total):
            for b in range(per_tile):
                o_ref[b] = total[:, b * tw:(b + 1) * tw].astype(BF16)

        _accumulate(acc_ref, product, k, nk, finish)

    if per_tile > 1:
        o_spec = pl.BlockSpec((per_tile, tm, nw), lambda i, j, k: (j, i, 0))
    else:
        o_spec = pl.BlockSpec((1, tm, tw), lambda i, j, k: (j // per_block, i, j % per_block))
    in_specs = [pl.BlockSpec((tk, tm), lambda i, j, k: (k, i)), pl.BlockSpec((tk, tn), lambda i, j, k: (k, j))]
    body, in_specs, operands = _run_after(deps, body, in_specs, [a, b])
    return pl.pallas_call(
        body, name=name, grid=(m // tm, n // tn, nk),
        out_shape=jax.ShapeDtypeStruct((n_blocks, m, nw), BF16),
        in_specs=in_specs, out_specs=o_spec,
        scratch_shapes=[pltpu.VMEM((tm, tn), F32)] if nk > 1 else [],
        compiler_params=_params("parallel", "parallel", "arbitrary"),
    )(*operands)


def _normalise(x):
    r = lax.rsqrt(jnp.mean(x * x, axis=-1, keepdims=True) + NORM_EPS)
    return x * r, r


def _rmsnorm_backward(dh, xhat, r, gain):
    dxhat = dh * gain
    return r * (dxhat - xhat * jnp.mean(dxhat * xhat, axis=-1, keepdims=True))


def _rmsnorm(x, gain, name, deps=()):
    t, d = x.shape
    tr = _tile(t, ROW_TILE)

    def body(x_ref, g_ref, o_ref):
        xhat, _ = _normalise(x_ref[...])
        o_ref[...] = (xhat * g_ref[...]).astype(BF16)

    in_specs = [pl.BlockSpec((tr, d), lambda i: (i, 0)), pl.BlockSpec((1, d), lambda i: (0, 0))]
    body, in_specs, operands = _run_after(deps, body, in_specs, [x, gain.reshape(1, d)])
    return pl.pallas_call(
        body, name=name, grid=(t // tr,),
        out_shape=jax.ShapeDtypeStruct((t, d), BF16),
        in_specs=in_specs,
        out_specs=pl.BlockSpec((tr, d), lambda i: (i, 0)),
        compiler_params=_params("parallel"),
    )(*operands)


def _rmsnorm_bwd(dh, x, gain, dres, name):
    t, d = x.shape
    tr = _tile(t, NORM_BWD_TILE)

    def body(dh_ref, x_ref, g_ref, r_ref, dx_ref, dxb_ref, dg_ref):
        xhat, r = _normalise(x_ref[...])
        dh_v = dh_ref[...].astype(F32)
        dx = r_ref[...] + _rmsnorm_backward(dh_v, xhat, r, g_ref[...])
        dx_ref[...] = dx
        dxb_ref[...] = dx.astype(BF16)
        part = jnp.sum(dh_v * xhat, axis=0, keepdims=True)

        @pl.when(pl.program_id(0) == 0)
        def _():
            dg_ref[...] = part

        @pl.when(pl.program_id(0) > 0)
        def _():
            dg_ref[...] += part

    row = pl.BlockSpec((tr, d), lambda i: (i, 0))
    vec = pl.BlockSpec((1, d), lambda i: (0, 0))
    return pl.pallas_call(
        body, name=name, grid=(t // tr,),
        out_shape=[jax.ShapeDtypeStruct((t, d), F32), jax.ShapeDtypeStruct((t, d), BF16),
                   jax.ShapeDtypeStruct((1, d), F32)],
        in_specs=[row, row, vec, row], out_specs=[row, row, vec],
        compiler_params=_params("arbitrary"),
    )(dh, x, gain.reshape(1, d), dres)


def _loss_head(x, gain, target, name):
    t, d = x.shape
    tr = _tile(t, ROW_TILE)
    steps = t // tr

    def body(x_ref, g_ref, t_ref, loss_ref, dx_ref, dxb_ref, dg_ref, sq_ref):
        i = pl.program_id(0)
        xhat, r = _normalise(x_ref[...])
        gain_v = g_ref[...]
        diff = xhat * gain_v - t_ref[...]
        dy = diff / float(d)
        dx = _rmsnorm_backward(dy, xhat, r, gain_v)
        dx_ref[...] = dx
        dxb_ref[...] = dx.astype(BF16)
        dg_part = jnp.sum(dy * xhat, axis=0, keepdims=True)
        sq_part = jnp.sum(diff * diff, axis=0, keepdims=True)

        @pl.when(i == 0)
        def _():
            dg_ref[...] = dg_part
            sq_ref[...] = sq_part

        @pl.when(i > 0)
        def _():
            dg_ref[...] += dg_part
            sq_ref[...] += sq_part

        @pl.when(i == steps - 1)
        def _():
            loss_ref[...] = (0.5 / float(d)) * jnp.sum(sq_ref[...], axis=1, keepdims=True)

    row = pl.BlockSpec((tr, d), lambda i: (i, 0))
    vec = pl.BlockSpec((1, d), lambda i: (0, 0))
    return pl.pallas_call(
        body, name=name, grid=(steps,),
        out_shape=[jax.ShapeDtypeStruct((1, 1), F32), jax.ShapeDtypeStruct((t, d), F32),
                   jax.ShapeDtypeStruct((t, d), BF16), jax.ShapeDtypeStruct((1, d), F32)],
        in_specs=[row, vec, row],
        out_specs=[pl.BlockSpec((1, 1), lambda i: (0, 0)), row, row, vec],
        scratch_shapes=[pltpu.VMEM((1, d), F32)],
        compiler_params=_params("arbitrary"),
    )(x, gain.reshape(1, d), target)


def _shift_down(ext, s):
    return pltpu.roll(ext, s, 0)


def _shift_up(ext, s):
    return pltpu.roll(ext, ext.shape[0] - s, 0)


def _window_sum(ext, w, shift):
    s = 1
    while s < w:
        ext = ext + shift(ext, s)
        s *= 2
    return ext


def _window_count(tile_index, rows, cols, w):
    t = tile_index * rows + lax.broadcasted_iota(jnp.int32, (rows, cols), 0)
    return jnp.minimum(t + 1, w).astype(F32)


def _mixer_sizes(proj, conv_w):
    t, e = proj.shape
    dc = conv_w.shape[1]
    dp = e - 3 * dc
    cg = dp // len(POOL_WINDOWS)
    tt = _tile(t, MIXER_TILE)
    assert tt % HALO_ROWS == 0 and tt >= HALO_ROWS
    cw = _tile(dc, cg)
    return t, e, dc, dp, cg, tt, cw


def _mixer_fwd(proj, conv_w, pool_w, pool_scale, name, deps=()):
    t, e, dc, dp, cg, tt, cw = _mixer_sizes(proj, conv_w)
    per_halo = tt // HALO_ROWS

    def body(cur_ref, prev_ref, cw_ref, pw_ref, ps_ref, y_ref):
        i = pl.program_id(0)
        first = i == 0

        def cur(lo, width):
            return cur_ref[:, lo:lo + width].astype(F32)

        def prev(lo, width):
            return jnp.where(first, 0.0, prev_ref[:, lo:lo + width].astype(F32))

        for lo in range(0, dc, cw):
            u = cur(dc + lo, cw) * cur(2 * dc + lo, cw)
            ext = jnp.concatenate([prev(dc + lo, cw) * prev(2 * dc + lo, cw), u], axis=0)
            u1 = _shift_down(ext, 1)[HALO_ROWS:]
            u2 = _shift_down(ext, 2)[HALO_ROWS:]
            conv = cw_ref[0:1, lo:lo + cw] * u2 + cw_ref[1:2, lo:lo + cw] * u1 + cw_ref[2:3, lo:lo + cw] * u
            y_ref[:, lo:lo + cw] = (cur(lo, cw) * conv).astype(BF16)

        for g, w in enumerate(POOL_WINDOWS):
            lo = 3 * dc + g * cg
            v = cur(lo, cg)
            ext = jnp.concatenate([prev(lo, cg), v], axis=0)
            mean = _window_sum(ext, w, _shift_down)[HALO_ROWS:] / _window_count(i, tt, cg, w)
            z = jnp.dot((mean - v).astype(BF16), pw_ref[g], preferred_element_type=F32)
            y_ref[:, dc + g * cg:dc + (g + 1) * cg] = (z * ps_ref[0:1, g * cg:(g + 1) * cg]).astype(BF16)

    in_specs = [pl.BlockSpec((tt, e), lambda i: (i, 0)),
                pl.BlockSpec((HALO_ROWS, e), lambda i: (jnp.maximum(i * per_halo - 1, 0), 0)),
                pl.BlockSpec((3, dc), lambda i: (0, 0)),
                pl.BlockSpec((len(POOL_WINDOWS), cg, cg), lambda i: (0, 0, 0)),
                pl.BlockSpec((1, dp), lambda i: (0, 0))]
    body, in_specs, operands = _run_after(
        deps, body, in_specs, [proj, proj, conv_w, pool_w, pool_scale.reshape(1, dp)])
    return pl.pallas_call(
        body, name=name, grid=(t // tt,),
        out_shape=jax.ShapeDtypeStruct((t, dc + dp), BF16),
        in_specs=in_specs,
        out_specs=pl.BlockSpec((tt, dc + dp), lambda i: (i, 0)),
        compiler_params=_params("parallel"),
    )(*operands)


def _mixer_bwd(proj, dy, conv_w, pool_w, pool_scale, name, deps=()):
    t, e, dc, dp, cg, tt, cw = _mixer_sizes(proj, conv_w)
    per_halo = tt // HALO_ROWS
    steps = t // tt
    n_groups = len(POOL_WINDOWS)

    def body(cur_ref, prev_ref, next_ref, dy_ref, dyn_ref, cw_ref, pw_ref, ps_ref,
             dp_ref, dcw_ref, dpw_ref, dps_ref):
        i = pl.program_id(0)
        first = i == 0
        last = i == steps - 1

        @pl.when(first)
        def _():
            dcw_ref[...] = jnp.zeros_like(dcw_ref)
            dpw_ref[...] = jnp.zeros_like(dpw_ref)
            dps_ref[...] = jnp.zeros_like(dps_ref)

        def cur(lo, width):
            return cur_ref[:, lo:lo + width].astype(F32)

        def prev(lo, width):
            return jnp.where(first, 0.0, prev_ref[:, lo:lo + width].astype(F32))

        def nxt(ref, lo, width):
            return jnp.where(last, 0.0, ref[:, lo:lo + width].astype(F32))

        def colsum(v):
            return jnp.sum(v, axis=0, keepdims=True)

        for lo in range(0, dc, cw):
            cols = slice(lo, lo + cw)
            b, c, xt = cur(lo, cw), cur(dc + lo, cw), cur(2 * dc + lo, cw)
            u = c * xt
            ext = jnp.concatenate([prev(dc + lo, cw) * prev(2 * dc + lo, cw), u], axis=0)
            u1 = _shift_down(ext, 1)[HALO_ROWS:]
            u2 = _shift_down(ext, 2)[HALO_ROWS:]
            w0, w1, w2 = cw_ref[0:1, cols], cw_ref[1:2, cols], cw_ref[2:3, cols]
            dyc = dy_ref[:, cols].astype(F32)
            dp_ref[:, cols] = (dyc * (w0 * u2 + w1 * u1 + w2 * u)).astype(BF16)
            dconv = dyc * b
            dcw_ref[0:1, cols] += colsum(dconv * u2)
            dcw_ref[1:2, cols] += colsum(dconv * u1)
            dcw_ref[2:3, cols] += colsum(dconv * u)
            dext = jnp.concatenate([dconv, nxt(dyn_ref, lo, cw) * nxt(next_ref, lo, cw)], axis=0)
            du = w2 * dconv + w1 * _shift_up(dext, 1)[:tt] + w0 * _shift_up(dext, 2)[:tt]
            dp_ref[:, dc + lo:dc + lo + cw] = (du * xt).astype(BF16)
            dp_ref[:, 2 * dc + lo:2 * dc + lo + cw] = (du * c).astype(BF16)

        for g, w in enumerate(POOL_WINDOWS):
            lo = 3 * dc + g * cg
            ycols = slice(dc + g * cg, dc + (g + 1) * cg)
            pcols = slice(g * cg, (g + 1) * cg)
            v = cur(lo, cg)
            ext = jnp.concatenate([prev(lo, cg), v], axis=0)
            count = _window_count(i, tt, cg, w)
            d = ((_window_sum(ext, w, _shift_down)[HALO_ROWS:] / count) - v).astype(BF16)
            pw = pw_ref[g]
            scale = ps_ref[0:1, pcols]
            dyp = dy_ref[:, ycols].astype(F32)
            z = jnp.dot(d, pw, preferred_element_type=F32)
            dps_ref[0:1, pcols] += colsum(dyp * z)
            dz = (dyp * scale).astype(BF16)
            dpw_ref[g] += lax.dot_general(d, dz, (((0,), (0,)), ((), ())), preferred_element_type=F32)
            dd = lax.dot_general(dz, pw, (((1,), (1,)), ((), ())), preferred_element_type=F32)
            dzn = (nxt(dyn_ref, dc + g * cg, cg) * scale).astype(BF16)
            ddn = lax.dot_general(dzn, pw, (((1,), (1,)), ((), ())), preferred_element_type=F32)
            qext = jnp.concatenate([dd / count, ddn / float(w)], axis=0)
            dp_ref[:, lo:lo + cg] = (_window_sum(qext, w, _shift_up)[:tt] - dd).astype(BF16)

    cur_spec = lambda width: pl.BlockSpec((tt, width), lambda i: (i, 0))
    prev_spec = pl.BlockSpec((HALO_ROWS, e), lambda i: (jnp.maximum(i * per_halo - 1, 0), 0))
    next_spec = lambda width: pl.BlockSpec(
        (HALO_ROWS, width), lambda i: (jnp.minimum((i + 1) * per_halo, t // HALO_ROWS - 1), 0))
    in_specs = [cur_spec(e), prev_spec, next_spec(e), cur_spec(dc + dp), next_spec(dc + dp),
                pl.BlockSpec((3, dc), lambda i: (0, 0)),
                pl.BlockSpec((n_groups, cg, cg), lambda i: (0, 0, 0)),
                pl.BlockSpec((1, dp), lambda i: (0, 0))]
    body, in_specs, operands = _run_after(
        deps, body, in_specs, [proj, proj, proj, dy, dy, conv_w, pool_w, pool_scale.reshape(1, dp)])
    return pl.pallas_call(
        body, name=name, grid=(steps,),
        out_shape=[jax.ShapeDtypeStruct((t, e), BF16), jax.ShapeDtypeStruct((3, dc), F32),
                   jax.ShapeDtypeStruct((n_groups, cg, cg), F32), jax.ShapeDtypeStruct((1, dp), F32)],
        in_specs=in_specs,
        out_specs=[cur_spec(e), pl.BlockSpec((3, dc), lambda i: (0, 0)),
                   pl.BlockSpec((n_groups, cg, cg), lambda i: (0, 0, 0)),
                   pl.BlockSpec((1, dp), lambda i: (0, 0))],
        compiler_params=_params("arbitrary"),
    )(*operands)


def _adamw(partials, w, m, v, me_slot, name, mine=()):
    n_layers, r, c = w.shape
    assert len(partials) == n_layers and len(mine) in (0, n_layers)
    n_mine = len(mine)
    tr = r if r * c <= ADAM_BLOCK_ELEMS else _tile(r, max(16, ADAM_BLOCK_ELEMS // c))

    def body(me_ref, *refs):
        own_refs, p_refs = refs[:n_mine], refs[n_mine:n_mine + n_layers]
        w_ref, m_ref, v_ref, g_out, d_out, m_out, v_out = refs[n_mine + n_layers:]
        for l in range(n_layers):
            g = own_refs[l][0].astype(F32) if n_mine else None
            for s in range(p_refs[l].shape[0]):
                part = p_refs[l][s].astype(F32)
                g = part if g is None else g + part
            m_new = ADAM_B1 * m_ref[l] + (1.0 - ADAM_B1) * g
            v_new = ADAM_B2 * v_ref[l] + (1.0 - ADAM_B2) * (g * g)
            m_hat = m_new / (1.0 - ADAM_B1 ** ADAM_STEP)
            v_hat = v_new / (1.0 - ADAM_B2 ** ADAM_STEP)
            g_out[l] = g
            d_out[l] = -ADAM_LR * (m_hat / (jnp.sqrt(v_hat) + ADAM_EPS) + ADAM_WD * w_ref[l])
            m_out[l] = m_new
            v_out[l] = v_new

    own_spec = pl.BlockSpec((1, tr, c), lambda i, me_ref: (me_ref[0], i, 0))
    p_specs = [pl.BlockSpec((p.shape[0], tr, c), lambda i, me_ref: (0, i, 0)) for p in partials]
    w_spec = pl.BlockSpec((n_layers, tr, c), lambda i, me_ref: (0, i, 0))
    return pl.pallas_call(
        body, name=name,
        grid_spec=pltpu.PrefetchScalarGridSpec(
            num_scalar_prefetch=1, grid=(r // tr,),
            in_specs=[own_spec] * n_mine + p_specs + [w_spec] * 3, out_specs=[w_spec] * 4),
        out_shape=[jax.ShapeDtypeStruct(w.shape, F32)] * 4,
        compiler_params=_params("parallel"),
    )(jnp.reshape(me_slot, (1,)).astype(jnp.int32), *mine, *partials, w, m, v)


def _to_bf16(w, first, last, name, deps=()):
    _, r, c = w.shape
    tr = r if r * c <= CAST_BLOCK_ELEMS else _tile(r, max(16, CAST_BLOCK_ELEMS // c))

    def body(w_ref, o_ref):
        o_ref[...] = w_ref[...].astype(BF16)

    in_specs = [pl.BlockSpec((1, tr, c), lambda l, i: (first + l, i, 0))]
    body, in_specs, operands = _run_after(deps, body, in_specs, [w])
    return pl.pallas_call(
        body, name=name, grid=(last - first, r // tr),
        out_shape=jax.ShapeDtypeStruct((last - first, r, c), BF16),
        in_specs=in_specs, out_specs=pl.BlockSpec((1, tr, c), lambda l, i: (l, i, 0)),
        compiler_params=_params("parallel", "parallel"),
    )(*operands)


def kernel(x, w_in, conv_w, pool_w, pool_scale, w_out, norm_mix, norm_mlp, w_up, w_down, norm_final, loss_target, m_w_in, m_conv_w, m_pool_w, m_pool_scale, m_w_out, m_norm_mix, m_norm_mlp, m_w_up, m_w_down, m_norm_final, v_w_in, v_conv_w, v_pool_w, v_pool_scale, v_w_out, v_norm_mix, v_norm_mlp, v_w_up, v_w_down, v_norm_final):
    n_layers, d, e_shard = w_in.shape
    t = x.shape[1]
    n_groups, cg_shard, cg = pool_w.shape[1:]
    dc_shard = conv_w.shape[2]
    dc, dp = dc_shard * N_DEV, n_groups * cg
    f_shard = w_up.shape[2]
    xs = x.reshape(t, d)
    target = loss_target.reshape(t, d)

    me = _slot(_mesh_position())
    big = (w_in, w_out, w_up, w_down)
    n_kinds = len(big)
    first_layer = [_to_bf16(w, 0, 1, "cast_first_layer")[0] for w in big]
    mixer_blocks = [conv_w.reshape(n_layers * 3, dc_shard), pool_w.reshape(n_layers * n_groups * cg_shard, cg)]
    groups = {}

    def start(tag, blocks, after):
        send_sems, recv_sems, blocks, lands, token = _gather_start(blocks, me, after, f"weights_start_{tag}")
        groups[tag] = (send_sems, recv_sems, blocks, lands)
        return token

    token = start("first", [first_layer[0], *mixer_blocks, *first_layer[1:]], xs)
    if n_layers > 1:
        others = [_to_bf16(w, 1, n_layers, "cast_other_layers", deps=(token,)) for w in big]
        token = start("rest", [others[k][l - 1] for l in range(1, n_layers) for k in range(n_kinds)], token)
    mixer_place = ("first", 1)

    def place(l, k):
        return ("first", 0 if k == 0 else k + len(mixer_blocks)) if l == 0 else ("rest", n_kinds * (l - 1) + k)

    def passed_on(where, n, after, name):
        tag, i = where
        lands = groups[tag][3]
        send_sems, recv_sems, lands[i:i + n] = _gather_pass_on(groups[tag][1], i, lands[i:i + n], after, f"weights_pass_{name}")
        return send_sems, recv_sems

    def gathered(where, n, passed_sems, after, name):
        tag, i = where
        first_send, first_recv, blocks, lands = groups[tag]
        lands[i:i + n] = _gather_finish((first_send, first_recv), i, passed_sems, blocks[i:i + n], lands[i:i + n], after,
                                        f"weights_finish_{name}")
        return lands[i:i + n]

    def landing(where):
        return groups[where[0]][3][where[1]]

    saved, weights = [], []
    xc = xs
    after = token
    for l in range(n_layers):
        if l == 0:
            h1 = _rmsnorm(xc, norm_mix[l], "norm_mix", deps=(after,))
            sems = passed_on(place(l, 0), 1, h1, f"{l}_0")
        else:
            sems = passed_on(place(l, 0), 1, after, f"{l}_0")
            h1 = _rmsnorm(xc, norm_mix[l], "norm_mix", deps=(landing(place(l, 0)),))
        win, = gathered(place(l, 0), 1, sems, h1, f"{l}_0")
        proj = _mm_nn(h1, win, out_dtype=BF16, name="in_proj")
        if l == 0:
            mixer_sems = passed_on(mixer_place, len(mixer_blocks), proj, "mixer")
        sems = passed_on(place(l, 1), 1, proj, f"{l}_1")
        if l == 0:
            conv_g, pool_g = gathered(mixer_place, len(mixer_blocks), mixer_sems, proj, "mixer")
            conv_full = conv_g.reshape(N_DEV, n_layers, 3, dc_shard).transpose(1, 2, 0, 3).reshape(n_layers, 3, dc)
            pool_full = pool_g.reshape(N_DEV, n_layers, n_groups, cg_shard, cg).transpose(1, 2, 0, 3, 4)
            pool_full = pool_full.reshape(n_layers, n_groups, cg, cg).astype(BF16)
        y = _mixer_fwd(proj, conv_full[l], pool_full[l], pool_scale[l], "mixer_fwd", deps=(landing(place(l, 1)),))
        wout = gathered(place(l, 1), 1, sems, y, f"{l}_1")[0].reshape(1, d, d)
        x1 = _mm_nn(y, wout, out_dtype=F32, res=xc, name="out_proj")
        sems = passed_on(place(l, 2), 1, x1, f"{l}_2")
        h2 = _rmsnorm(x1, norm_mlp[l], "norm_mlp", deps=(landing(place(l, 2)),))
        wup, = gathered(place(l, 2), 1, sems, h2, f"{l}_2")
        if l < 2:
            a, s = _mm_nn(h2, wup, out_dtype=BF16, with_relu2=True, name="mlp_up")
            sems = passed_on(place(l, 3), 1, a, f"{l}_3")
        else:
            sems = passed_on(place(l, 3), 1, h2, f"{l}_3")
            a, s = _mm_nn(h2, wup, out_dtype=BF16, with_relu2=True, name="mlp_up", deps=(landing(place(l, 3)),))
        wdown = gathered(place(l, 3), 1, sems, a, f"{l}_3")[0].reshape(1, f_shard * N_DEV, d)
        x2 = _mm_nn(s, wdown, out_dtype=F32, res=x1, name="mlp_down")
        weights.append((win, wout, wup, wdown))
        saved.append((xc, h1, proj, y, x1, h2, a, s))
        xc = after = x2

    loss_part, dx, dxb, g_norm_final = _loss_head(xc, norm_final, target, "loss_head")
    loss = lax.psum(loss_part[0, 0], MESH_AXES)

    in_flight = {}
    received = {}

    def push(l, arrs, keys, tag):
        send_sems, recv_sems, thru, lands, token = _exchange_start(arrs, f"grads_start_{tag}_{l}")
        in_flight.setdefault(l, []).append((send_sems, recv_sems, thru, lands, keys, tag))
        return token

    def land(l, after):
        for send_sems, recv_sems, thru, lands, keys, tag in in_flight.pop(l):
            thru, lands = _exchange_wait(send_sems, recv_sems, thru, lands, after, f"grads_wait_{tag}_{l}")
            for key, own, got in zip(keys, thru, lands):
                received[key, l] = (own, got)

    g_conv, g_scale, g_mix, g_mlp = ([None] * n_layers for _ in range(4))
    for l in reversed(range(n_layers)):
        win, wout, wup, wdown = weights[l]
        x0, h1, proj, y, x1, h2, a, s = saved[l]
        gw_down = _mm_tn(s, dxb, n_blocks=1, name="mlp_down_dw")
        tok_down = push(l, [gw_down.reshape(N_DEV, f_shard, d)], ["w_down"], "down")
        da = _mm_nt(dxb, wdown, out_dtype=BF16, relu2_grad_of=a, name="mlp_down_dx", deps=(tok_down,))
        gw_up = _mm_tn(h2, da, n_blocks=N_DEV, name="mlp_up_dw")
        tok_up = push(l, [gw_up], ["w_up"], "up")
        dh2 = _mm_nt(da, wup, out_dtype=BF16, name="mlp_up_dx", deps=(tok_up,))
        dx1, dx1b, g_mlp[l] = _rmsnorm_bwd(dh2, x1, norm_mlp[l], dx, "norm_mlp_bwd")
        dy = _mm_nt(dx1b, wout, out_dtype=BF16, name="out_proj_dx")
        gw_out = _mm_tn(y, dx1b, n_blocks=1, name="out_proj_dw")
        tok_out = push(l, [gw_out.reshape(N_DEV, d // N_DEV, d)], ["w_out"], "out")
        dproj, g_conv[l], gw_pool, g_scale[l] = _mixer_bwd(
            proj, dy, conv_full[l], pool_full[l], pool_scale[l], "mixer_bwd", deps=(tok_out,))
        gw_in = _mm_tn(h1, dproj, n_blocks=N_DEV, name="in_proj_dw")
        gw_pool = gw_pool.reshape(n_groups, N_DEV, cg_shard, cg).transpose(1, 0, 2, 3)
        tok_in = push(l, [gw_in, gw_pool.reshape(N_DEV, n_groups * cg_shard, cg).astype(BF16)], ["w_in", "pool_w"], "in")
        dh1 = _mm_nt(dproj, win, out_dtype=BF16, name="in_proj_dx", deps=(tok_in,))
        dx, dxb, g_mix[l] = _rmsnorm_bwd(dh1, x0, norm_mix[l], dx1, "norm_mix_bwd")
        if l + 1 < n_layers:
            land(l + 1, dx)
    land(0, dx)
    grad_x = dx.reshape(x.shape)

    small = _all_gather(
        [jnp.stack(g_conv).reshape(n_layers * 3, dc), jnp.concatenate(g_scale, axis=0),
         jnp.concatenate(g_mix, axis=0), jnp.concatenate(g_mlp, axis=0), g_norm_final], "gather_small_grads")
    conv_parts = lax.dynamic_slice_in_dim(small[0], me * dc_shard, dc_shard, axis=2)

    def update(partials, w, m, v, name, mine=()):
        shape = w.shape
        rc = (shape[0], -1, shape[-1]) if w.ndim > 2 else (1, *shape) if w.ndim == 2 else (1, 1, *shape)
        outs = _adamw(partials, w.reshape(rc), m.reshape(rc), v.reshape(rc), me, name, mine=mine)
        return [o.reshape(shape) for o in outs]

    def exchanged(key, w, m, v):
        return update([received[key, l][1] for l in range(n_layers)], w, m, v, f"adamw_{key}",
                      mine=[received[key, l][0] for l in range(n_layers)])

    results = {
        "w_in": exchanged("w_in", w_in, m_w_in, v_w_in),
        "conv_w": update([conv_parts], conv_w.reshape(n_layers * 3, dc_shard), m_conv_w.reshape(n_layers * 3, dc_shard),
                         v_conv_w.reshape(n_layers * 3, dc_shard), "adamw_conv_w"),
        "pool_w": exchanged("pool_w", pool_w, m_pool_w, v_pool_w),
        "pool_scale": update([small[1]], pool_scale, m_pool_scale, v_pool_scale, "adamw_pool_scale"),
        "w_out": exchanged("w_out", w_out, m_w_out, v_w_out),
        "norm_mix": update([small[2]], norm_mix, m_norm_mix, v_norm_mix, "adamw_norm_mix"),
        "norm_mlp": update([small[3]], norm_mlp, m_norm_mlp, v_norm_mlp, "adamw_norm_mlp"),
        "w_up": exchanged("w_up", w_up, m_w_up, v_w_up),
        "w_down": exchanged("w_down", w_down, m_w_down, v_w_down),
        "norm_final": update([small[4]], norm_final, m_norm_final, v_norm_final, "adamw_norm_final"),
    }
    results["conv_w"] = [o.reshape(conv_w.shape) for o in results["conv_w"]]
    order = ("w_in", "conv_w", "pool_w", "pool_scale", "w_out", "norm_mix", "norm_mlp", "w_up", "w_down", "norm_final")
    return (loss, grad_x, *[results[k][0] for k in order], *[results[k][1] for k in order],
            *[results[k][2] for k in order], *[results[k][3] for k in order])
```

```python
import functools

import jax
import jax.numpy as jnp
from jax import lax
from jax.experimental import pallas as pl
from jax.experimental.pallas import tpu as pltpu

F32 = jnp.float32
BF16 = jnp.bfloat16

N_DEV = 8
MESH_AXES = ("x", "y", "c")
NORM_EPS = 1e-6
POOL_WINDOWS = (2, 4, 8, 16)
HALO_ROWS = 16

ADAM_LR = 0.001
ADAM_B1 = 0.9
ADAM_B2 = 0.999
ADAM_EPS = 1e-08
ADAM_WD = 0.01
ADAM_STEP = 10

VMEM_BYTES_V7X = 64 * 1024 * 1024
VMEM_LIMIT = (VMEM_BYTES_V7X * 3) // 4

MM_TILE = 1024
MM_TILE_K = 2048
ROW_TILE = 256
NORM_BWD_TILE = 512
MIXER_TILE = 512
ADAM_BLOCK_ELEMS = 64 * 1024
CAST_BLOCK_ELEMS = 512 * 1024


def _params(*semantics):
    return pltpu.CompilerParams(dimension_semantics=semantics, vmem_limit_bytes=VMEM_LIMIT)


def _tile(dim, pref):
    t = min(dim, pref)
    assert dim % t == 0, (dim, pref)
    return t


def _mesh_position():
    return lax.axis_index("x"), lax.axis_index("y"), lax.axis_index("c")


def _slot(p):
    return 4 * p[0] + 2 * p[1] + p[2]


def _all_gather(arrs, name):
    n = len(arrs)

    def body(*refs):
        ins, outs = refs[:n], refs[n:2 * n]
        send_sems, recv_sems, local_sems = refs[2 * n:]
        x, y, c = _mesh_position()
        me, sibling = (x, y, c), (x, y, 1 - c)
        chips = [(1 - x, y), (x, 1 - y), (1 - x, 1 - y)]

        def copy(a, k, block, to, src=None):
            dst = outs[a].at[_slot(block)]
            return pltpu.make_async_remote_copy(
                src_ref=dst if src is None else src, dst_ref=dst,
                send_sem=send_sems.at[a, k], recv_sem=recv_sems.at[a, k],
                device_id=to, device_id_type=pl.DeviceIdType.MESH)

        mine = [pltpu.make_async_copy(ins[a], outs[a].at[_slot(me)], local_sems.at[a]) for a in range(n)]
        for cp in mine:
            cp.start()
        first = []
        for a in range(n):
            first.append(copy(a, 0, me, sibling, src=ins[a]))
            first += [copy(a, 1 + j, me, (*chip, c), src=ins[a]) for j, chip in enumerate(chips)]
        for cp in first:
            cp.start()
        passed = []
        for j, chip in enumerate(chips):
            for a in range(n):
                copy(a, 1 + j, (*chip, c), me).wait_recv()
                cp = copy(a, 4 + j, (*chip, c), sibling)
                cp.start()
                passed.append(cp)
        for a in range(n):
            copy(a, 0, sibling, me).wait_recv()
            for j, chip in enumerate(chips):
                copy(a, 4 + j, (*chip, 1 - c), me).wait_recv()
        for cp in first + passed:
            cp.wait_send()
        for cp in mine:
            cp.wait()

    any_spec = pl.BlockSpec(memory_space=pl.ANY)
    return pl.pallas_call(
        body, name=name,
        out_shape=[jax.ShapeDtypeStruct((N_DEV, *a.shape), a.dtype) for a in arrs],
        in_specs=[any_spec] * n, out_specs=[any_spec] * n,
        scratch_shapes=[pltpu.SemaphoreType.DMA((n, 7)), pltpu.SemaphoreType.DMA((n, 7)),
                        pltpu.SemaphoreType.DMA((n,))],
    )(*arrs)


def _peer(k):
    x, y, c = _mesh_position()
    return (1 - x if k & 4 else x, 1 - y if k & 2 else y, 1 - c if k & 1 else c)


_HBM = pl.BlockSpec(memory_space=pltpu.HBM)
_SEM = pl.BlockSpec(memory_space=pltpu.SEMAPHORE)
_EFFECT = pltpu.SideEffectType.DATAFLOW_SIDE_EFFECTING


def _exchange_copy(g_ref, land_ref, send_sems, recv_sems, a, k):
    return pltpu.make_async_remote_copy(
        src_ref=g_ref.at[_slot(_peer(k))], dst_ref=land_ref.at[k - 1],
        send_sem=send_sems.at[a * (N_DEV - 1) + k - 1], recv_sem=recv_sems.at[a * (N_DEV - 1) + k - 1],
        device_id=_peer(k), device_id_type=pl.DeviceIdType.MESH)


def _exchange_start(arrs, name):
    n = len(arrs)

    def body(*refs):
        g_refs, land_refs = refs[:n], refs[n:2 * n]
        send_sems, recv_sems = refs[2 * n:2 * n + 2]
        token = refs[-1]
        for k in range(1, N_DEV):
            for a in range(n):
                _exchange_copy(g_refs[a], land_refs[a], send_sems, recv_sems, a, k).start()
        token[...] = jnp.zeros_like(token)

    lands = [lax.empty((N_DEV - 1, *g.shape[1:]), g.dtype) for g in arrs]
    outs = pl.pallas_call(
        body, name=name,
        out_shape=(pltpu.SemaphoreType.DMA((n * (N_DEV - 1),)), pltpu.SemaphoreType.DMA((n * (N_DEV - 1),)),
                   *[pltpu.HBM(g.shape, g.dtype) for g in arrs], *[pltpu.HBM(z.shape, z.dtype) for z in lands],
                   jax.ShapeDtypeStruct((8, 128), F32)),
        in_specs=[_HBM] * (2 * n),
        out_specs=(_SEM, _SEM, *[_HBM] * (2 * n), pl.BlockSpec(memory_space=pltpu.VMEM)),
        input_output_aliases={i: 2 + i for i in range(2 * n)},
        compiler_params=pltpu.CompilerParams(has_side_effects=_EFFECT),
    )(*[pltpu.with_memory_space_constraint(g, pltpu.HBM) for g in arrs],
      *[pltpu.with_memory_space_constraint(z, pltpu.HBM) for z in lands])
    return outs[0], outs[1], list(outs[2:2 + n]), list(outs[2 + n:2 + 2 * n]), outs[-1]


def _exchange_wait(send_sems, recv_sems, arrs, lands, after, name):
    n = len(arrs)

    def body(*refs):
        g_refs, land_refs = refs[:n], refs[n:2 * n]
        send_sems_ref, recv_sems_ref = refs[2 * n:2 * n + 2]
        for k in range(1, N_DEV):
            for a in range(n):
                cp = _exchange_copy(g_refs[a], land_refs[a], send_sems_ref, recv_sems_ref, a, k)
                cp.wait_send()
                cp.wait_recv()

    outs = pl.pallas_call(
        body, name=name,
        out_shape=(*[pltpu.HBM(g.shape, g.dtype) for g in arrs], *[pltpu.HBM(z.shape, z.dtype) for z in lands]),
        in_specs=[_HBM] * (2 * n) + [_SEM, _SEM, pl.BlockSpec(memory_space=pl.ANY)],
        out_specs=[_HBM] * (2 * n),
        input_output_aliases={i: i for i in range(2 * n)},
        compiler_params=pltpu.CompilerParams(has_side_effects=_EFFECT),
    )(*arrs, *lands, send_sems, recv_sems, after)
    return list(outs[:n]), list(outs[n:])


N_FIRST = 4
N_PASSED = 3


def _other_chips():
    x, y, _ = _mesh_position()
    return [(1 - x, y), (x, 1 - y), (1 - x, 1 - y)]


def _gather_copy(src_ref, land_ref, block, to, send_sem, recv_sem):
    rows = land_ref.at[_slot(block)]
    return pltpu.make_async_remote_copy(
        src_ref=rows if src_ref is None else src_ref, dst_ref=rows, send_sem=send_sem, recv_sem=recv_sem,
        device_id=to, device_id_type=pl.DeviceIdType.MESH)


def _gather_start(arrs, me_slot, after, name):
    n = len(arrs)

    def body(*refs):
        src_refs, land_refs = refs[:n], refs[n:2 * n]
        send_sems, recv_sems = refs[2 * n + 1:2 * n + 3]
        token = refs[-1]
        x, y, c = _mesh_position()
        targets = [(x, y, 1 - c)] + [(*chip, c) for chip in _other_chips()]
        for a in range(n):
            for k, to in enumerate(targets):
                _gather_copy(src_refs[a], land_refs[a], (x, y, c), to,
                             send_sems.at[a * N_FIRST + k], recv_sems.at[a * N_FIRST + k]).start()
        token[...] = jnp.zeros_like(token)

    lands = [lax.dynamic_update_slice_in_dim(lax.empty((N_DEV, *a.shape), a.dtype), a[None], me_slot, axis=0)
             for a in arrs]
    outs = pl.pallas_call(
        body, name=name,
        out_shape=(pltpu.SemaphoreType.DMA((n * N_FIRST,)), pltpu.SemaphoreType.DMA((n * N_FIRST,)),
                   *[pltpu.HBM(a.shape, a.dtype) for a in arrs], *[pltpu.HBM(z.shape, z.dtype) for z in lands],
                   jax.ShapeDtypeStruct((8, 128), F32)),
        in_specs=[_HBM] * (2 * n) + [pl.BlockSpec(memory_space=pl.ANY)],
        out_specs=(_SEM, _SEM, *[_HBM] * (2 * n), pl.BlockSpec(memory_space=pltpu.VMEM)),
        input_output_aliases={i: 2 + i for i in range(2 * n)},
        compiler_params=pltpu.CompilerParams(has_side_effects=_EFFECT),
    )(*[pltpu.with_memory_space_constraint(a, pltpu.HBM) for a in arrs],
      *[pltpu.with_memory_space_constraint(z, pltpu.HBM) for z in lands], after)
    return outs[0], outs[1], list(outs[2:2 + n]), list(outs[2 + n:2 + 2 * n]), outs[-1]


def _gather_pass_on(first_recv_sems, base, lands, after, name):
    n = len(lands)

    def body(*refs):
        land_refs = refs[:n]
        first_recv = refs[n]
        send_sems, recv_sems = refs[n + 2:n + 4]
        x, y, c = _mesh_position()
        sibling = (x, y, 1 - c)
        for j, chip in enumerate(_other_chips()):
            for a in range(n):
                _gather_copy(None, land_refs[a], (*chip, c), sibling,
                             send_sems.at[a * N_PASSED + j], first_recv.at[(base + a) * N_FIRST + 1 + j]).wait_recv()
                _gather_copy(None, land_refs[a], (*chip, c), sibling,
                             send_sems.at[a * N_PASSED + j], recv_sems.at[a * N_PASSED + j]).start()

    outs = pl.pallas_call(
        body, name=name,
        out_shape=(pltpu.SemaphoreType.DMA((n * N_PASSED,)), pltpu.SemaphoreType.DMA((n * N_PASSED,)),
                   *[pltpu.HBM(z.shape, z.dtype) for z in lands]),
        in_specs=[_HBM] * n + [_SEM, pl.BlockSpec(memory_space=pl.ANY)],
        out_specs=(_SEM, _SEM, *[_HBM] * n),
        input_output_aliases={i: 2 + i for i in range(n)},
        compiler_params=pltpu.CompilerParams(has_side_effects=_EFFECT),
    )(*lands, first_recv_sems, after)
    return outs[0], outs[1], list(outs[2:])


def _gather_finish(first_sems, base, passed_sems, arrs, lands, after, name):
    n = len(lands)

    def body(*refs):
        src_refs, land_refs = refs[:n], refs[n:2 * n]
        first_send, first_recv, passed_send, passed_recv = refs[2 * n:2 * n + 4]
        x, y, c = _mesh_position()
        sibling = (x, y, 1 - c)
        chips = _other_chips()
        for a in range(n):
            _gather_copy(src_refs[a], land_refs[a], sibling, sibling,
                         first_send.at[(base + a) * N_FIRST], first_recv.at[(base + a) * N_FIRST]).wait_recv()
            for j, chip in enumerate(chips):
                _gather_copy(None, land_refs[a], (*chip, 1 - c), sibling,
                             passed_send.at[a * N_PASSED + j], passed_recv.at[a * N_PASSED + j]).wait_recv()
        for a in range(n):
            for k in range(N_FIRST):
                _gather_copy(src_refs[a], land_refs[a], (x, y, c), sibling,
                             first_send.at[(base + a) * N_FIRST + k], first_recv.at[(base + a) * N_FIRST + k]).wait_send()
            for j, chip in enumerate(chips):
                _gather_copy(None, land_refs[a], (*chip, c), sibling,
                             passed_send.at[a * N_PASSED + j], passed_recv.at[a * N_PASSED + j]).wait_send()

    outs = pl.pallas_call(
        body, name=name,
        out_shape=(*[pltpu.HBM(a.shape, a.dtype) for a in arrs], *[pltpu.HBM(z.shape, z.dtype) for z in lands]),
        in_specs=[_HBM] * (2 * n) + [_SEM] * 4 + [pl.BlockSpec(memory_space=pl.ANY)],
        out_specs=[_HBM] * (2 * n),
        input_output_aliases={i: i for i in range(2 * n)},
        compiler_params=pltpu.CompilerParams(has_side_effects=_EFFECT),
    )(*arrs, *lands, *first_sems, *passed_sems, after)
    return list(outs[n:])


def _run_after(deps, body, in_specs, operands):
    n_deps = len(deps)
    if n_deps == 0:
        return body, in_specs, operands

    def body_behind(*refs):
        body(*refs[n_deps:])

    return body_behind, [pl.BlockSpec(memory_space=pl.ANY)] * n_deps + list(in_specs), list(deps) + list(operands)


def _zero_at_first(acc_ref, k, nk):
    if nk > 1:
        @pl.when(k == 0)
        def _():
            acc_ref[...] = jnp.zeros_like(acc_ref)


def _accumulate(acc_ref, product, k, nk, finish):
    if nk == 1:
        finish(product())
        return

    @pl.when(k < nk - 1)
    def _():
        acc_ref[...] += product()

    @pl.when(k == nk - 1)
    def _():
        finish(acc_ref[...] + product())


def _blocks_per_tile(n_blocks, width, pref):
    if width >= pref:
        return 1
    per = min(n_blocks, pref // width)
    assert n_blocks % per == 0
    return per


def _mm_nn(a, b3, *, out_dtype, name, res=None, with_relu2=False, deps=()):
    m, kdim = a.shape
    nb, kb_, nw = b3.shape
    assert kb_ == kdim
    per_tile = _blocks_per_tile(nb, nw, MM_TILE)
    tw = _tile(nw, MM_TILE)
    per_block = nw // tw
    tn = per_tile * tw
    tm, tk = _tile(m, MM_TILE), _tile(kdim, MM_TILE_K)
    nk = kdim // tk
    n_in = 2 + (res is not None)

    def body(*refs):
        a_ref, b_ref = refs[:2]
        r_ref = refs[2] if res is not None else None
        o_ref = refs[n_in]
        s_ref = refs[n_in + 1] if with_relu2 else None
        acc_ref = refs[-1] if nk > 1 else None
        k = pl.program_id(2)
        _zero_at_first(acc_ref, k, nk)
        for b in range(per_tile):
            cols = slice(b * tw, (b + 1) * tw)

            def product(b=b):
                return jnp.dot(a_ref[...], b_ref[b], preferred_element_type=F32)

            def finish(total, cols=cols):
                if r_ref is not None:
                    total = total + r_ref[:, cols]
                o_ref[:, cols] = total.astype(out_dtype)
                if s_ref is not None:
                    r = jnp.maximum(total, 0.0)
                    s_ref[:, cols] = (r * r).astype(BF16)

            _accumulate(acc_ref.at[:, cols] if nk > 1 else None, product, k, nk, finish)

    if per_tile > 1:
        b_spec = pl.BlockSpec((per_tile, tk, nw), lambda i, j, k: (j, k, 0))
    else:
        b_spec = pl.BlockSpec((1, tk, tw), lambda i, j, k: (j // per_block, k, j % per_block))
    in_specs = [pl.BlockSpec((tm, tk), lambda i, j, k: (i, k)), b_spec]
    operands = [a, b3]
    if res is not None:
        in_specs.append(pl.BlockSpec((tm, tn), lambda i, j, k: (i, j)))
        operands.append(res)
    body, in_specs, operands = _run_after(deps, body, in_specs, operands)
    o_spec = pl.BlockSpec((tm, tn), lambda i, j, k: (i, j))
    out_shape = [jax.ShapeDtypeStruct((m, nb * nw), out_dtype)]
    if with_relu2:
        out_shape.append(jax.ShapeDtypeStruct((m, nb * nw), BF16))
    outs = pl.pallas_call(
        body, name=name, grid=(m // tm, (nb * nw) // tn, nk),
        out_shape=out_shape, in_specs=in_specs, out_specs=[o_spec] * len(out_shape),
        scratch_shapes=[pltpu.VMEM((tm, tn), F32)] if nk > 1 else [],
        compiler_params=_params("parallel", "parallel", "arbitrary"),
    )(*operands)
    return outs if with_relu2 else outs[0]


def _mm_nt(a, b3, *, out_dtype, name, relu2_grad_of=None, deps=()):
    m, kdim = a.shape
    kb, n, kw = b3.shape
    assert kb * kw == kdim
    per_step = _blocks_per_tile(kb, kw, MM_TILE_K)
    tw = _tile(kw, MM_TILE_K)
    per_block = kw // tw
    tk = per_step * tw
    tm, tn = _tile(m, MM_TILE), _tile(n, MM_TILE)
    nk = kdim // tk

    def body(*refs):
        if relu2_grad_of is None:
            a_ref, b_ref, o_ref = refs[:3]
            g_ref = None
        else:
            a_ref, b_ref, g_ref, o_ref = refs[:4]
        acc_ref = refs[-1] if nk > 1 else None
        k = pl.program_id(2)
        _zero_at_first(acc_ref, k, nk)
        def product():
            p = None
            for b in range(per_step):
                part = lax.dot_general(a_ref[:, b * tw:(b + 1) * tw], b_ref[b], (((1,), (1,)), ((), ())),
                                       preferred_element_type=F32)
                p = part if p is None else p + part
            return p

        def finish(total):
            if g_ref is not None:
                total = total * (2.0 * jnp.maximum(g_ref[...].astype(F32), 0.0))
            o_ref[...] = total.astype(out_dtype)

        _accumulate(acc_ref, product, k, nk, finish)

    if per_step > 1:
        b_spec = pl.BlockSpec((per_step, tn, kw), lambda i, j, k: (k, j, 0))
    else:
        b_spec = pl.BlockSpec((1, tn, tw), lambda i, j, k: (k // per_block, j, k % per_block))
    in_specs = [pl.BlockSpec((tm, tk), lambda i, j, k: (i, k)), b_spec]
    operands = [a, b3]
    if relu2_grad_of is not None:
        in_specs.append(pl.BlockSpec((tm, tn), lambda i, j, k: (i, j)))
        operands.append(relu2_grad_of)
    body, in_specs, operands = _run_after(deps, body, in_specs, operands)
    return pl.pallas_call(
        body, name=name, grid=(m // tm, n // tn, nk),
        out_shape=jax.ShapeDtypeStruct((m, n), out_dtype),
        in_specs=in_specs, out_specs=pl.BlockSpec((tm, tn), lambda i, j, k: (i, j)),
        scratch_shapes=[pltpu.VMEM((tm, tn), F32)] if nk > 1 else [],
        compiler_params=_params("parallel", "parallel", "arbitrary"),
    )(*operands)


def _mm_tn(a, b, *, n_blocks, name, deps=()):
    t, m = a.shape
    t2, n = b.shape
    assert t == t2 and n % n_blocks == 0
    nw = n // n_blocks
    per_tile = _blocks_per_tile(n_blocks, nw, MM_TILE)
    tw = _tile(nw, MM_TILE)
    per_block = nw // tw
    tn = per_tile * tw
    tm, tk = _tile(m, MM_TILE), _tile(t, MM_TILE_K)
    nk = t // tk

    def body(a_ref, b_ref, o_ref, *scratch):
        acc_ref = scratch[0] if nk > 1 else None
        k = pl.program_id(2)
        _zero_at_first(acc_ref, k, nk)
        def product():
            return lax.dot_general(a_ref[...], b_ref[...], (((0,), (0,)), ((), ())), preferred_element_type=F32)

        def finish(total):
            for b in range(per_tile):
                o_ref[b] = total[:, b * tw:(b + 1) * tw].astype(BF16)

        _accumulate(acc_ref, product, k, nk, finish)

    if per_tile > 1:
        o_spec = pl.BlockSpec((per_tile, tm, nw), lambda i, j, k: (j, i, 0))
    else:
        o_spec = pl.BlockSpec((1, tm, tw), lambda i, j, k: (j // per_block, i, j % per_block))
    in_specs = [pl.BlockSpec((tk, tm), lambda i, j, k: (k, i)), pl.BlockSpec((tk, tn), lambda i, j, k: (k, j))]
    body, in_specs, operands = _run_after(deps, body, in_specs, [a, b])
    return pl.pallas_call(
        body, name=name, grid=(m // tm, n // tn, nk),
        out_shape=jax.ShapeDtypeStruct((n_blocks, m, nw), BF16),
        in_specs=in_specs, out_specs=o_spec,
        scratch_shapes=[pltpu.VMEM((tm, tn), F32)] if nk > 1 else [],
        compiler_params=_params("parallel", "parallel", "arbitrary"),
    )(*operands)


def _normalise(x):
    r = lax.rsqrt(jnp.mean(x * x, axis=-1, keepdims=True) + NORM_EPS)
    return x * r, r


def _rmsnorm_backward(dh, xhat, r, gain):
    dxhat = dh * gain
    return r * (dxhat - xhat * jnp.mean(dxhat * xhat, axis=-1, keepdims=True))


def _rmsnorm(x, gain, name, deps=()):
    t, d = x.shape
    tr = _tile(t, ROW_TILE)

    def body(x_ref, g_ref, o_ref):
        xhat, _ = _normalise(x_ref[...])
        o_ref[...] = (xhat * g_ref[...]).astype(BF16)

    in_specs = [pl.BlockSpec((tr, d), lambda i: (i, 0)), pl.BlockSpec((1, d), lambda i: (0, 0))]
    body, in_specs, operands = _run_after(deps, body, in_specs, [x, gain.reshape(1, d)])
    return pl.pallas_call(
        body, name=name, grid=(t // tr,),
        out_shape=jax.ShapeDtypeStruct((t, d), BF16),
        in_specs=in_specs,
        out_specs=pl.BlockSpec((tr, d), lambda i: (i, 0)),
        compiler_params=_params("parallel"),
    )(*operands)


def _rmsnorm_bwd(dh, x, gain, dres, out_dtype, name):
    t, d = x.shape
    tr = _tile(t, NORM_BWD_TILE)

    def body(dh_ref, x_ref, g_ref, r_ref, dx_ref, dg_ref):
        xhat, r = _normalise(x_ref[...])
        dh_v = dh_ref[...].astype(F32)
        dx = r_ref[...].astype(F32) + _rmsnorm_backward(dh_v, xhat, r, g_ref[...])
        dx_ref[...] = dx.astype(out_dtype)
        part = jnp.sum(dh_v * xhat, axis=0, keepdims=True)

        @pl.when(pl.program_id(0) == 0)
        def _():
            dg_ref[...] = part

        @pl.when(pl.program_id(0) > 0)
        def _():
            dg_ref[...] += part

    row = pl.BlockSpec((tr, d), lambda i: (i, 0))
    vec = pl.BlockSpec((1, d), lambda i: (0, 0))
    return pl.pallas_call(
        body, name=name, grid=(t // tr,),
        out_shape=[jax.ShapeDtypeStruct((t, d), out_dtype), jax.ShapeDtypeStruct((1, d), F32)],
        in_specs=[row, row, vec, row], out_specs=[row, vec],
        compiler_params=_params("arbitrary"),
    )(dh, x, gain.reshape(1, d), dres)


def _loss_head(x, gain, target, name):
    t, d = x.shape
    tr = _tile(t, ROW_TILE)
    steps = t // tr

    def body(x_ref, g_ref, t_ref, loss_ref, dxb_ref, dg_ref, sq_ref):
        i = pl.program_id(0)
        xhat, r = _normalise(x_ref[...])
        gain_v = g_ref[...]
        diff = xhat * gain_v - t_ref[...]
        dy = diff / float(d)
        dxb_ref[...] = _rmsnorm_backward(dy, xhat, r, gain_v).astype(BF16)
        dg_part = jnp.sum(dy * xhat, axis=0, keepdims=True)
        sq_part = jnp.sum(diff * diff, axis=0, keepdims=True)

        @pl.when(i == 0)
        def _():
            dg_ref[...] = dg_part
            sq_ref[...] = sq_part

        @pl.when(i > 0)
        def _():
            dg_ref[...] += dg_part
            sq_ref[...] += sq_part

        @pl.when(i == steps - 1)
        def _():
            loss_ref[...] = (0.5 / float(d)) * jnp.sum(sq_ref[...], axis=1, keepdims=True)

    row = pl.BlockSpec((tr, d), lambda i: (i, 0))
    vec = pl.BlockSpec((1, d), lambda i: (0, 0))
    return pl.pallas_call(
        body, name=name, grid=(steps,),
        out_shape=[jax.ShapeDtypeStruct((1, 1), F32), jax.ShapeDtypeStruct((t, d), BF16),
                   jax.ShapeDtypeStruct((1, d), F32)],
        in_specs=[row, vec, row],
        out_specs=[pl.BlockSpec((1, 1), lambda i: (0, 0)), row, vec],
        scratch_shapes=[pltpu.VMEM((1, d), F32)],
        compiler_params=_params("arbitrary"),
    )(x, gain.reshape(1, d), target)


def _shift_down(ext, s):
    return pltpu.roll(ext, s, 0)


def _shift_up(ext, s):
    return pltpu.roll(ext, ext.shape[0] - s, 0)


def _window_sum(ext, w, shift):
    s = 1
    while s < w:
        ext = ext + shift(ext, s)
        s *= 2
    return ext


def _window_count(tile_index, rows, cols, w):
    t = tile_index * rows + lax.broadcasted_iota(jnp.int32, (rows, cols), 0)
    return jnp.minimum(t + 1, w).astype(F32)


def _mixer_sizes(proj, conv_w):
    t, e = proj.shape
    dc = conv_w.shape[1]
    dp = e - 3 * dc
    cg = dp // len(POOL_WINDOWS)
    tt = _tile(t, MIXER_TILE)
    assert tt % HALO_ROWS == 0 and tt >= HALO_ROWS
    cw = _tile(dc, cg)
    return t, e, dc, dp, cg, tt, cw


def _mixer_fwd(proj, conv_w, pool_w, pool_scale, name, deps=()):
    t, e, dc, dp, cg, tt, cw = _mixer_sizes(proj, conv_w)
    per_halo = tt // HALO_ROWS

    def body(cur_ref, prev_ref, cw_ref, pw_ref, ps_ref, y_ref):
        i = pl.program_id(0)
        first = i == 0

        def cur(lo, width):
            return cur_ref[:, lo:lo + width].astype(F32)

        def prev(lo, width):
            return jnp.where(first, 0.0, prev_ref[:, lo:lo + width].astype(F32))

        for lo in range(0, dc, cw):
            u = cur(dc + lo, cw) * cur(2 * dc + lo, cw)
            ext = jnp.concatenate([prev(dc + lo, cw) * prev(2 * dc + lo, cw), u], axis=0)
            u1 = _shift_down(ext, 1)[HALO_ROWS:]
            u2 = _shift_down(ext, 2)[HALO_ROWS:]
            conv = cw_ref[0:1, lo:lo + cw] * u2 + cw_ref[1:2, lo:lo + cw] * u1 + cw_ref[2:3, lo:lo + cw] * u
            y_ref[:, lo:lo + cw] = (cur(lo, cw) * conv).astype(BF16)

        for g, w in enumerate(POOL_WINDOWS):
            lo = 3 * dc + g * cg
            v = cur(lo, cg)
            ext = jnp.concatenate([prev(lo, cg), v], axis=0)
            mean = _window_sum(ext, w, _shift_down)[HALO_ROWS:] / _window_count(i, tt, cg, w)
            z = jnp.dot((mean - v).astype(BF16), pw_ref[g], preferred_element_type=F32)
            y_ref[:, dc + g * cg:dc + (g + 1) * cg] = (z * ps_ref[0:1, g * cg:(g + 1) * cg]).astype(BF16)

    in_specs = [pl.BlockSpec((tt, e), lambda i: (i, 0)),
                pl.BlockSpec((HALO_ROWS, e), lambda i: (jnp.maximum(i * per_halo - 1, 0), 0)),
                pl.BlockSpec((3, dc), lambda i: (0, 0)),
                pl.BlockSpec((len(POOL_WINDOWS), cg, cg), lambda i: (0, 0, 0)),
                pl.BlockSpec((1, dp), lambda i: (0, 0))]
    body, in_specs, operands = _run_after(
        deps, body, in_specs, [proj, proj, conv_w, pool_w, pool_scale.reshape(1, dp)])
    return pl.pallas_call(
        body, name=name, grid=(t // tt,),
        out_shape=jax.ShapeDtypeStruct((t, dc + dp), BF16),
        in_specs=in_specs,
        out_specs=pl.BlockSpec((tt, dc + dp), lambda i: (i, 0)),
        compiler_params=_params("parallel"),
    )(*operands)


def _mixer_bwd(proj, dy, conv_w, pool_w, pool_scale, name, deps=()):
    t, e, dc, dp, cg, tt, cw = _mixer_sizes(proj, conv_w)
    per_halo = tt // HALO_ROWS
    steps = t // tt
    n_groups = len(POOL_WINDOWS)

    def body(cur_ref, prev_ref, next_ref, dy_ref, dyn_ref, cw_ref, pw_ref, ps_ref,
             dp_ref, dcw_ref, dpw_ref, dps_ref):
        i = pl.program_id(0)
        first = i == 0
        last = i == steps - 1

        @pl.when(first)
        def _():
            dcw_ref[...] = jnp.zeros_like(dcw_ref)
            dpw_ref[...] = jnp.zeros_like(dpw_ref)
            dps_ref[...] = jnp.zeros_like(dps_ref)

        def cur(lo, width):
            return cur_ref[:, lo:lo + width].astype(F32)

        def prev(lo, width):
            return jnp.where(first, 0.0, prev_ref[:, lo:lo + width].astype(F32))

        def nxt(ref, lo, width):
            return jnp.where(last, 0.0, ref[:, lo:lo + width].astype(F32))

        def colsum(v):
            return jnp.sum(v, axis=0, keepdims=True)

        for lo in range(0, dc, cw):
            cols = slice(lo, lo + cw)
            b, c, xt = cur(lo, cw), cur(dc + lo, cw), cur(2 * dc + lo, cw)
            u = c * xt
            ext = jnp.concatenate([prev(dc + lo, cw) * prev(2 * dc + lo, cw), u], axis=0)
            u1 = _shift_down(ext, 1)[HALO_ROWS:]
            u2 = _shift_down(ext, 2)[HALO_ROWS:]
            w0, w1, w2 = cw_ref[0:1, cols], cw_ref[1:2, cols], cw_ref[2:3, cols]
            dyc = dy_ref[:, cols].astype(F32)
            dp_ref[:, cols] = (dyc * (w0 * u2 + w1 * u1 + w2 * u)).astype(BF16)
            dconv = dyc * b
            dcw_ref[0:1, cols] += colsum(dconv * u2)
            dcw_ref[1:2, cols] += colsum(dconv * u1)
            dcw_ref[2:3, cols] += colsum(dconv * u)
            dext = jnp.concatenate([dconv, nxt(dyn_ref, lo, cw) * nxt(next_ref, lo, cw)], axis=0)
            du = w2 * dconv + w1 * _shift_up(dext, 1)[:tt] + w0 * _shift_up(dext, 2)[:tt]
            dp_ref[:, dc + lo:dc + lo + cw] = (du * xt).astype(BF16)
            dp_ref[:, 2 * dc + lo:2 * dc + lo + cw] = (du * c).astype(BF16)

        for g, w in enumerate(POOL_WINDOWS):
            lo = 3 * dc + g * cg
            ycols = slice(dc + g * cg, dc + (g + 1) * cg)
            pcols = slice(g * cg, (g + 1) * cg)
            v = cur(lo, cg)
            ext = jnp.concatenate([prev(lo, cg), v], axis=0)
            count = _window_count(i, tt, cg, w)
            d = ((_window_sum(ext, w, _shift_down)[HALO_ROWS:] / count) - v).astype(BF16)
            pw = pw_ref[g]
            scale = ps_ref[0:1, pcols]
            dyp = dy_ref[:, ycols].astype(F32)
            z = jnp.dot(d, pw, preferred_element_type=F32)
            dps_ref[0:1, pcols] += colsum(dyp * z)
            dz = (dyp * scale).astype(BF16)
            dpw_ref[g] += lax.dot_general(d, dz, (((0,), (0,)), ((), ())), preferred_element_type=F32)
            dd = lax.dot_general(dz, pw, (((1,), (1,)), ((), ())), preferred_element_type=F32)
            dzn = (nxt(dyn_ref, dc + g * cg, cg) * scale).astype(BF16)
            ddn = lax.dot_general(dzn, pw, (((1,), (1,)), ((), ())), preferred_element_type=F32)
            qext = jnp.concatenate([dd / count, ddn / float(w)], axis=0)
            dp_ref[:, lo:lo + cg] = (_window_sum(qext, w, _shift_up)[:tt] - dd).astype(BF16)

    cur_spec = lambda width: pl.BlockSpec((tt, width), lambda i: (i, 0))
    prev_spec = pl.BlockSpec((HALO_ROWS, e), lambda i: (jnp.maximum(i * per_halo - 1, 0), 0))
    next_spec = lambda width: pl.BlockSpec(
        (HALO_ROWS, width), lambda i: (jnp.minimum((i + 1) * per_halo, t // HALO_ROWS - 1), 0))
    in_specs = [cur_spec(e), prev_spec, next_spec(e), cur_spec(dc + dp), next_spec(dc + dp),
                pl.BlockSpec((3, dc), lambda i: (0, 0)),
                pl.BlockSpec((n_groups, cg, cg), lambda i: (0, 0, 0)),
                pl.BlockSpec((1, dp), lambda i: (0, 0))]
    body, in_specs, operands = _run_after(
        deps, body, in_specs, [proj, proj, proj, dy, dy, conv_w, pool_w, pool_scale.reshape(1, dp)])
    return pl.pallas_call(
        body, name=name, grid=(steps,),
        out_shape=[jax.ShapeDtypeStruct((t, e), BF16), jax.ShapeDtypeStruct((3, dc), F32),
                   jax.ShapeDtypeStruct((n_groups, cg, cg), F32), jax.ShapeDtypeStruct((1, dp), F32)],
        in_specs=in_specs,
        out_specs=[cur_spec(e), pl.BlockSpec((3, dc), lambda i: (0, 0)),
                   pl.BlockSpec((n_groups, cg, cg), lambda i: (0, 0, 0)),
                   pl.BlockSpec((1, dp), lambda i: (0, 0))],
        compiler_params=_params("arbitrary"),
    )(*operands)


def _adamw(partials, w, m, v, me_slot, name, mine=()):
    n_layers, r, c = w.shape
    assert len(partials) == n_layers and len(mine) in (0, n_layers)
    n_mine = len(mine)
    tr = r if r * c <= ADAM_BLOCK_ELEMS else _tile(r, max(16, ADAM_BLOCK_ELEMS // c))

    def body(me_ref, *refs):
        own_refs, p_refs = refs[:n_mine], refs[n_mine:n_mine + n_layers]
        w_ref, m_ref, v_ref, g_out, d_out, m_out, v_out = refs[n_mine + n_layers:]
        for l in range(n_layers):
            g = own_refs[l][0].astype(F32) if n_mine else None
            for s in range(p_refs[l].shape[0]):
                part = p_refs[l][s].astype(F32)
                g = part if g is None else g + part
            m_new = ADAM_B1 * m_ref[l] + (1.0 - ADAM_B1) * g
            v_new = ADAM_B2 * v_ref[l] + (1.0 - ADAM_B2) * (g * g)
            m_hat = m_new / (1.0 - ADAM_B1 ** ADAM_STEP)
            v_hat = v_new / (1.0 - ADAM_B2 ** ADAM_STEP)
            g_out[l] = g
            d_out[l] = -ADAM_LR * (m_hat / (jnp.sqrt(v_hat) + ADAM_EPS) + ADAM_WD * w_ref[l])
            m_out[l] = m_new
            v_out[l] = v_new

    own_spec = pl.BlockSpec((1, tr, c), lambda i, me_ref: (me_ref[0], i, 0))
    p_specs = [pl.BlockSpec((p.shape[0], tr, c), lambda i, me_ref: (0, i, 0)) for p in partials]
    w_spec = pl.BlockSpec((n_layers, tr, c), lambda i, me_ref: (0, i, 0))
    return pl.pallas_call(
        body, name=name,
        grid_spec=pltpu.PrefetchScalarGridSpec(
            num_scalar_prefetch=1, grid=(r // tr,),
            in_specs=[own_spec] * n_mine + p_specs + [w_spec] * 3, out_specs=[w_spec] * 4),
        out_shape=[jax.ShapeDtypeStruct(w.shape, F32)] * 4,
        compiler_params=_params("parallel"),
    )(jnp.reshape(me_slot, (1,)).astype(jnp.int32), *mine, *partials, w, m, v)


def _to_bf16(w, first, last, name, deps=()):
    _, r, c = w.shape
    tr = r if r * c <= CAST_BLOCK_ELEMS else _tile(r, max(16, CAST_BLOCK_ELEMS // c))

    def body(w_ref, o_ref):
        o_ref[...] = w_ref[...].astype(BF16)

    in_specs = [pl.BlockSpec((1, tr, c), lambda l, i: (first + l, i, 0))]
    body, in_specs, operands = _run_after(deps, body, in_specs, [w])
    return pl.pallas_call(
        body, name=name, grid=(last - first, r // tr),
        out_shape=jax.ShapeDtypeStruct((last - first, r, c), BF16),
        in_specs=in_specs, out_specs=pl.BlockSpec((1, tr, c), lambda l, i: (l, i, 0)),
        compiler_params=_params("parallel", "parallel"),
    )(*operands)


def kernel(x, w_in, conv_w, pool_w, pool_scale, w_out, norm_mix, norm_mlp, w_up, w_down, norm_final, loss_target, m_w_in, m_conv_w, m_pool_w, m_pool_scale, m_w_out, m_norm_mix, m_norm_mlp, m_w_up, m_w_down, m_norm_final, v_w_in, v_conv_w, v_pool_w, v_pool_scale, v_w_out, v_norm_mix, v_norm_mlp, v_w_up, v_w_down, v_norm_final):
    n_layers, d, e_shard = w_in.shape
    t = x.shape[1]
    n_groups, cg_shard, cg = pool_w.shape[1:]
    dc_shard = conv_w.shape[2]
    dc, dp = dc_shard * N_DEV, n_groups * cg
    f_shard = w_up.shape[2]
    xs = x.reshape(t, d)
    target = loss_target.reshape(t, d)

    me = _slot(_mesh_position())
    big = (w_in, w_out, w_up, w_down)
    n_kinds = len(big)
    first_layer = [_to_bf16(w, 0, 1, "cast_first_layer")[0] for w in big]
    mixer_blocks = [conv_w.reshape(n_layers * 3, dc_shard), pool_w.reshape(n_layers * n_groups * cg_shard, cg)]
    groups = {}

    def start(tag, blocks, after):
        send_sems, recv_sems, blocks, lands, token = _gather_start(blocks, me, after, f"weights_start_{tag}")
        groups[tag] = (send_sems, recv_sems, blocks, lands)
        return token

    token = start("first", [first_layer[0], *mixer_blocks, *first_layer[1:]], xs)
    if n_layers > 1:
        others = [_to_bf16(w, 1, n_layers, "cast_other_layers", deps=(token,)) for w in big]
        token = start("rest", [others[k][l - 1] for l in range(1, n_layers) for k in range(n_kinds)], token)
    mixer_place = ("first", 1)

    def place(l, k):
        return ("first", 0 if k == 0 else k + len(mixer_blocks)) if l == 0 else ("rest", n_kinds * (l - 1) + k)

    def passed_on(where, n, after, name):
        tag, i = where
        lands = groups[tag][3]
        send_sems, recv_sems, lands[i:i + n] = _gather_pass_on(groups[tag][1], i, lands[i:i + n], after, f"weights_pass_{name}")
        return send_sems, recv_sems

    def gathered(where, n, passed_sems, after, name):
        tag, i = where
        first_send, first_recv, blocks, lands = groups[tag]
        lands[i:i + n] = _gather_finish((first_send, first_recv), i, passed_sems, blocks[i:i + n], lands[i:i + n], after,
                                        f"weights_finish_{name}")
        return lands[i:i + n]

    def landing(where):
        return groups[where[0]][3][where[1]]

    saved, weights = [], []
    xc = xs
    after = token
    for l in range(n_layers):
        if l == 0:
            h1 = _rmsnorm(xc, norm_mix[l], "norm_mix", deps=(after,))
            sems = passed_on(place(l, 0), 1, h1, f"{l}_0")
        else:
            sems = passed_on(place(l, 0), 1, after, f"{l}_0")
            h1 = _rmsnorm(xc, norm_mix[l], "norm_mix", deps=(landing(place(l, 0)),))
        win, = gathered(place(l, 0), 1, sems, h1, f"{l}_0")
        proj = _mm_nn(h1, win, out_dtype=BF16, name="in_proj")
        if l == 0:
            mixer_sems = passed_on(mixer_place, len(mixer_blocks), proj, "mixer")
        sems = passed_on(place(l, 1), 1, proj, f"{l}_1")
        if l == 0:
            conv_g, pool_g = gathered(mixer_place, len(mixer_blocks), mixer_sems, proj, "mixer")
            conv_full = conv_g.reshape(N_DEV, n_layers, 3, dc_shard).transpose(1, 2, 0, 3).reshape(n_layers, 3, dc)
            pool_full = pool_g.reshape(N_DEV, n_layers, n_groups, cg_shard, cg).transpose(1, 2, 0, 3, 4)
            pool_full = pool_full.reshape(n_layers, n_groups, cg, cg).astype(BF16)
        y = _mixer_fwd(proj, conv_full[l], pool_full[l], pool_scale[l], "mixer_fwd", deps=(landing(place(l, 1)),))
        wout = gathered(place(l, 1), 1, sems, y, f"{l}_1")[0].reshape(1, d, d)
        x1 = _mm_nn(y, wout, out_dtype=F32, res=xc, name="out_proj")
        sems = passed_on(place(l, 2), 1, x1, f"{l}_2")
        h2 = _rmsnorm(x1, norm_mlp[l], "norm_mlp", deps=(landing(place(l, 2)),))
        wup, = gathered(place(l, 2), 1, sems, h2, f"{l}_2")
        if l < 2:
            a, s = _mm_nn(h2, wup, out_dtype=BF16, with_relu2=True, name="mlp_up")
            sems = passed_on(place(l, 3), 1, a, f"{l}_3")
        else:
            sems = passed_on(place(l, 3), 1, h2, f"{l}_3")
            a, s = _mm_nn(h2, wup, out_dtype=BF16, with_relu2=True, name="mlp_up", deps=(landing(place(l, 3)),))
        wdown = gathered(place(l, 3), 1, sems, a, f"{l}_3")[0].reshape(1, f_shard * N_DEV, d)
        x2 = _mm_nn(s, wdown, out_dtype=F32, res=x1, name="mlp_down")
        weights.append((win, wout, wup, wdown))
        saved.append((xc, h1, proj, y, x1, h2, a, s))
        xc = after = x2

    loss_part, dxb, g_norm_final = _loss_head(xc, norm_final, target, "loss_head")
    loss = lax.psum(loss_part[0, 0], MESH_AXES)

    in_flight = {}
    received = {}

    def push(l, arrs, keys, tag):
        send_sems, recv_sems, thru, lands, token = _exchange_start(arrs, f"grads_start_{tag}_{l}")
        in_flight.setdefault(l, []).append((send_sems, recv_sems, thru, lands, keys, tag))
        return token

    def land(l, after):
        for send_sems, recv_sems, thru, lands, keys, tag in in_flight.pop(l):
            thru, lands = _exchange_wait(send_sems, recv_sems, thru, lands, after, f"grads_wait_{tag}_{l}")
            for key, own, got in zip(keys, thru, lands):
                received[key, l] = (own, got)

    g_conv, g_scale, g_mix, g_mlp = ([None] * n_layers for _ in range(4))
    for l in reversed(range(n_layers)):
        win, wout, wup, wdown = weights[l]
        x0, h1, proj, y, x1, h2, a, s = saved[l]
        gw_down = _mm_tn(s, dxb, n_blocks=1, name="mlp_down_dw")
        tok_down = push(l, [gw_down.reshape(N_DEV, f_shard, d)], ["w_down"], "down")
        da = _mm_nt(dxb, wdown, out_dtype=BF16, relu2_grad_of=a, name="mlp_down_dx", deps=(tok_down,))
        gw_up = _mm_tn(h2, da, n_blocks=N_DEV, name="mlp_up_dw")
        tok_up = push(l, [gw_up], ["w_up"], "up")
        dh2 = _mm_nt(da, wup, out_dtype=BF16, name="mlp_up_dx", deps=(tok_up,))
        dx1b, g_mlp[l] = _rmsnorm_bwd(dh2, x1, norm_mlp[l], dxb, BF16, "norm_mlp_bwd")
        dy = _mm_nt(dx1b, wout, out_dtype=BF16, name="out_proj_dx")
        gw_out = _mm_tn(y, dx1b, n_blocks=1, name="out_proj_dw")
        tok_out = push(l, [gw_out.reshape(N_DEV, d // N_DEV, d)], ["w_out"], "out")
        dproj, g_conv[l], gw_pool, g_scale[l] = _mixer_bwd(
            proj, dy, conv_full[l], pool_full[l], pool_scale[l], "mixer_bwd", deps=(tok_out,))
        gw_in = _mm_tn(h1, dproj, n_blocks=N_DEV, name="in_proj_dw")
        gw_pool = gw_pool.reshape(n_groups, N_DEV, cg_shard, cg).transpose(1, 0, 2, 3)
        tok_in = push(l, [gw_in, gw_pool.reshape(N_DEV, n_groups * cg_shard, cg).astype(BF16)], ["w_in", "pool_w"], "in")
        dh1 = _mm_nt(dproj, win, out_dtype=BF16, name="in_proj_dx", deps=(tok_in,))
        dxb, g_mix[l] = _rmsnorm_bwd(dh1, x0, norm_mix[l], dx1b, F32 if l == 0 else BF16, "norm_mix_bwd")
        if l + 1 < n_layers:
            land(l + 1, dxb)
    land(0, dxb)
    grad_x = dxb.reshape(x.shape)

    small = _all_gather(
        [jnp.stack(g_conv).reshape(n_layers * 3, dc), jnp.concatenate(g_scale, axis=0),
         jnp.concatenate(g_mix, axis=0), jnp.concatenate(g_mlp, axis=0), g_norm_final], "gather_small_grads")
    conv_parts = lax.dynamic_slice_in_dim(small[0], me * dc_shard, dc_shard, axis=2)

    def update(partials, w, m, v, name, mine=()):
        shape = w.shape
        rc = (shape[0], -1, shape[-1]) if w.ndim > 2 else (1, *shape) if w.ndim == 2 else (1, 1, *shape)
        outs = _adamw(partials, w.reshape(rc), m.reshape(rc), v.reshape(rc), me, name, mine=mine)
        return [o.reshape(shape) for o in outs]

    def exchanged(key, w, m, v):
        return update([received[key, l][1] for l in range(n_layers)], w, m, v, f"adamw_{key}",
                      mine=[received[key, l][0] for l in range(n_layers)])

    results = {
        "w_in": exchanged("w_in", w_in, m_w_in, v_w_in),
        "conv_w": update([conv_parts], conv_w.reshape(n_layers * 3, dc_shard), m_conv_w.reshape(n_layers * 3, dc_shard),
                         v_conv_w.reshape(n_layers * 3, dc_shard), "adamw_conv_w"),
        "pool_w": exchanged("pool_w", pool_w, m_pool_w, v_pool_w),
        "pool_scale": update([small[1]], pool_scale, m_pool_scale, v_pool_scale, "adamw_pool_scale"),
        "w_out": exchanged("w_out", w_out, m_w_out, v_w_out),
        "norm_mix": update([small[2]], norm_mix, m_norm_mix, v_norm_mix, "adamw_norm_mix"),
        "norm_mlp": update([small[3]], norm_mlp, m_norm_mlp, v_norm_mlp, "adamw_norm_mlp"),
        "w_up": exchanged("w_up", w_up, m_w_up, v_w_up),
        "w_down": exchanged("w_down", w_down, m_w_down, v_w_down),
        "norm_final": update([small[4]], norm_final, m_norm_final, v_norm_final, "adamw_norm_final"),
    }
    results["conv_w"] = [o.reshape(conv_w.shape) for o in results["conv_w"]]
    order = ("w_in", "conv_w", "pool_w", "pool_scale", "w_out", "norm_mix", "norm_mlp", "w_up", "w_down", "norm_final")
    return (loss, grad_x, *[results[k][0] for k in order], *[results[k][1] for k in order],
            *[results[k][2] for k in order], *[results[k][3] for k in order])
```

```python
import functools

import jax
import jax.numpy as jnp
from jax import lax
from jax.experimental import pallas as pl
from jax.experimental.pallas import tpu as pltpu

F32 = jnp.float32
BF16 = jnp.bfloat16

N_DEV = 8
MESH_AXES = ("x", "y", "c")
NORM_EPS = 1e-6
POOL_WINDOWS = (2, 4, 8, 16)
HALO_ROWS = 16

ADAM_LR = 0.001
ADAM_B1 = 0.9
ADAM_B2 = 0.999
ADAM_EPS = 1e-08
ADAM_WD = 0.01
ADAM_STEP = 10

VMEM_BYTES_V7X = 64 * 1024 * 1024
VMEM_LIMIT = (VMEM_BYTES_V7X * 3) // 4

MM_TILE = 1024
MM_TILE_K = 2048
ROW_TILE = 512
MIXER_TILE = 512
ADAM_BLOCK_ELEMS = 64 * 1024
CAST_BLOCK_ELEMS = 512 * 1024


def _params(*semantics):
    return pltpu.CompilerParams(dimension_semantics=semantics, vmem_limit_bytes=VMEM_LIMIT)


def _tile(dim, pref):
    t = min(dim, pref)
    assert dim % t == 0, (dim, pref)
    return t


def _mesh_position():
    return lax.axis_index("x"), lax.axis_index("y"), lax.axis_index("c")


def _slot(p):
    return 4 * p[0] + 2 * p[1] + p[2]


def _all_gather(arrs, name):
    n = len(arrs)

    def body(*refs):
        ins, outs = refs[:n], refs[n:2 * n]
        send_sems, recv_sems, local_sems = refs[2 * n:]
        x, y, c = _mesh_position()
        me, sibling = (x, y, c), (x, y, 1 - c)
        chips = [(1 - x, y), (x, 1 - y), (1 - x, 1 - y)]

        def copy(a, k, block, to, src=None):
            dst = outs[a].at[_slot(block)]
            return pltpu.make_async_remote_copy(
                src_ref=dst if src is None else src, dst_ref=dst,
                send_sem=send_sems.at[a, k], recv_sem=recv_sems.at[a, k],
                device_id=to, device_id_type=pl.DeviceIdType.MESH)

        mine = [pltpu.make_async_copy(ins[a], outs[a].at[_slot(me)], local_sems.at[a]) for a in range(n)]
        for cp in mine:
            cp.start()
        first = []
        for a in range(n):
            first.append(copy(a, 0, me, sibling, src=ins[a]))
            first += [copy(a, 1 + j, me, (*chip, c), src=ins[a]) for j, chip in enumerate(chips)]
        for cp in first:
            cp.start()
        passed = []
        for j, chip in enumerate(chips):
            for a in range(n):
                copy(a, 1 + j, (*chip, c), me).wait_recv()
                cp = copy(a, 4 + j, (*chip, c), sibling)
                cp.start()
                passed.append(cp)
        for a in range(n):
            copy(a, 0, sibling, me).wait_recv()
            for j, chip in enumerate(chips):
                copy(a, 4 + j, (*chip, 1 - c), me).wait_recv()
        for cp in first + passed:
            cp.wait_send()
        for cp in mine:
            cp.wait()

    any_spec = pl.BlockSpec(memory_space=pl.ANY)
    return pl.pallas_call(
        body, name=name,
        out_shape=[jax.ShapeDtypeStruct((N_DEV, *a.shape), a.dtype) for a in arrs],
        in_specs=[any_spec] * n, out_specs=[any_spec] * n,
        scratch_shapes=[pltpu.SemaphoreType.DMA((n, 7)), pltpu.SemaphoreType.DMA((n, 7)),
                        pltpu.SemaphoreType.DMA((n,))],
    )(*arrs)


def _peer(k):
    x, y, c = _mesh_position()
    return (1 - x if k & 4 else x, 1 - y if k & 2 else y, 1 - c if k & 1 else c)


_HBM = pl.BlockSpec(memory_space=pltpu.HBM)
_SEM = pl.BlockSpec(memory_space=pltpu.SEMAPHORE)
_EFFECT = pltpu.SideEffectType.DATAFLOW_SIDE_EFFECTING


def _exchange_copy(g_ref, land_ref, send_sems, recv_sems, a, k):
    return pltpu.make_async_remote_copy(
        src_ref=g_ref.at[_slot(_peer(k))], dst_ref=land_ref.at[k - 1],
        send_sem=send_sems.at[a * (N_DEV - 1) + k - 1], recv_sem=recv_sems.at[a * (N_DEV - 1) + k - 1],
        device_id=_peer(k), device_id_type=pl.DeviceIdType.MESH)


def _exchange_start(arrs, name):
    n = len(arrs)

    def body(*refs):
        g_refs, land_refs = refs[:n], refs[n:2 * n]
        send_sems, recv_sems = refs[2 * n:2 * n + 2]
        token = refs[-1]
        for k in range(1, N_DEV):
            for a in range(n):
                _exchange_copy(g_refs[a], land_refs[a], send_sems, recv_sems, a, k).start()
        token[...] = jnp.zeros_like(token)

    lands = [lax.empty((N_DEV - 1, *g.shape[1:]), g.dtype) for g in arrs]
    outs = pl.pallas_call(
        body, name=name,
        out_shape=(pltpu.SemaphoreType.DMA((n * (N_DEV - 1),)), pltpu.SemaphoreType.DMA((n * (N_DEV - 1),)),
                   *[pltpu.HBM(g.shape, g.dtype) for g in arrs], *[pltpu.HBM(z.shape, z.dtype) for z in lands],
                   jax.ShapeDtypeStruct((8, 128), F32)),
        in_specs=[_HBM] * (2 * n),
        out_specs=(_SEM, _SEM, *[_HBM] * (2 * n), pl.BlockSpec(memory_space=pltpu.VMEM)),
        input_output_aliases={i: 2 + i for i in range(2 * n)},
        compiler_params=pltpu.CompilerParams(has_side_effects=_EFFECT),
    )(*[pltpu.with_memory_space_constraint(g, pltpu.HBM) for g in arrs],
      *[pltpu.with_memory_space_constraint(z, pltpu.HBM) for z in lands])
    return outs[0], outs[1], list(outs[2:2 + n]), list(outs[2 + n:2 + 2 * n]), outs[-1]


def _exchange_wait(send_sems, recv_sems, arrs, lands, after, name):
    n = len(arrs)

    def body(*refs):
        g_refs, land_refs = refs[:n], refs[n:2 * n]
        send_sems_ref, recv_sems_ref = refs[2 * n:2 * n + 2]
        for k in range(1, N_DEV):
            for a in range(n):
                cp = _exchange_copy(g_refs[a], land_refs[a], send_sems_ref, recv_sems_ref, a, k)
                cp.wait_send()
                cp.wait_recv()

    outs = pl.pallas_call(
        body, name=name,
        out_shape=(*[pltpu.HBM(g.shape, g.dtype) for g in arrs], *[pltpu.HBM(z.shape, z.dtype) for z in lands]),
        in_specs=[_HBM] * (2 * n) + [_SEM, _SEM, pl.BlockSpec(memory_space=pl.ANY)],
        out_specs=[_HBM] * (2 * n),
        input_output_aliases={i: i for i in range(2 * n)},
        compiler_params=pltpu.CompilerParams(has_side_effects=_EFFECT),
    )(*arrs, *lands, send_sems, recv_sems, after)
    return list(outs[:n]), list(outs[n:])


N_FIRST = 4
N_PASSED = 3


def _other_chips():
    x, y, _ = _mesh_position()
    return [(1 - x, y), (x, 1 - y), (1 - x, 1 - y)]


def _gather_copy(src_ref, land_ref, block, to, send_sem, recv_sem):
    rows = land_ref.at[_slot(block)]
    return pltpu.make_async_remote_copy(
        src_ref=rows if src_ref is None else src_ref, dst_ref=rows, send_sem=send_sem, recv_sem=recv_sem,
        device_id=to, device_id_type=pl.DeviceIdType.MESH)


def _gather_start(arrs, me_slot, after, name):
    n = len(arrs)

    def body(*refs):
        src_refs, land_refs = refs[:n], refs[n:2 * n]
        send_sems, recv_sems = refs[2 * n + 1:2 * n + 3]
        token = refs[-1]
        x, y, c = _mesh_position()
        targets = [(x, y, 1 - c)] + [(*chip, c) for chip in _other_chips()]
        for a in range(n):
            for k, to in enumerate(targets):
                _gather_copy(src_refs[a], land_refs[a], (x, y, c), to,
                             send_sems.at[a * N_FIRST + k], recv_sems.at[a * N_FIRST + k]).start()
        token[...] = jnp.zeros_like(token)

    lands = [lax.dynamic_update_slice_in_dim(lax.empty((N_DEV, *a.shape), a.dtype), a[None], me_slot, axis=0)
             for a in arrs]
    outs = pl.pallas_call(
        body, name=name,
        out_shape=(pltpu.SemaphoreType.DMA((n * N_FIRST,)), pltpu.SemaphoreType.DMA((n * N_FIRST,)),
                   *[pltpu.HBM(a.shape, a.dtype) for a in arrs], *[pltpu.HBM(z.shape, z.dtype) for z in lands],
                   jax.ShapeDtypeStruct((8, 128), F32)),
        in_specs=[_HBM] * (2 * n) + [pl.BlockSpec(memory_space=pl.ANY)],
        out_specs=(_SEM, _SEM, *[_HBM] * (2 * n), pl.BlockSpec(memory_space=pltpu.VMEM)),
        input_output_aliases={i: 2 + i for i in range(2 * n)},
        compiler_params=pltpu.CompilerParams(has_side_effects=_EFFECT),
    )(*[pltpu.with_memory_space_constraint(a, pltpu.HBM) for a in arrs],
      *[pltpu.with_memory_space_constraint(z, pltpu.HBM) for z in lands], after)
    return outs[0], outs[1], list(outs[2:2 + n]), list(outs[2 + n:2 + 2 * n]), outs[-1]


def _gather_pass_on(first_recv_sems, base, lands, after, name):
    n = len(lands)

    def body(*refs):
        land_refs = refs[:n]
        first_recv = refs[n]
        send_sems, recv_sems = refs[n + 2:n + 4]
        x, y, c = _mesh_position()
        sibling = (x, y, 1 - c)
        for j, chip in enumerate(_other_chips()):
            for a in range(n):
                _gather_copy(None, land_refs[a], (*chip, c), sibling,
                             send_sems.at[a * N_PASSED + j], first_recv.at[(base + a) * N_FIRST + 1 + j]).wait_recv()
                _gather_copy(None, land_refs[a], (*chip, c), sibling,
                             send_sems.at[a * N_PASSED + j], recv_sems.at[a * N_PASSED + j]).start()

    outs = pl.pallas_call(
        body, name=name,
        out_shape=(pltpu.SemaphoreType.DMA((n * N_PASSED,)), pltpu.SemaphoreType.DMA((n * N_PASSED,)),
                   *[pltpu.HBM(z.shape, z.dtype) for z in lands]),
        in_specs=[_HBM] * n + [_SEM, pl.BlockSpec(memory_space=pl.ANY)],
        out_specs=(_SEM, _SEM, *[_HBM] * n),
        input_output_aliases={i: 2 + i for i in range(n)},
        compiler_params=pltpu.CompilerParams(has_side_effects=_EFFECT),
    )(*lands, first_recv_sems, after)
    return outs[0], outs[1], list(outs[2:])


def _gather_finish(first_sems, base, passed_sems, arrs, lands, after, name):
    n = len(lands)

    def body(*refs):
        src_refs, land_refs = refs[:n], refs[n:2 * n]
        first_send, first_recv, passed_send, passed_recv = refs[2 * n:2 * n + 4]
        x, y, c = _mesh_position()
        sibling = (x, y, 1 - c)
        chips = _other_chips()
        for a in range(n):
            _gather_copy(src_refs[a], land_refs[a], sibling, sibling,
                         first_send.at[(base + a) * N_FIRST], first_recv.at[(base + a) * N_FIRST]).wait_recv()
            for j, chip in enumerate(chips):
                _gather_copy(None, land_refs[a], (*chip, 1 - c), sibling,
                             passed_send.at[a * N_PASSED + j], passed_recv.at[a * N_PASSED + j]).wait_recv()
        for a in range(n):
            for k in range(N_FIRST):
                _gather_copy(src_refs[a], land_refs[a], (x, y, c), sibling,
                             first_send.at[(base + a) * N_FIRST + k], first_recv.at[(base + a) * N_FIRST + k]).wait_send()
            for j, chip in enumerate(chips):
                _gather_copy(None, land_refs[a], (*chip, c), sibling,
                             passed_send.at[a * N_PASSED + j], passed_recv.at[a * N_PASSED + j]).wait_send()

    outs = pl.pallas_call(
        body, name=name,
        out_shape=(*[pltpu.HBM(a.shape, a.dtype) for a in arrs], *[pltpu.HBM(z.shape, z.dtype) for z in lands]),
        in_specs=[_HBM] * (2 * n) + [_SEM] * 4 + [pl.BlockSpec(memory_space=pl.ANY)],
        out_specs=[_HBM] * (2 * n),
        input_output_aliases={i: i for i in range(2 * n)},
        compiler_params=pltpu.CompilerParams(has_side_effects=_EFFECT),
    )(*arrs, *lands, *first_sems, *passed_sems, after)
    return list(outs[n:])


def _run_after(deps, body, in_specs, operands):
    n_deps = len(deps)
    if n_deps == 0:
        return body, in_specs, operands

    def body_behind(*refs):
        body(*refs[n_deps:])

    return body_behind, [pl.BlockSpec(memory_space=pl.ANY)] * n_deps + list(in_specs), list(deps) + list(operands)


def _zero_at_first(acc_ref, k, nk):
    if nk > 1:
        @pl.when(k == 0)
        def _():
            acc_ref[...] = jnp.zeros_like(acc_ref)


def _accumulate(acc_ref, product, k, nk, finish):
    if nk == 1:
        finish(product())
        return

    @pl.when(k < nk - 1)
    def _():
        acc_ref[...] += product()

    @pl.when(k == nk - 1)
    def _():
        finish(acc_ref[...] + product())


def _blocks_per_tile(n_blocks, width, pref):
    if width >= pref:
        return 1
    per = min(n_blocks, pref // width)
    assert n_blocks % per == 0
    return per


def _mm_nn(a, b3, *, out_dtype, name, res=None, with_relu2=False, deps=()):
    m, kdim = a.shape
    nb, kb_, nw = b3.shape
    assert kb_ == kdim
    per_tile = _blocks_per_tile(nb, nw, MM_TILE)
    tw = _tile(nw, MM_TILE)
    per_block = nw // tw
    tn = per_tile * tw
    tm, tk = _tile(m, MM_TILE), _tile(kdim, MM_TILE_K)
    nk = kdim // tk
    n_in = 2 + (res is not None)

    def body(*refs):
        a_ref, b_ref = refs[:2]
        r_ref = refs[2] if res is not None else None
        o_ref = refs[n_in]
        s_ref = refs[n_in + 1] if with_relu2 else None
        acc_ref = refs[-1] if nk > 1 else None
        k = pl.program_id(2)
        _zero_at_first(acc_ref, k, nk)
        for b in range(per_tile):
            cols = slice(b * tw, (b + 1) * tw)

            def product(b=b):
                return jnp.dot(a_ref[...], b_ref[b], preferred_element_type=F32)

            def finish(total, cols=cols):
                if r_ref is not None:
                    total = total + r_ref[:, cols]
                o_ref[:, cols] = total.astype(out_dtype)
                if s_ref is not None:
                    r = jnp.maximum(total, 0.0)
                    s_ref[:, cols] = (r * r).astype(BF16)

            _accumulate(acc_ref.at[:, cols] if nk > 1 else None, product, k, nk, finish)

    if per_tile > 1:
        b_spec = pl.BlockSpec((per_tile, tk, nw), lambda i, j, k: (j, k, 0))
    else:
        b_spec = pl.BlockSpec((1, tk, tw), lambda i, j, k: (j // per_block, k, j % per_block))
    in_specs = [pl.BlockSpec((tm, tk), lambda i, j, k: (i, k)), b_spec]
    operands = [a, b3]
    if res is not None:
        in_specs.append(pl.BlockSpec((tm, tn), lambda i, j, k: (i, j)))
        operands.append(res)
    body, in_specs, operands = _run_after(deps, body, in_specs, operands)
    o_spec = pl.BlockSpec((tm, tn), lambda i, j, k: (i, j))
    out_shape = [jax.ShapeDtypeStruct((m, nb * nw), out_dtype)]
    if with_relu2:
        out_shape.append(jax.ShapeDtypeStruct((m, nb * nw), BF16))
    outs = pl.pallas_call(
        body, name=name, grid=(m // tm, (nb * nw) // tn, nk),
        out_shape=out_shape, in_specs=in_specs, out_specs=[o_spec] * len(out_shape),
        scratch_shapes=[pltpu.VMEM((tm, tn), F32)] if nk > 1 else [],
        compiler_params=_params("parallel", "parallel", "arbitrary"),
    )(*operands)
    return outs if with_relu2 else outs[0]


def _mm_nt(a, b3, *, out_dtype, name, relu2_grad_of=None, deps=()):
    m, kdim = a.shape
    kb, n, kw = b3.shape
    assert kb * kw == kdim
    per_step = _blocks_per_tile(kb, kw, MM_TILE_K)
    tw = _tile(kw, MM_TILE_K)
    per_block = kw // tw
    tk = per_step * tw
    tm, tn = _tile(m, MM_TILE), _tile(n, MM_TILE)
    nk = kdim // tk

    def body(*refs):
        if relu2_grad_of is None:
            a_ref, b_ref, o_ref = refs[:3]
            g_ref = None
        else:
            a_ref, b_ref, g_ref, o_ref = refs[:4]
        acc_ref = refs[-1] if nk > 1 else None
        k = pl.program_id(2)
        _zero_at_first(acc_ref, k, nk)
        def product():
            p = None
            for b in range(per_step):
                part = lax.dot_general(a_ref[:, b * tw:(b + 1) * tw], b_ref[b], (((1,), (1,)), ((), ())),
                                       preferred_element_type=F32)
                p = part if p is None else p + part
            return p

        def finish(total):
            if g_ref is not None:
                total = total * (2.0 * jnp.maximum(g_ref[...].astype(F32), 0.0))
            o_ref[...] = total.astype(out_dtype)

        _accumulate(acc_ref, product, k, nk, finish)

    if per_step > 1:
        b_spec = pl.BlockSpec((per_step, tn, kw), lambda i, j, k: (k, j, 0))
    else:
        b_spec = pl.BlockSpec((1, tn, tw), lambda i, j, k: (k // per_block, j, k % per_block))
    in_specs = [pl.BlockSpec((tm, tk), lambda i, j, k: (i, k)), b_spec]
    operands = [a, b3]
    if relu2_grad_of is not None:
        in_specs.append(pl.BlockSpec((tm, tn), lambda i, j, k: (i, j)))
        operands.append(relu2_grad_of)
    body, in_specs, operands = _run_after(deps, body, in_specs, operands)
    return pl.pallas_call(
        body, name=name, grid=(m // tm, n // tn, nk),
        out_shape=jax.ShapeDtypeStruct((m, n), out_dtype),
        in_specs=in_specs, out_specs=pl.BlockSpec((tm, tn), lambda i, j, k: (i, j)),
        scratch_shapes=[pltpu.VMEM((tm, tn), F32)] if nk > 1 else [],
        compiler_params=_params("parallel", "parallel", "arbitrary"),
    )(*operands)


def _mm_tn(a, b, *, n_blocks, name, deps=()):
    t, m = a.shape
    t2, n = b.shape
    assert t == t2 and n % n_blocks == 0
    nw = n // n_blocks
    per_tile = _blocks_per_tile(n_blocks, nw, MM_TILE)
    tw = _tile(nw, MM_TILE)
    per_block = nw // tw
    tn = per_tile * tw
    tm, tk = _tile(m, MM_TILE), _tile(t, MM_TILE_K)
    nk = t // tk

    def body(a_ref, b_ref, o_ref, *scratch):
        acc_ref = scratch[0] if nk > 1 else None
        k = pl.program_id(2)
        _zero_at_first(acc_ref, k, nk)
        def product():
            return lax.dot_general(a_ref[...], b_ref[...], (((0,), (0,)), ((), ())), preferred_element_type=F32)

        def finish(total):
            for b in range(per_tile):
                o_ref[b] = total[:, b * tw:(b + 1) * tw].astype(BF16)

        _accumulate(acc_ref, product, k, nk, finish)

    if per_tile > 1:
        o_spec = pl.BlockSpec((per_tile, tm, nw), lambda i, j, k: (j, i, 0))
    else:
        o_spec = pl.BlockSpec((1, tm, tw), lambda i, j, k: (j // per_block, i, j % per_block))
    in_specs = [pl.BlockSpec((tk, tm), lambda i, j, k: (k, i)), pl.BlockSpec((tk, tn), lambda i, j, k: (k, j))]
    body, in_specs, operands = _run_after(deps, body, in_specs, [a, b])
    return pl.pallas_call(
        body, name=name, grid=(m // tm, n // tn, nk),
        out_shape=jax.ShapeDtypeStruct((n_blocks, m, nw), BF16),
        in_specs=in_specs, out_specs=o_spec,
        scratch_shapes=[pltpu.VMEM((tm, tn), F32)] if nk > 1 else [],
        compiler_params=_params("parallel", "parallel", "arbitrary"),
    )(*operands)


def _normalise(x):
    r = lax.rsqrt(jnp.mean(x * x, axis=-1, keepdims=True) + NORM_EPS)
    return x * r, r


def _rmsnorm_backward(dh, xhat, r, gain):
    dxhat = dh * gain
    return r * (dxhat - xhat * jnp.mean(dxhat * xhat, axis=-1, keepdims=True))


def _rmsnorm(x, gain, name, deps=()):
    t, d = x.shape
    tr = _tile(t, ROW_TILE)

    def body(x_ref, g_ref, o_ref, xb_ref):
        xv = x_ref[...]
        xhat, _ = _normalise(xv)
        o_ref[...] = (xhat * g_ref[...]).astype(BF16)
        xb_ref[...] = xv.astype(BF16)

    row = pl.BlockSpec((tr, d), lambda i: (i, 0))
    in_specs = [row, pl.BlockSpec((1, d), lambda i: (0, 0))]
    body, in_specs, operands = _run_after(deps, body, in_specs, [x, gain.reshape(1, d)])
    return pl.pallas_call(
        body, name=name, grid=(t // tr,),
        out_shape=[jax.ShapeDtypeStruct((t, d), BF16)] * 2,
        in_specs=in_specs, out_specs=[row, row],
        compiler_params=_params("parallel"),
    )(*operands)


def _rmsnorm_bwd(dh, x, gain, dres, out_dtype, name):
    t, d = x.shape
    tr = _tile(t, ROW_TILE)

    def body(dh_ref, x_ref, g_ref, r_ref, dx_ref, dg_ref):
        xhat, r = _normalise(x_ref[...].astype(F32))
        dh_v = dh_ref[...].astype(F32)
        dx = r_ref[...].astype(F32) + _rmsnorm_backward(dh_v, xhat, r, g_ref[...])
        dx_ref[...] = dx.astype(out_dtype)
        part = jnp.sum(dh_v * xhat, axis=0, keepdims=True)

        @pl.when(pl.program_id(0) == 0)
        def _():
            dg_ref[...] = part

        @pl.when(pl.program_id(0) > 0)
        def _():
            dg_ref[...] += part

    row = pl.BlockSpec((tr, d), lambda i: (i, 0))
    vec = pl.BlockSpec((1, d), lambda i: (0, 0))
    return pl.pallas_call(
        body, name=name, grid=(t // tr,),
        out_shape=[jax.ShapeDtypeStruct((t, d), out_dtype), jax.ShapeDtypeStruct((1, d), F32)],
        in_specs=[row, row, vec, row], out_specs=[row, vec],
        compiler_params=_params("arbitrary"),
    )(dh, x, gain.reshape(1, d), dres)


def _loss_head(x, gain, target, name):
    t, d = x.shape
    tr = _tile(t, ROW_TILE)
    steps = t // tr

    def body(x_ref, g_ref, t_ref, loss_ref, dxb_ref, dg_ref, sq_ref):
        i = pl.program_id(0)
        xhat, r = _normalise(x_ref[...])
        gain_v = g_ref[...]
        diff = xhat * gain_v - t_ref[...]
        dy = diff / float(d)
        dxb_ref[...] = _rmsnorm_backward(dy, xhat, r, gain_v).astype(BF16)
        dg_part = jnp.sum(dy * xhat, axis=0, keepdims=True)
        sq_part = jnp.sum(diff * diff, axis=0, keepdims=True)

        @pl.when(i == 0)
        def _():
            dg_ref[...] = dg_part
            sq_ref[...] = sq_part

        @pl.when(i > 0)
        def _():
            dg_ref[...] += dg_part
            sq_ref[...] += sq_part

        @pl.when(i == steps - 1)
        def _():
            loss_ref[...] = (0.5 / float(d)) * jnp.sum(sq_ref[...], axis=1, keepdims=True)

    row = pl.BlockSpec((tr, d), lambda i: (i, 0))
    vec = pl.BlockSpec((1, d), lambda i: (0, 0))
    return pl.pallas_call(
        body, name=name, grid=(steps,),
        out_shape=[jax.ShapeDtypeStruct((1, 1), F32), jax.ShapeDtypeStruct((t, d), BF16),
                   jax.ShapeDtypeStruct((1, d), F32)],
        in_specs=[row, vec, row],
        out_specs=[pl.BlockSpec((1, 1), lambda i: (0, 0)), row, vec],
        scratch_shapes=[pltpu.VMEM((1, d), F32)],
        compiler_params=_params("arbitrary"),
    )(x, gain.reshape(1, d), target)


def _shift_down(ext, s):
    return pltpu.roll(ext, s, 0)


def _shift_up(ext, s):
    return pltpu.roll(ext, ext.shape[0] - s, 0)


def _window_sum(ext, w, shift):
    s = 1
    while s < w:
        ext = ext + shift(ext, s)
        s *= 2
    return ext


def _window_count(tile_index, rows, cols, w):
    t = tile_index * rows + lax.broadcasted_iota(jnp.int32, (rows, cols), 0)
    return jnp.minimum(t + 1, w).astype(F32)


def _mixer_sizes(proj, conv_w):
    t, e = proj.shape
    dc = conv_w.shape[1]
    dp = e - 3 * dc
    cg = dp // len(POOL_WINDOWS)
    tt = _tile(t, MIXER_TILE)
    assert tt % HALO_ROWS == 0 and tt >= HALO_ROWS
    cw = _tile(dc, cg)
    return t, e, dc, dp, cg, tt, cw


def _mixer_fwd(proj, conv_w, pool_w, pool_scale, name, deps=()):
    t, e, dc, dp, cg, tt, cw = _mixer_sizes(proj, conv_w)
    per_halo = tt // HALO_ROWS

    def body(cur_ref, prev_ref, cw_ref, pw_ref, ps_ref, y_ref):
        i = pl.program_id(0)
        first = i == 0

        def cur(lo, width):
            return cur_ref[:, lo:lo + width].astype(F32)

        def prev(lo, width):
            return jnp.where(first, 0.0, prev_ref[:, lo:lo + width].astype(F32))

        for lo in range(0, dc, cw):
            u = cur(dc + lo, cw) * cur(2 * dc + lo, cw)
            ext = jnp.concatenate([prev(dc + lo, cw) * prev(2 * dc + lo, cw), u], axis=0)
            u1 = _shift_down(ext, 1)[HALO_ROWS:]
            u2 = _shift_down(ext, 2)[HALO_ROWS:]
            conv = cw_ref[0:1, lo:lo + cw] * u2 + cw_ref[1:2, lo:lo + cw] * u1 + cw_ref[2:3, lo:lo + cw] * u
            y_ref[:, lo:lo + cw] = (cur(lo, cw) * conv).astype(BF16)

        for g, w in enumerate(POOL_WINDOWS):
            lo = 3 * dc + g * cg
            v = cur(lo, cg)
            ext = jnp.concatenate([prev(lo, cg), v], axis=0)
            mean = _window_sum(ext, w, _shift_down)[HALO_ROWS:] / _window_count(i, tt, cg, w)
            z = jnp.dot((mean - v).astype(BF16), pw_ref[g], preferred_element_type=F32)
            y_ref[:, dc + g * cg:dc + (g + 1) * cg] = (z * ps_ref[0:1, g * cg:(g + 1) * cg]).astype(BF16)

    in_specs = [pl.BlockSpec((tt, e), lambda i: (i, 0)),
                pl.BlockSpec((HALO_ROWS, e), lambda i: (jnp.maximum(i * per_halo - 1, 0), 0)),
                pl.BlockSpec((3, dc), lambda i: (0, 0)),
                pl.BlockSpec((len(POOL_WINDOWS), cg, cg), lambda i: (0, 0, 0)),
                pl.BlockSpec((1, dp), lambda i: (0, 0))]
    body, in_specs, operands = _run_after(
        deps, body, in_specs, [proj, proj, conv_w, pool_w, pool_scale.reshape(1, dp)])
    return pl.pallas_call(
        body, name=name, grid=(t // tt,),
        out_shape=jax.ShapeDtypeStruct((t, dc + dp), BF16),
        in_specs=in_specs,
        out_specs=pl.BlockSpec((tt, dc + dp), lambda i: (i, 0)),
        compiler_params=_params("parallel"),
    )(*operands)


def _mixer_bwd(proj, dy, conv_w, pool_w, pool_scale, name, deps=()):
    t, e, dc, dp, cg, tt, cw = _mixer_sizes(proj, conv_w)
    per_halo = tt // HALO_ROWS
    steps = t // tt
    n_groups = len(POOL_WINDOWS)

    def body(cur_ref, prev_ref, next_ref, dy_ref, dyn_ref, cw_ref, pw_ref, ps_ref,
             dp_ref, dcw_ref, dpw_ref, dps_ref):
        i = pl.program_id(0)
        first = i == 0
        last = i == steps - 1

        @pl.when(first)
        def _():
            dcw_ref[...] = jnp.zeros_like(dcw_ref)
            dpw_ref[...] = jnp.zeros_like(dpw_ref)
            dps_ref[...] = jnp.zeros_like(dps_ref)

        def cur(lo, width):
            return cur_ref[:, lo:lo + width].astype(F32)

        def prev(lo, width):
            return jnp.where(first, 0.0, prev_ref[:, lo:lo + width].astype(F32))

        def nxt(ref, lo, width):
            return jnp.where(last, 0.0, ref[:, lo:lo + width].astype(F32))

        def colsum(v):
            return jnp.sum(v, axis=0, keepdims=True)

        for lo in range(0, dc, cw):
            cols = slice(lo, lo + cw)
            b, c, xt = cur(lo, cw), cur(dc + lo, cw), cur(2 * dc + lo, cw)
            u = c * xt
            ext = jnp.concatenate([prev(dc + lo, cw) * prev(2 * dc + lo, cw), u], axis=0)
            u1 = _shift_down(ext, 1)[HALO_ROWS:]
            u2 = _shift_down(ext, 2)[HALO_ROWS:]
            w0, w1, w2 = cw_ref[0:1, cols], cw_ref[1:2, cols], cw_ref[2:3, cols]
            dyc = dy_ref[:, cols].astype(F32)
            dp_ref[:, cols] = (dyc * (w0 * u2 + w1 * u1 + w2 * u)).astype(BF16)
            dconv = dyc * b
            dcw_ref[0:1, cols] += colsum(dconv * u2)
            dcw_ref[1:2, cols] += colsum(dconv * u1)
            dcw_ref[2:3, cols] += colsum(dconv * u)
            dext = jnp.concatenate([dconv, nxt(dyn_ref, lo, cw) * nxt(next_ref, lo, cw)], axis=0)
            du = w2 * dconv + w1 * _shift_up(dext, 1)[:tt] + w0 * _shift_up(dext, 2)[:tt]
            dp_ref[:, dc + lo:dc + lo + cw] = (du * xt).astype(BF16)
            dp_ref[:, 2 * dc + lo:2 * dc + lo + cw] = (du * c).astype(BF16)

        for g, w in enumerate(POOL_WINDOWS):
            lo = 3 * dc + g * cg
            ycols = slice(dc + g * cg, dc + (g + 1) * cg)
            pcols = slice(g * cg, (g + 1) * cg)
            v = cur(lo, cg)
            ext = jnp.concatenate([prev(lo, cg), v], axis=0)
            count = _window_count(i, tt, cg, w)
            d = ((_window_sum(ext, w, _shift_down)[HALO_ROWS:] / count) - v).astype(BF16)
            pw = pw_ref[g]
            scale = ps_ref[0:1, pcols]
            dyp = dy_ref[:, ycols].astype(F32)
            z = jnp.dot(d, pw, preferred_element_type=F32)
            dps_ref[0:1, pcols] += colsum(dyp * z)
            dz = (dyp * scale).astype(BF16)
            dpw_ref[g] += lax.dot_general(d, dz, (((0,), (0,)), ((), ())), preferred_element_type=F32)
            dd = lax.dot_general(dz, pw, (((1,), (1,)), ((), ())), preferred_element_type=F32)
            dzn = (nxt(dyn_ref, dc + g * cg, cg) * scale).astype(BF16)
            ddn = lax.dot_general(dzn, pw, (((1,), (1,)), ((), ())), preferred_element_type=F32)
            qext = jnp.concatenate([dd / count, ddn / float(w)], axis=0)
            dp_ref[:, lo:lo + cg] = (_window_sum(qext, w, _shift_up)[:tt] - dd).astype(BF16)

    cur_spec = lambda width: pl.BlockSpec((tt, width), lambda i: (i, 0))
    prev_spec = pl.BlockSpec((HALO_ROWS, e), lambda i: (jnp.maximum(i * per_halo - 1, 0), 0))
    next_spec = lambda width: pl.BlockSpec(
        (HALO_ROWS, width), lambda i: (jnp.minimum((i + 1) * per_halo, t // HALO_ROWS - 1), 0))
    in_specs = [cur_spec(e), prev_spec, next_spec(e), cur_spec(dc + dp), next_spec(dc + dp),
                pl.BlockSpec((3, dc), lambda i: (0, 0)),
                pl.BlockSpec((n_groups, cg, cg), lambda i: (0, 0, 0)),
                pl.BlockSpec((1, dp), lambda i: (0, 0))]
    body, in_specs, operands = _run_after(
        deps, body, in_specs, [proj, proj, proj, dy, dy, conv_w, pool_w, pool_scale.reshape(1, dp)])
    return pl.pallas_call(
        body, name=name, grid=(steps,),
        out_shape=[jax.ShapeDtypeStruct((t, e), BF16), jax.ShapeDtypeStruct((3, dc), F32),
                   jax.ShapeDtypeStruct((n_groups, cg, cg), F32), jax.ShapeDtypeStruct((1, dp), F32)],
        in_specs=in_specs,
        out_specs=[cur_spec(e), pl.BlockSpec((3, dc), lambda i: (0, 0)),
                   pl.BlockSpec((n_groups, cg, cg), lambda i: (0, 0, 0)),
                   pl.BlockSpec((1, dp), lambda i: (0, 0))],
        compiler_params=_params("arbitrary"),
    )(*operands)


def _adamw(partials, w, m, v, me_slot, name, mine=()):
    n_layers, r, c = w.shape
    assert len(partials) == n_layers and len(mine) in (0, n_layers)
    n_mine = len(mine)
    tr = r if r * c <= ADAM_BLOCK_ELEMS else _tile(r, max(16, ADAM_BLOCK_ELEMS // c))

    def body(me_ref, *refs):
        own_refs, p_refs = refs[:n_mine], refs[n_mine:n_mine + n_layers]
        w_ref, m_ref, v_ref, g_out, d_out, m_out, v_out = refs[n_mine + n_layers:]
        for l in range(n_layers):
            g = own_refs[l][0].astype(F32) if n_mine else None
            for s in range(p_refs[l].shape[0]):
                part = p_refs[l][s].astype(F32)
                g = part if g is None else g + part
            m_new = ADAM_B1 * m_ref[l] + (1.0 - ADAM_B1) * g
            v_new = ADAM_B2 * v_ref[l] + (1.0 - ADAM_B2) * (g * g)
            m_hat = m_new / (1.0 - ADAM_B1 ** ADAM_STEP)
            v_hat = v_new / (1.0 - ADAM_B2 ** ADAM_STEP)
            g_out[l] = g
            d_out[l] = -ADAM_LR * (m_hat / (jnp.sqrt(v_hat) + ADAM_EPS) + ADAM_WD * w_ref[l])
            m_out[l] = m_new
            v_out[l] = v_new

    own_spec = pl.BlockSpec((1, tr, c), lambda i, me_ref: (me_ref[0], i, 0))
    p_specs = [pl.BlockSpec((p.shape[0], tr, c), lambda i, me_ref: (0, i, 0)) for p in partials]
    w_spec = pl.BlockSpec((n_layers, tr, c), lambda i, me_ref: (0, i, 0))
    return pl.pallas_call(
        body, name=name,
        grid_spec=pltpu.PrefetchScalarGridSpec(
            num_scalar_prefetch=1, grid=(r // tr,),
            in_specs=[own_spec] * n_mine + p_specs + [w_spec] * 3, out_specs=[w_spec] * 4),
        out_shape=[jax.ShapeDtypeStruct(w.shape, F32)] * 4,
        compiler_params=_params("parallel"),
    )(jnp.reshape(me_slot, (1,)).astype(jnp.int32), *mine, *partials, w, m, v)


def _to_bf16(w, first, last, name, deps=()):
    _, r, c = w.shape
    tr = r if r * c <= CAST_BLOCK_ELEMS else _tile(r, max(16, CAST_BLOCK_ELEMS // c))

    def body(w_ref, o_ref):
        o_ref[...] = w_ref[...].astype(BF16)

    in_specs = [pl.BlockSpec((1, tr, c), lambda l, i: (first + l, i, 0))]
    body, in_specs, operands = _run_after(deps, body, in_specs, [w])
    return pl.pallas_call(
        body, name=name, grid=(last - first, r // tr),
        out_shape=jax.ShapeDtypeStruct((last - first, r, c), BF16),
        in_specs=in_specs, out_specs=pl.BlockSpec((1, tr, c), lambda l, i: (l, i, 0)),
        compiler_params=_params("parallel", "parallel"),
    )(*operands)


def kernel(x, w_in, conv_w, pool_w, pool_scale, w_out, norm_mix, norm_mlp, w_up, w_down, norm_final, loss_target, m_w_in, m_conv_w, m_pool_w, m_pool_scale, m_w_out, m_norm_mix, m_norm_mlp, m_w_up, m_w_down, m_norm_final, v_w_in, v_conv_w, v_pool_w, v_pool_scale, v_w_out, v_norm_mix, v_norm_mlp, v_w_up, v_w_down, v_norm_final):
    n_layers, d, e_shard = w_in.shape
    t = x.shape[1]
    n_groups, cg_shard, cg = pool_w.shape[1:]
    dc_shard = conv_w.shape[2]
    dc, dp = dc_shard * N_DEV, n_groups * cg
    f_shard = w_up.shape[2]
    xs = x.reshape(t, d)
    target = loss_target.reshape(t, d)

    me = _slot(_mesh_position())
    big = (w_in, w_out, w_up, w_down)
    n_kinds = len(big)
    first_layer = [_to_bf16(w, 0, 1, "cast_first_layer")[0] for w in big]
    mixer_blocks = [conv_w.reshape(n_layers * 3, dc_shard), pool_w.reshape(n_layers * n_groups * cg_shard, cg)]
    groups = {}

    def start(tag, blocks, after):
        send_sems, recv_sems, blocks, lands, token = _gather_start(blocks, me, after, f"weights_start_{tag}")
        groups[tag] = (send_sems, recv_sems, blocks, lands)
        return token

    token = start("first", [first_layer[0], *mixer_blocks, *first_layer[1:]], xs)
    if n_layers > 1:
        others = [_to_bf16(w, 1, n_layers, "cast_other_layers", deps=(token,)) for w in big]
        token = start("rest", [others[k][l - 1] for l in range(1, n_layers) for k in range(n_kinds)], token)
    mixer_place = ("first", 1)

    def place(l, k):
        return ("first", 0 if k == 0 else k + len(mixer_blocks)) if l == 0 else ("rest", n_kinds * (l - 1) + k)

    def passed_on(where, n, after, name):
        tag, i = where
        lands = groups[tag][3]
        send_sems, recv_sems, lands[i:i + n] = _gather_pass_on(groups[tag][1], i, lands[i:i + n], after, f"weights_pass_{name}")
        return send_sems, recv_sems

    def gathered(where, n, passed_sems, after, name):
        tag, i = where
        first_send, first_recv, blocks, lands = groups[tag]
        lands[i:i + n] = _gather_finish((first_send, first_recv), i, passed_sems, blocks[i:i + n], lands[i:i + n], after,
                                        f"weights_finish_{name}")
        return lands[i:i + n]

    def landing(where):
        return groups[where[0]][3][where[1]]

    saved, weights = [], []
    xc = xs
    after = token
    for l in range(n_layers):
        if l == 0:
            h1, x0b = _rmsnorm(xc, norm_mix[l], "norm_mix", deps=(after,))
            sems = passed_on(place(l, 0), 1, h1, f"{l}_0")
        else:
            sems = passed_on(place(l, 0), 1, after, f"{l}_0")
            h1, x0b = _rmsnorm(xc, norm_mix[l], "norm_mix", deps=(landing(place(l, 0)),))
        win, = gathered(place(l, 0), 1, sems, h1, f"{l}_0")
        proj = _mm_nn(h1, win, out_dtype=BF16, name="in_proj")
        if l == 0:
            mixer_sems = passed_on(mixer_place, len(mixer_blocks), proj, "mixer")
        sems = passed_on(place(l, 1), 1, proj, f"{l}_1")
        if l == 0:
            conv_g, pool_g = gathered(mixer_place, len(mixer_blocks), mixer_sems, proj, "mixer")
            conv_full = conv_g.reshape(N_DEV, n_layers, 3, dc_shard).transpose(1, 2, 0, 3).reshape(n_layers, 3, dc)
            pool_full = pool_g.reshape(N_DEV, n_layers, n_groups, cg_shard, cg).transpose(1, 2, 0, 3, 4)
            pool_full = pool_full.reshape(n_layers, n_groups, cg, cg).astype(BF16)
        y = _mixer_fwd(proj, conv_full[l], pool_full[l], pool_scale[l], "mixer_fwd", deps=(landing(place(l, 1)),))
        wout = gathered(place(l, 1), 1, sems, y, f"{l}_1")[0].reshape(1, d, d)
        x1 = _mm_nn(y, wout, out_dtype=F32, res=xc, name="out_proj")
        sems = passed_on(place(l, 2), 1, x1, f"{l}_2")
        h2, x1b = _rmsnorm(x1, norm_mlp[l], "norm_mlp", deps=(landing(place(l, 2)),))
        wup, = gathered(place(l, 2), 1, sems, h2, f"{l}_2")
        if l < 2:
            a, s = _mm_nn(h2, wup, out_dtype=BF16, with_relu2=True, name="mlp_up")
            sems = passed_on(place(l, 3), 1, a, f"{l}_3")
        else:
            sems = passed_on(place(l, 3), 1, h2, f"{l}_3")
            a, s = _mm_nn(h2, wup, out_dtype=BF16, with_relu2=True, name="mlp_up", deps=(landing(place(l, 3)),))
        wdown = gathered(place(l, 3), 1, sems, a, f"{l}_3")[0].reshape(1, f_shard * N_DEV, d)
        x2 = _mm_nn(s, wdown, out_dtype=F32, res=x1, name="mlp_down")
        weights.append((win, wout, wup, wdown))
        saved.append((x0b, h1, proj, y, x1b, h2, a, s))
        xc = after = x2

    loss_part, dxb, g_norm_final = _loss_head(xc, norm_final, target, "loss_head")
    loss = lax.psum(loss_part[0, 0], MESH_AXES)

    in_flight = {}
    received = {}

    def push(l, arrs, keys, tag):
        send_sems, recv_sems, thru, lands, token = _exchange_start(arrs, f"grads_start_{tag}_{l}")
        in_flight.setdefault(l, []).append((send_sems, recv_sems, thru, lands, keys, tag))
        return token

    def land(l, after):
        for send_sems, recv_sems, thru, lands, keys, tag in in_flight.pop(l):
            thru, lands = _exchange_wait(send_sems, recv_sems, thru, lands, after, f"grads_wait_{tag}_{l}")
            for key, own, got in zip(keys, thru, lands):
                received[key, l] = (own, got)

    g_conv, g_scale, g_mix, g_mlp = ([None] * n_layers for _ in range(4))
    for l in reversed(range(n_layers)):
        win, wout, wup, wdown = weights[l]
        x0b, h1, proj, y, x1b, h2, a, s = saved[l]
        gw_down = _mm_tn(s, dxb, n_blocks=1, name="mlp_down_dw")
        tok_down = push(l, [gw_down.reshape(N_DEV, f_shard, d)], ["w_down"], "down")
        da = _mm_nt(dxb, wdown, out_dtype=BF16, relu2_grad_of=a, name="mlp_down_dx", deps=(tok_down,))
        gw_up = _mm_tn(h2, da, n_blocks=N_DEV, name="mlp_up_dw")
        tok_up = push(l, [gw_up], ["w_up"], "up")
        dh2 = _mm_nt(da, wup, out_dtype=BF16, name="mlp_up_dx", deps=(tok_up,))
        dx1b, g_mlp[l] = _rmsnorm_bwd(dh2, x1b, norm_mlp[l], dxb, BF16, "norm_mlp_bwd")
        dy = _mm_nt(dx1b, wout, out_dtype=BF16, name="out_proj_dx")
        gw_out = _mm_tn(y, dx1b, n_blocks=1, name="out_proj_dw")
        tok_out = push(l, [gw_out.reshape(N_DEV, d // N_DEV, d)], ["w_out"], "out")
        dproj, g_conv[l], gw_pool, g_scale[l] = _mixer_bwd(
            proj, dy, conv_full[l], pool_full[l], pool_scale[l], "mixer_bwd", deps=(tok_out,))
        gw_in = _mm_tn(h1, dproj, n_blocks=N_DEV, name="in_proj_dw")
        gw_pool = gw_pool.reshape(n_groups, N_DEV, cg_shard, cg).transpose(1, 0, 2, 3)
        tok_in = push(l, [gw_in, gw_pool.reshape(N_DEV, n_groups * cg_shard, cg).astype(BF16)], ["w_in", "pool_w"], "in")
        dh1 = _mm_nt(dproj, win, out_dtype=BF16, name="in_proj_dx", deps=(tok_in,))
        dxb, g_mix[l] = _rmsnorm_bwd(dh1, x0b, norm_mix[l], dx1b, F32 if l == 0 else BF16, "norm_mix_bwd")
        if l + 1 < n_layers:
            land(l + 1, dxb)
    land(0, dxb)
    grad_x = dxb.reshape(x.shape)

    small = _all_gather(
        [jnp.stack(g_conv).reshape(n_layers * 3, dc), jnp.concatenate(g_scale, axis=0),
         jnp.concatenate(g_mix, axis=0), jnp.concatenate(g_mlp, axis=0), g_norm_final], "gather_small_grads")
    conv_parts = lax.dynamic_slice_in_dim(small[0], me * dc_shard, dc_shard, axis=2)

    def update(partials, w, m, v, name, mine=()):
        shape = w.shape
        rc = (shape[0], -1, shape[-1]) if w.ndim > 2 else (1, *shape) if w.ndim == 2 else (1, 1, *shape)
        outs = _adamw(partials, w.reshape(rc), m.reshape(rc), v.reshape(rc), me, name, mine=mine)
        return [o.reshape(shape) for o in outs]

    def exchanged(key, w, m, v):
        return update([received[key, l][1] for l in range(n_layers)], w, m, v, f"adamw_{key}",
                      mine=[received[key, l][0] for l in range(n_layers)])

    results = {
        "w_in": exchanged("w_in", w_in, m_w_in, v_w_in),
        "conv_w": update([conv_parts], conv_w.reshape(n_layers * 3, dc_shard), m_conv_w.reshape(n_layers * 3, dc_shard),
                         v_conv_w.reshape(n_layers * 3, dc_shard), "adamw_conv_w"),
        "pool_w": exchanged("pool_w", pool_w, m_pool_w, v_pool_w),
        "pool_scale": update([small[1]], pool_scale, m_pool_scale, v_pool_scale, "adamw_pool_scale"),
        "w_out": exchanged("w_out", w_out, m_w_out, v_w_out),
        "norm_mix": update([small[2]], norm_mix, m_norm_mix, v_norm_mix, "adamw_norm_mix"),
        "norm_mlp": update([small[3]], norm_mlp, m_norm_mlp, v_norm_mlp, "adamw_norm_mlp"),
        "w_up": exchanged("w_up", w_up, m_w_up, v_w_up),
        "w_down": exchanged("w_down", w_down, m_w_down, v_w_down),
        "norm_final": update([small[4]], norm_final, m_norm_final, v_norm_final, "adamw_norm_final"),
    }
    results["conv_w"] = [o.reshape(conv_w.shape) for o in results["conv_w"]]
    order = ("w_in", "conv_w", "pool_w", "pool_scale", "w_out", "norm_mix", "norm_mlp", "w_up", "w_down", "norm_final")
    return (loss, grad_x, *[results[k][0] for k in order], *[results[k][1] for k in order],
            *[results[k][2] for k in order], *[results[k][3] for k in order])
```

```python
import jax
import jax.numpy as jnp
from jax import lax
from jax.experimental import pallas as pl
from jax.experimental.pallas import tpu as pltpu

F32 = jnp.float32
BF16 = jnp.bfloat16

N_DEV = 8
MESH_AXES = ("x", "y", "c")
NORM_EPS = 1e-6
POOL_WINDOWS = (2, 4, 8, 16)
HALO_ROWS = 16

ADAM_LR = 0.001
ADAM_B1 = 0.9
ADAM_B2 = 0.999
ADAM_EPS = 1e-08
ADAM_WD = 0.01
ADAM_STEP = 10

VMEM_BYTES_V7X = 64 * 1024 * 1024
VMEM_LIMIT = (VMEM_BYTES_V7X * 3) // 4

MM_TILE = 1024
MM_TILE_K = 2048
ROW_TILE = 512
MIXER_TILE = 512
STREAM_BUFFERS = 3
ADAM_BLOCK_ELEMS = 64 * 1024
CAST_BLOCK_ELEMS = 512 * 1024


def _params(*semantics):
    return pltpu.CompilerParams(dimension_semantics=semantics, vmem_limit_bytes=VMEM_LIMIT)


def _tile(dim, pref):
    t = min(dim, pref)
    assert dim % t == 0, (dim, pref)
    return t


def _stream_rows(hbm_refs, bufs, sems, tr, steps):
    i = pl.program_id(0)

    def fetch(q, step):
        start = step * tr if isinstance(step, int) else pl.multiple_of(step * tr, tr)
        slot = step % STREAM_BUFFERS
        return pltpu.make_async_copy(hbm_refs[q].at[pl.ds(start, tr), :], bufs[q].at[slot], sems.at[q, slot])

    @pl.when(i == 0)
    def _():
        for step in range(min(STREAM_BUFFERS - 1, steps)):
            for q in range(len(hbm_refs)):
                fetch(q, step).start()

    ahead = i + (STREAM_BUFFERS - 1)

    @pl.when(ahead < steps)
    def _():
        for q in range(len(hbm_refs)):
            fetch(q, ahead).start()

    for q in range(len(hbm_refs)):
        fetch(q, i).wait()
    return [bufs[q][i % STREAM_BUFFERS] for q in range(len(hbm_refs))]


def _mesh_position():
    return lax.axis_index("x"), lax.axis_index("y"), lax.axis_index("c")


def _slot(p):
    return 4 * p[0] + 2 * p[1] + p[2]


def _all_gather(arrs, name):
    n = len(arrs)

    def body(*refs):
        ins, outs = refs[:n], refs[n:2 * n]
        send_sems, recv_sems, local_sems = refs[2 * n:]
        x, y, c = _mesh_position()
        me, sibling = (x, y, c), (x, y, 1 - c)
        chips = [(1 - x, y), (x, 1 - y), (1 - x, 1 - y)]

        def copy(a, k, block, to, src=None):
            dst = outs[a].at[_slot(block)]
            return pltpu.make_async_remote_copy(
                src_ref=dst if src is None else src, dst_ref=dst,
                send_sem=send_sems.at[a, k], recv_sem=recv_sems.at[a, k],
                device_id=to, device_id_type=pl.DeviceIdType.MESH)

        mine = [pltpu.make_async_copy(ins[a], outs[a].at[_slot(me)], local_sems.at[a]) for a in range(n)]
        for cp in mine:
            cp.start()
        first = []
        for a in range(n):
            first.append(copy(a, 0, me, sibling, src=ins[a]))
            first += [copy(a, 1 + j, me, (*chip, c), src=ins[a]) for j, chip in enumerate(chips)]
        for cp in first:
            cp.start()
        passed = []
        for j, chip in enumerate(chips):
            for a in range(n):
                copy(a, 1 + j, (*chip, c), me).wait_recv()
                cp = copy(a, 4 + j, (*chip, c), sibling)
                cp.start()
                passed.append(cp)
        for a in range(n):
            copy(a, 0, sibling, me).wait_recv()
            for j, chip in enumerate(chips):
                copy(a, 4 + j, (*chip, 1 - c), me).wait_recv()
        for cp in first + passed:
            cp.wait_send()
        for cp in mine:
            cp.wait()

    any_spec = pl.BlockSpec(memory_space=pl.ANY)
    return pl.pallas_call(
        body, name=name,
        out_shape=[jax.ShapeDtypeStruct((N_DEV, *a.shape), a.dtype) for a in arrs],
        in_specs=[any_spec] * n, out_specs=[any_spec] * n,
        scratch_shapes=[pltpu.SemaphoreType.DMA((n, 7)), pltpu.SemaphoreType.DMA((n, 7)),
                        pltpu.SemaphoreType.DMA((n,))],
    )(*arrs)


def _peer(k):
    x, y, c = _mesh_position()
    return (1 - x if k & 4 else x, 1 - y if k & 2 else y, 1 - c if k & 1 else c)


_HBM = pl.BlockSpec(memory_space=pltpu.HBM)
_SEM = pl.BlockSpec(memory_space=pltpu.SEMAPHORE)
_EFFECT = pltpu.SideEffectType.DATAFLOW_SIDE_EFFECTING


def _exchange_copy(g_ref, land_ref, send_sems, recv_sems, a, k):
    return pltpu.make_async_remote_copy(
        src_ref=g_ref.at[_slot(_peer(k))], dst_ref=land_ref.at[k - 1],
        send_sem=send_sems.at[a * (N_DEV - 1) + k - 1], recv_sem=recv_sems.at[a * (N_DEV - 1) + k - 1],
        device_id=_peer(k), device_id_type=pl.DeviceIdType.MESH)


def _exchange_start(arrs, name):
    n = len(arrs)

    def body(*refs):
        g_refs, land_refs = refs[:n], refs[n:2 * n]
        send_sems, recv_sems = refs[2 * n:2 * n + 2]
        token = refs[-1]
        for k in range(1, N_DEV):
            for a in range(n):
                _exchange_copy(g_refs[a], land_refs[a], send_sems, recv_sems, a, k).start()
        token[...] = jnp.zeros_like(token)

    lands = [lax.empty((N_DEV - 1, *g.shape[1:]), g.dtype) for g in arrs]
    outs = pl.pallas_call(
        body, name=name,
        out_shape=(pltpu.SemaphoreType.DMA((n * (N_DEV - 1),)), pltpu.SemaphoreType.DMA((n * (N_DEV - 1),)),
                   *[pltpu.HBM(g.shape, g.dtype) for g in arrs], *[pltpu.HBM(z.shape, z.dtype) for z in lands],
                   jax.ShapeDtypeStruct((8, 128), F32)),
        in_specs=[_HBM] * (2 * n),
        out_specs=(_SEM, _SEM, *[_HBM] * (2 * n), pl.BlockSpec(memory_space=pltpu.VMEM)),
        input_output_aliases={i: 2 + i for i in range(2 * n)},
        compiler_params=pltpu.CompilerParams(has_side_effects=_EFFECT),
    )(*[pltpu.with_memory_space_constraint(g, pltpu.HBM) for g in arrs],
      *[pltpu.with_memory_space_constraint(z, pltpu.HBM) for z in lands])
    return outs[0], outs[1], list(outs[2:2 + n]), list(outs[2 + n:2 + 2 * n]), outs[-1]


def _exchange_wait(send_sems, recv_sems, arrs, lands, after, name):
    n = len(arrs)

    def body(*refs):
        g_refs, land_refs = refs[:n], refs[n:2 * n]
        send_sems_ref, recv_sems_ref = refs[2 * n:2 * n + 2]
        for k in range(1, N_DEV):
            for a in range(n):
                cp = _exchange_copy(g_refs[a], land_refs[a], send_sems_ref, recv_sems_ref, a, k)
                cp.wait_send()
                cp.wait_recv()

    outs = pl.pallas_call(
        body, name=name,
        out_shape=(*[pltpu.HBM(g.shape, g.dtype) for g in arrs], *[pltpu.HBM(z.shape, z.dtype) for z in lands]),
        in_specs=[_HBM] * (2 * n) + [_SEM, _SEM, pl.BlockSpec(memory_space=pl.ANY)],
        out_specs=[_HBM] * (2 * n),
        input_output_aliases={i: i for i in range(2 * n)},
        compiler_params=pltpu.CompilerParams(has_side_effects=_EFFECT),
    )(*arrs, *lands, send_sems, recv_sems, after)
    return list(outs[:n]), list(outs[n:])


N_FIRST = 4
N_PASSED = 3


def _other_chips():
    x, y, _ = _mesh_position()
    return [(1 - x, y), (x, 1 - y), (1 - x, 1 - y)]


def _gather_copy(src_ref, land_ref, block, to, send_sem, recv_sem):
    rows = land_ref.at[_slot(block)]
    return pltpu.make_async_remote_copy(
        src_ref=rows if src_ref is None else src_ref, dst_ref=rows, send_sem=send_sem, recv_sem=recv_sem,
        device_id=to, device_id_type=pl.DeviceIdType.MESH)


def _gather_start(arrs, me_slot, after, name):
    n = len(arrs)

    def body(*refs):
        src_refs, land_refs = refs[:n], refs[n:2 * n]
        send_sems, recv_sems = refs[2 * n + 1:2 * n + 3]
        token = refs[-1]
        x, y, c = _mesh_position()
        targets = [(x, y, 1 - c)] + [(*chip, c) for chip in _other_chips()]
        for a in range(n):
            for k, to in enumerate(targets):
                _gather_copy(src_refs[a], land_refs[a], (x, y, c), to,
                             send_sems.at[a * N_FIRST + k], recv_sems.at[a * N_FIRST + k]).start()
        token[...] = jnp.zeros_like(token)

    lands = [lax.dynamic_update_slice_in_dim(lax.empty((N_DEV, *a.shape), a.dtype), a[None], me_slot, axis=0)
             for a in arrs]
    outs = pl.pallas_call(
        body, name=name,
        out_shape=(pltpu.SemaphoreType.DMA((n * N_FIRST,)), pltpu.SemaphoreType.DMA((n * N_FIRST,)),
                   *[pltpu.HBM(a.shape, a.dtype) for a in arrs], *[pltpu.HBM(z.shape, z.dtype) for z in lands],
                   jax.ShapeDtypeStruct((8, 128), F32)),
        in_specs=[_HBM] * (2 * n) + [pl.BlockSpec(memory_space=pl.ANY)],
        out_specs=(_SEM, _SEM, *[_HBM] * (2 * n), pl.BlockSpec(memory_space=pltpu.VMEM)),
        input_output_aliases={i: 2 + i for i in range(2 * n)},
        compiler_params=pltpu.CompilerParams(has_side_effects=_EFFECT),
    )(*[pltpu.with_memory_space_constraint(a, pltpu.HBM) for a in arrs],
      *[pltpu.with_memory_space_constraint(z, pltpu.HBM) for z in lands], after)
    return outs[0], outs[1], list(outs[2:2 + n]), list(outs[2 + n:2 + 2 * n]), outs[-1]


def _gather_pass_on(first_recv_sems, base, lands, after, name):
    n = len(lands)

    def body(*refs):
        land_refs = refs[:n]
        first_recv = refs[n]
        send_sems, recv_sems = refs[n + 2:n + 4]
        x, y, c = _mesh_position()
        sibling = (x, y, 1 - c)
        for j, chip in enumerate(_other_chips()):
            for a in range(n):
                _gather_copy(None, land_refs[a], (*chip, c), sibling,
                             send_sems.at[a * N_PASSED + j], first_recv.at[(base + a) * N_FIRST + 1 + j]).wait_recv()
                _gather_copy(None, land_refs[a], (*chip, c), sibling,
                             send_sems.at[a * N_PASSED + j], recv_sems.at[a * N_PASSED + j]).start()

    outs = pl.pallas_call(
        body, name=name,
        out_shape=(pltpu.SemaphoreType.DMA((n * N_PASSED,)), pltpu.SemaphoreType.DMA((n * N_PASSED,)),
                   *[pltpu.HBM(z.shape, z.dtype) for z in lands]),
        in_specs=[_HBM] * n + [_SEM, pl.BlockSpec(memory_space=pl.ANY)],
        out_specs=(_SEM, _SEM, *[_HBM] * n),
        input_output_aliases={i: 2 + i for i in range(n)},
        compiler_params=pltpu.CompilerParams(has_side_effects=_EFFECT),
    )(*lands, first_recv_sems, after)
    return outs[0], outs[1], list(outs[2:])


def _gather_finish(first_sems, base, passed_sems, arrs, lands, after, name):
    n = len(lands)

    def body(*refs):
        src_refs, land_refs = refs[:n], refs[n:2 * n]
        first_send, first_recv, passed_send, passed_recv = refs[2 * n:2 * n + 4]
        x, y, c = _mesh_position()
        sibling = (x, y, 1 - c)
        chips = _other_chips()
        for a in range(n):
            _gather_copy(src_refs[a], land_refs[a], sibling, sibling,
                         first_send.at[(base + a) * N_FIRST], first_recv.at[(base + a) * N_FIRST]).wait_recv()
            for j, chip in enumerate(chips):
                _gather_copy(None, land_refs[a], (*chip, 1 - c), sibling,
                             passed_send.at[a * N_PASSED + j], passed_recv.at[a * N_PASSED + j]).wait_recv()
        for a in range(n):
            for k in range(N_FIRST):
                _gather_copy(src_refs[a], land_refs[a], (x, y, c), sibling,
                             first_send.at[(base + a) * N_FIRST + k], first_recv.at[(base + a) * N_FIRST + k]).wait_send()
            for j, chip in enumerate(chips):
                _gather_copy(None, land_refs[a], (*chip, c), sibling,
                             passed_send.at[a * N_PASSED + j], passed_recv.at[a * N_PASSED + j]).wait_send()

    outs = pl.pallas_call(
        body, name=name,
        out_shape=(*[pltpu.HBM(a.shape, a.dtype) for a in arrs], *[pltpu.HBM(z.shape, z.dtype) for z in lands]),
        in_specs=[_HBM] * (2 * n) + [_SEM] * 4 + [pl.BlockSpec(memory_space=pl.ANY)],
        out_specs=[_HBM] * (2 * n),
        input_output_aliases={i: i for i in range(2 * n)},
        compiler_params=pltpu.CompilerParams(has_side_effects=_EFFECT),
    )(*arrs, *lands, *first_sems, *passed_sems, after)
    return list(outs[n:])


def _run_after(deps, body, in_specs, operands):
    n_deps = len(deps)
    if n_deps == 0:
        return body, in_specs, operands

    def body_behind(*refs):
        body(*refs[n_deps:])

    return body_behind, [pl.BlockSpec(memory_space=pl.ANY)] * n_deps + list(in_specs), list(deps) + list(operands)


def _zero_at_first(acc_ref, k, nk):
    if nk > 1:
        @pl.when(k == 0)
        def _():
            acc_ref[...] = jnp.zeros_like(acc_ref)


def _accumulate(acc_ref, product, k, nk, finish):
    if nk == 1:
        finish(product())
        return

    @pl.when(k < nk - 1)
    def _():
        acc_ref[...] += product()

    @pl.when(k == nk - 1)
    def _():
        finish(acc_ref[...] + product())


def _blocks_per_tile(n_blocks, width, pref):
    if width >= pref:
        return 1
    per = min(n_blocks, pref // width)
    assert n_blocks % per == 0
    return per


def _mm_nn(a, b3, *, out_dtype, name, res=None, with_relu2=False, deps=()):
    m, kdim = a.shape
    nb, kb_, nw = b3.shape
    assert kb_ == kdim
    per_tile = _blocks_per_tile(nb, nw, MM_TILE)
    tw = _tile(nw, MM_TILE)
    per_block = nw // tw
    tn = per_tile * tw
    tm, tk = _tile(m, MM_TILE), _tile(kdim, MM_TILE_K)
    nk = kdim // tk
    n_in = 2 + (res is not None)

    def body(*refs):
        a_ref, b_ref = refs[:2]
        r_ref = refs[2] if res is not None else None
        o_ref = refs[n_in]
        s_ref = refs[n_in + 1] if with_relu2 else None
        acc_ref = refs[-1] if nk > 1 else None
        k = pl.program_id(2)
        _zero_at_first(acc_ref, k, nk)
        for b in range(per_tile):
            cols = slice(b * tw, (b + 1) * tw)

            def product(b=b):
                return jnp.dot(a_ref[...], b_ref[b], preferred_element_type=F32)

            def finish(total, cols=cols):
                if r_ref is not None:
                    total = total + r_ref[:, cols]
                o_ref[:, cols] = total.astype(out_dtype)
                if s_ref is not None:
                    r = jnp.maximum(total, 0.0)
                    s_ref[:, cols] = (r * r).astype(BF16)

            _accumulate(acc_ref.at[:, cols] if nk > 1 else None, product, k, nk, finish)

    if per_tile > 1:
        b_spec = pl.BlockSpec((per_tile, tk, nw), lambda i, j, k: (j, k, 0))
    else:
        b_spec = pl.BlockSpec((1, tk, tw), lambda i, j, k: (j // per_block, k, j % per_block))
    in_specs = [pl.BlockSpec((tm, tk), lambda i, j, k: (i, k)), b_spec]
    operands = [a, b3]
    if res is not None:
        in_specs.append(pl.BlockSpec((tm, tn), lambda i, j, k: (i, j)))
        operands.append(res)
    body, in_specs, operands = _run_after(deps, body, in_specs, operands)
    o_spec = pl.BlockSpec((tm, tn), lambda i, j, k: (i, j))
    out_shape = [jax.ShapeDtypeStruct((m, nb * nw), out_dtype)]
    if with_relu2:
        out_shape.append(jax.ShapeDtypeStruct((m, nb * nw), BF16))
    outs = pl.pallas_call(
        body, name=name, grid=(m // tm, (nb * nw) // tn, nk),
        out_shape=out_shape, in_specs=in_specs, out_specs=[o_spec] * len(out_shape),
        scratch_shapes=[pltpu.VMEM((tm, tn), F32)] if nk > 1 else [],
        compiler_params=_params("parallel", "parallel", "arbitrary"),
    )(*operands)
    return outs if with_relu2 else outs[0]


def _mm_nt(a, b3, *, out_dtype, name, relu2_grad_of=None, deps=()):
    m, kdim = a.shape
    kb, n, kw = b3.shape
    assert kb * kw == kdim
    per_step = _blocks_per_tile(kb, kw, MM_TILE_K)
    tw = _tile(kw, MM_TILE_K)
    per_block = kw // tw
    tk = per_step * tw
    tm, tn = _tile(m, MM_TILE), _tile(n, MM_TILE)
    nk = kdim // tk

    def body(*refs):
        if relu2_grad_of is None:
            a_ref, b_ref, o_ref = refs[:3]
            g_ref = None
        else:
            a_ref, b_ref, g_ref, o_ref = refs[:4]
        acc_ref = refs[-1] if nk > 1 else None
        k = pl.program_id(2)
        _zero_at_first(acc_ref, k, nk)
        def product():
            p = None
            for b in range(per_step):
                part = lax.dot_general(a_ref[:, b * tw:(b + 1) * tw], b_ref[b], (((1,), (1,)), ((), ())),
                                       preferred_element_type=F32)
                p = part if p is None else p + part
            return p

        def finish(total):
            if g_ref is not None:
                total = total * (2.0 * jnp.maximum(g_ref[...].astype(F32), 0.0))
            o_ref[...] = total.astype(out_dtype)

        _accumulate(acc_ref, product, k, nk, finish)

    if per_step > 1:
        b_spec = pl.BlockSpec((per_step, tn, kw), lambda i, j, k: (k, j, 0))
    else:
        b_spec = pl.BlockSpec((1, tn, tw), lambda i, j, k: (k // per_block, j, k % per_block))
    in_specs = [pl.BlockSpec((tm, tk), lambda i, j, k: (i, k)), b_spec]
    operands = [a, b3]
    if relu2_grad_of is not None:
        in_specs.append(pl.BlockSpec((tm, tn), lambda i, j, k: (i, j)))
        operands.append(relu2_grad_of)
    body, in_specs, operands = _run_after(deps, body, in_specs, operands)
    return pl.pallas_call(
        body, name=name, grid=(m // tm, n // tn, nk),
        out_shape=jax.ShapeDtypeStruct((m, n), out_dtype),
        in_specs=in_specs, out_specs=pl.BlockSpec((tm, tn), lambda i, j, k: (i, j)),
        scratch_shapes=[pltpu.VMEM((tm, tn), F32)] if nk > 1 else [],
        compiler_params=_params("parallel", "parallel", "arbitrary"),
    )(*operands)


def _mm_tn(a, b, *, n_blocks, name, deps=()):
    t, m = a.shape
    t2, n = b.shape
    assert t == t2 and n % n_blocks == 0
    nw = n // n_blocks
    per_tile = _blocks_per_tile(n_blocks, nw, MM_TILE)
    tw = _tile(nw, MM_TILE)
    per_block = nw // tw
    tn = per_tile * tw
    tm, tk = _tile(m, MM_TILE), _tile(t, MM_TILE_K)
    nk = t // tk

    def body(a_ref, b_ref, o_ref, *scratch):
        acc_ref = scratch[0] if nk > 1 else None
        k = pl.program_id(2)
        _zero_at_first(acc_ref, k, nk)
        def product():
            return lax.dot_general(a_ref[...], b_ref[...], (((0,), (0,)), ((), ())), preferred_element_type=F32)

        def finish(total):
            for b in range(per_tile):
                o_ref[b] = total[:, b * tw:(b + 1) * tw].astype(BF16)

        _accumulate(acc_ref, product, k, nk, finish)

    if per_tile > 1:
        o_spec = pl.BlockSpec((per_tile, tm, nw), lambda i, j, k: (j, i, 0))
    else:
        o_spec = pl.BlockSpec((1, tm, tw), lambda i, j, k: (j // per_block, i, j % per_block))
    in_specs = [pl.BlockSpec((tk, tm), lambda i, j, k: (k, i)), pl.BlockSpec((tk, tn), lambda i, j, k: (k, j))]
    body, in_specs, operands = _run_after(deps, body, in_specs, [a, b])
    return pl.pallas_call(
        body, name=name, grid=(m // tm, n // tn, nk),
        out_shape=jax.ShapeDtypeStruct((n_blocks, m, nw), BF16),
        in_specs=in_specs, out_specs=o_spec,
        scratch_shapes=[pltpu.VMEM((tm, tn), F32)] if nk > 1 else [],
        compiler_params=_params("parallel", "parallel", "arbitrary"),
    )(*operands)


def _normalise(x):
    r = lax.rsqrt(jnp.mean(x * x, axis=-1, keepdims=True) + NORM_EPS)
    return x * r, r


def _rmsnorm_backward(dh, xhat, r, gain):
    dxhat = dh * gain
    return r * (dxhat - xhat * jnp.mean(dxhat * xhat, axis=-1, keepdims=True))


def _rmsnorm(x, gain, name, deps=()):
    t, d = x.shape
    tr = _tile(t, ROW_TILE)

    steps = t // tr

    def body(x_hbm, g_ref, o_ref, xb_ref, x_buf, sems):
        xv, = _stream_rows([x_hbm], [x_buf], sems, tr, steps)
        xhat, _ = _normalise(xv)
        o_ref[...] = (xhat * g_ref[...]).astype(BF16)
        xb_ref[...] = xv.astype(BF16)

    row = pl.BlockSpec((tr, d), lambda i: (i, 0))
    in_specs = [pl.BlockSpec(memory_space=pl.ANY), pl.BlockSpec((1, d), lambda i: (0, 0))]
    body, in_specs, operands = _run_after(deps, body, in_specs, [x, gain.reshape(1, d)])
    return pl.pallas_call(
        body, name=name, grid=(steps,),
        out_shape=[jax.ShapeDtypeStruct((t, d), BF16)] * 2,
        in_specs=in_specs, out_specs=[row, row],
        scratch_shapes=[pltpu.VMEM((STREAM_BUFFERS, tr, d), x.dtype), pltpu.SemaphoreType.DMA((1, STREAM_BUFFERS))],
        compiler_params=_params("arbitrary"),
    )(*operands)


def _rmsnorm_bwd(dh, x, gain, dres, out_dtype, name):
    t, d = x.shape
    tr = _tile(t, ROW_TILE)
    steps = t // tr
    streams = (dh, x, dres)

    def body(dh_hbm, x_hbm, r_hbm, g_ref, dx_ref, dg_ref, dh_buf, x_buf, r_buf, sems):
        dh_b, x_b, r_b = _stream_rows([dh_hbm, x_hbm, r_hbm], [dh_buf, x_buf, r_buf], sems, tr, steps)
        xhat, r = _normalise(x_b.astype(F32))
        dh_v = dh_b.astype(F32)
        dx = r_b.astype(F32) + _rmsnorm_backward(dh_v, xhat, r, g_ref[...])
        dx_ref[...] = dx.astype(out_dtype)
        part = jnp.sum(dh_v * xhat, axis=0, keepdims=True)

        @pl.when(pl.program_id(0) == 0)
        def _():
            dg_ref[...] = part

        @pl.when(pl.program_id(0) > 0)
        def _():
            dg_ref[...] += part

    row = pl.BlockSpec((tr, d), lambda i: (i, 0))
    vec = pl.BlockSpec((1, d), lambda i: (0, 0))
    return pl.pallas_call(
        body, name=name, grid=(steps,),
        out_shape=[jax.ShapeDtypeStruct((t, d), out_dtype), jax.ShapeDtypeStruct((1, d), F32)],
        in_specs=[pl.BlockSpec(memory_space=pl.ANY)] * len(streams) + [vec], out_specs=[row, vec],
        scratch_shapes=[pltpu.VMEM((STREAM_BUFFERS, tr, d), s.dtype) for s in streams]
        + [pltpu.SemaphoreType.DMA((len(streams), STREAM_BUFFERS))],
        compiler_params=_params("arbitrary"),
    )(*streams, gain.reshape(1, d))


def _loss_head(x, gain, target, name):
    t, d = x.shape
    tr = _tile(t, ROW_TILE)
    steps = t // tr

    def body(x_ref, g_ref, t_ref, loss_ref, dxb_ref, dg_ref, sq_ref):
        i = pl.program_id(0)
        xhat, r = _normalise(x_ref[...])
        gain_v = g_ref[...]
        diff = xhat * gain_v - t_ref[...]
        dy = diff / float(d)
        dxb_ref[...] = _rmsnorm_backward(dy, xhat, r, gain_v).astype(BF16)
        dg_part = jnp.sum(dy * xhat, axis=0, keepdims=True)
        sq_part = jnp.sum(diff * diff, axis=0, keepdims=True)

        @pl.when(i == 0)
        def _():
            dg_ref[...] = dg_part
            sq_ref[...] = sq_part

        @pl.when(i > 0)
        def _():
            dg_ref[...] += dg_part
            sq_ref[...] += sq_part

        @pl.when(i == steps - 1)
        def _():
            loss_ref[...] = (0.5 / float(d)) * jnp.sum(sq_ref[...], axis=1, keepdims=True)

    row = pl.BlockSpec((tr, d), lambda i: (i, 0))
    vec = pl.BlockSpec((1, d), lambda i: (0, 0))
    return pl.pallas_call(
        body, name=name, grid=(steps,),
        out_shape=[jax.ShapeDtypeStruct((1, 1), F32), jax.ShapeDtypeStruct((t, d), BF16),
                   jax.ShapeDtypeStruct((1, d), F32)],
        in_specs=[row, vec, row],
        out_specs=[pl.BlockSpec((1, 1), lambda i: (0, 0)), row, vec],
        scratch_shapes=[pltpu.VMEM((1, d), F32)],
        compiler_params=_params("arbitrary"),
    )(x, gain.reshape(1, d), target)


def _shift_down(ext, s):
    return pltpu.roll(ext, s, 0)


def _shift_up(ext, s):
    return pltpu.roll(ext, ext.shape[0] - s, 0)


def _window_sum(ext, w, shift):
    s = 1
    while s < w:
        ext = ext + shift(ext, s)
        s *= 2
    return ext


def _window_count(tile_index, rows, cols, w):
    t = tile_index * rows + lax.broadcasted_iota(jnp.int32, (rows, cols), 0)
    return jnp.minimum(t + 1, w).astype(F32)


def _mixer_sizes(proj, conv_w):
    t, e = proj.shape
    dc = conv_w.shape[1]
    dp = e - 3 * dc
    cg = dp // len(POOL_WINDOWS)
    tt = _tile(t, MIXER_TILE)
    assert tt % HALO_ROWS == 0 and tt >= HALO_ROWS
    cw = _tile(dc, cg)
    return t, e, dc, dp, cg, tt, cw


def _mixer_fwd(proj, conv_w, pool_w, pool_scale, name, deps=()):
    t, e, dc, dp, cg, tt, cw = _mixer_sizes(proj, conv_w)
    per_halo = tt // HALO_ROWS

    def body(cur_ref, prev_ref, cw_ref, pw_ref, ps_ref, y_ref):
        i = pl.program_id(0)
        first = i == 0

        def cur(lo, width):
            return cur_ref[:, lo:lo + width].astype(F32)

        def prev(lo, width):
            return jnp.where(first, 0.0, prev_ref[:, lo:lo + width].astype(F32))

        for lo in range(0, dc, cw):
            u = cur(dc + lo, cw) * cur(2 * dc + lo, cw)
            ext = jnp.concatenate([prev(dc + lo, cw) * prev(2 * dc + lo, cw), u], axis=0)
            u1 = _shift_down(ext, 1)[HALO_ROWS:]
            u2 = _shift_down(ext, 2)[HALO_ROWS:]
            conv = cw_ref[0:1, lo:lo + cw] * u2 + cw_ref[1:2, lo:lo + cw] * u1 + cw_ref[2:3, lo:lo + cw] * u
            y_ref[:, lo:lo + cw] = (cur(lo, cw) * conv).astype(BF16)

        for g, w in enumerate(POOL_WINDOWS):
            lo = 3 * dc + g * cg
            v = cur(lo, cg)
            ext = jnp.concatenate([prev(lo, cg), v], axis=0)
            mean = _window_sum(ext, w, _shift_down)[HALO_ROWS:] / _window_count(i, tt, cg, w)
            z = jnp.dot((mean - v).astype(BF16), pw_ref[g], preferred_element_type=F32)
            y_ref[:, dc + g * cg:dc + (g + 1) * cg] = (z * ps_ref[0:1, g * cg:(g + 1) * cg]).astype(BF16)

    in_specs = [pl.BlockSpec((tt, e), lambda i: (i, 0)),
                pl.BlockSpec((HALO_ROWS, e), lambda i: (jnp.maximum(i * per_halo - 1, 0), 0)),
                pl.BlockSpec((3, dc), lambda i: (0, 0)),
                pl.BlockSpec((len(POOL_WINDOWS), cg, cg), lambda i: (0, 0, 0)),
                pl.BlockSpec((1, dp), lambda i: (0, 0))]
    body, in_specs, operands = _run_after(
        deps, body, in_specs, [proj, proj, conv_w, pool_w, pool_scale.reshape(1, dp)])
    return pl.pallas_call(
        body, name=name, grid=(t // tt,),
        out_shape=jax.ShapeDtypeStruct((t, dc + dp), BF16),
        in_specs=in_specs,
        out_specs=pl.BlockSpec((tt, dc + dp), lambda i: (i, 0)),
        compiler_params=_params("parallel"),
    )(*operands)


def _mixer_bwd(proj, dy, conv_w, pool_w, pool_scale, name, deps=()):
    t, e, dc, dp, cg, tt, cw = _mixer_sizes(proj, conv_w)
    per_halo = tt // HALO_ROWS
    steps = t // tt
    n_groups = len(POOL_WINDOWS)

    def body(cur_ref, prev_ref, next_ref, dy_ref, dyn_ref, cw_ref, pw_ref, ps_ref,
             dp_ref, dcw_ref, dpw_ref, dps_ref):
        i = pl.program_id(0)
        first = i == 0
        last = i == steps - 1

        @pl.when(first)
        def _():
            dcw_ref[...] = jnp.zeros_like(dcw_ref)
            dpw_ref[...] = jnp.zeros_like(dpw_ref)
            dps_ref[...] = jnp.zeros_like(dps_ref)

        def cur(lo, width):
            return cur_ref[:, lo:lo + width].astype(F32)

        def prev(lo, width):
            return jnp.where(first, 0.0, prev_ref[:, lo:lo + width].astype(F32))

        def nxt(ref, lo, width):
            return jnp.where(last, 0.0, ref[:, lo:lo + width].astype(F32))

        def colsum(v):
            return jnp.sum(v, axis=0, keepdims=True)

        for lo in range(0, dc, cw):
            cols = slice(lo, lo + cw)
            b, c, xt = cur(lo, cw), cur(dc + lo, cw), cur(2 * dc + lo, cw)
            u = c * xt
            ext = jnp.concatenate([prev(dc + lo, cw) * prev(2 * dc + lo, cw), u], axis=0)
            u1 = _shift_down(ext, 1)[HALO_ROWS:]
            u2 = _shift_down(ext, 2)[HALO_ROWS:]
            w0, w1, w2 = cw_ref[0:1, cols], cw_ref[1:2, cols], cw_ref[2:3, cols]
            dyc = dy_ref[:, cols].astype(F32)
            dp_ref[:, cols] = (dyc * (w0 * u2 + w1 * u1 + w2 * u)).astype(BF16)
            dconv = dyc * b
            dcw_ref[0:1, cols] += colsum(dconv * u2)
            dcw_ref[1:2, cols] += colsum(dconv * u1)
            dcw_ref[2:3, cols] += colsum(dconv * u)
            dext = jnp.concatenate([dconv, nxt(dyn_ref, lo, cw) * nxt(next_ref, lo, cw)], axis=0)
            du = w2 * dconv + w1 * _shift_up(dext, 1)[:tt] + w0 * _shift_up(dext, 2)[:tt]
            dp_ref[:, dc + lo:dc + lo + cw] = (du * xt).astype(BF16)
            dp_ref[:, 2 * dc + lo:2 * dc + lo + cw] = (du * c).astype(BF16)

        for g, w in enumerate(POOL_WINDOWS):
            lo = 3 * dc + g * cg
            ycols = slice(dc + g * cg, dc + (g + 1) * cg)
            pcols = slice(g * cg, (g + 1) * cg)
            v = cur(lo, cg)
            ext = jnp.concatenate([prev(lo, cg), v], axis=0)
            count = _window_count(i, tt, cg, w)
            d = ((_window_sum(ext, w, _shift_down)[HALO_ROWS:] / count) - v).astype(BF16)
            pw = pw_ref[g]
            scale = ps_ref[0:1, pcols]
            dyp = dy_ref[:, ycols].astype(F32)
            z = jnp.dot(d, pw, preferred_element_type=F32)
            dps_ref[0:1, pcols] += colsum(dyp * z)
            dz = (dyp * scale).astype(BF16)
            dpw_ref[g] += lax.dot_general(d, dz, (((0,), (0,)), ((), ())), preferred_element_type=F32)
            dd = lax.dot_general(dz, pw, (((1,), (1,)), ((), ())), preferred_element_type=F32)
            dzn = (nxt(dyn_ref, dc + g * cg, cg) * scale).astype(BF16)
            ddn = lax.dot_general(dzn, pw, (((1,), (1,)), ((), ())), preferred_element_type=F32)
            qext = jnp.concatenate([dd / count, ddn / float(w)], axis=0)
            dp_ref[:, lo:lo + cg] = (_window_sum(qext, w, _shift_up)[:tt] - dd).astype(BF16)

    cur_spec = lambda width: pl.BlockSpec((tt, width), lambda i: (i, 0))
    prev_spec = pl.BlockSpec((HALO_ROWS, e), lambda i: (jnp.maximum(i * per_halo - 1, 0), 0))
    next_spec = lambda width: pl.BlockSpec(
        (HALO_ROWS, width), lambda i: (jnp.minimum((i + 1) * per_halo, t // HALO_ROWS - 1), 0))
    in_specs = [cur_spec(e), prev_spec, next_spec(e), cur_spec(dc + dp), next_spec(dc + dp),
                pl.BlockSpec((3, dc), lambda i: (0, 0)),
                pl.BlockSpec((n_groups, cg, cg), lambda i: (0, 0, 0)),
                pl.BlockSpec((1, dp), lambda i: (0, 0))]
    body, in_specs, operands = _run_after(
        deps, body, in_specs, [proj, proj, proj, dy, dy, conv_w, pool_w, pool_scale.reshape(1, dp)])
    return pl.pallas_call(
        body, name=name, grid=(steps,),
        out_shape=[jax.ShapeDtypeStruct((t, e), BF16), jax.ShapeDtypeStruct((3, dc), F32),
                   jax.ShapeDtypeStruct((n_groups, cg, cg), F32), jax.ShapeDtypeStruct((1, dp), F32)],
        in_specs=in_specs,
        out_specs=[cur_spec(e), pl.BlockSpec((3, dc), lambda i: (0, 0)),
                   pl.BlockSpec((n_groups, cg, cg), lambda i: (0, 0, 0)),
                   pl.BlockSpec((1, dp), lambda i: (0, 0))],
        compiler_params=_params("arbitrary"),
    )(*operands)


def _adamw(partials, w, m, v, me_slot, name, mine=()):
    n_layers, r, c = w.shape
    assert len(partials) == n_layers and len(mine) in (0, n_layers)
    n_mine = len(mine)
    tr = r if r * c <= ADAM_BLOCK_ELEMS else _tile(r, max(16, ADAM_BLOCK_ELEMS // c))

    def body(me_ref, *refs):
        own_refs, p_refs = refs[:n_mine], refs[n_mine:n_mine + n_layers]
        w_ref, m_ref, v_ref, g_out, d_out, m_out, v_out = refs[n_mine + n_layers:]
        for l in range(n_layers):
            g = own_refs[l][0].astype(F32) if n_mine else None
            for s in range(p_refs[l].shape[0]):
                part = p_refs[l][s].astype(F32)
                g = part if g is None else g + part
            m_new = ADAM_B1 * m_ref[l] + (1.0 - ADAM_B1) * g
            v_new = ADAM_B2 * v_ref[l] + (1.0 - ADAM_B2) * (g * g)
            m_hat = m_new / (1.0 - ADAM_B1 ** ADAM_STEP)
            v_hat = v_new / (1.0 - ADAM_B2 ** ADAM_STEP)
            g_out[l] = g
            d_out[l] = -ADAM_LR * (m_hat / (jnp.sqrt(v_hat) + ADAM_EPS) + ADAM_WD * w_ref[l])
            m_out[l] = m_new
            v_out[l] = v_new

    own_spec = pl.BlockSpec((1, tr, c), lambda i, me_ref: (me_ref[0], i, 0))
    p_specs = [pl.BlockSpec((p.shape[0], tr, c), lambda i, me_ref: (0, i, 0)) for p in partials]
    w_spec = pl.BlockSpec((n_layers, tr, c), lambda i, me_ref: (0, i, 0))
    return pl.pallas_call(
        body, name=name,
        grid_spec=pltpu.PrefetchScalarGridSpec(
            num_scalar_prefetch=1, grid=(r // tr,),
            in_specs=[own_spec] * n_mine + p_specs + [w_spec] * 3, out_specs=[w_spec] * 4),
        out_shape=[jax.ShapeDtypeStruct(w.shape, F32)] * 4,
        compiler_params=_params("parallel"),
    )(jnp.reshape(me_slot, (1,)).astype(jnp.int32), *mine, *partials, w, m, v)


def _to_bf16(w, first, last, name, deps=()):
    _, r, c = w.shape
    tr = r if r * c <= CAST_BLOCK_ELEMS else _tile(r, max(16, CAST_BLOCK_ELEMS // c))

    def body(w_ref, o_ref):
        o_ref[...] = w_ref[...].astype(BF16)

    in_specs = [pl.BlockSpec((1, tr, c), lambda l, i: (first + l, i, 0))]
    body, in_specs, operands = _run_after(deps, body, in_specs, [w])
    return pl.pallas_call(
        body, name=name, grid=(last - first, r // tr),
        out_shape=jax.ShapeDtypeStruct((last - first, r, c), BF16),
        in_specs=in_specs, out_specs=pl.BlockSpec((1, tr, c), lambda l, i: (l, i, 0)),
        compiler_params=_params("parallel", "parallel"),
    )(*operands)


def kernel(x, w_in, conv_w, pool_w, pool_scale, w_out, norm_mix, norm_mlp, w_up, w_down, norm_final, loss_target, m_w_in, m_conv_w, m_pool_w, m_pool_scale, m_w_out, m_norm_mix, m_norm_mlp, m_w_up, m_w_down, m_norm_final, v_w_in, v_conv_w, v_pool_w, v_pool_scale, v_w_out, v_norm_mix, v_norm_mlp, v_w_up, v_w_down, v_norm_final):
    n_layers, d, e_shard = w_in.shape
    t = x.shape[1]
    n_groups, cg_shard, cg = pool_w.shape[1:]
    dc_shard = conv_w.shape[2]
    dc, dp = dc_shard * N_DEV, n_groups * cg
    f_shard = w_up.shape[2]
    xs = x.reshape(t, d)
    target = loss_target.reshape(t, d)

    me = _slot(_mesh_position())
    big = (w_in, w_out, w_up, w_down)
    n_kinds = len(big)
    first_layer = [_to_bf16(w, 0, 1, "cast_first_layer")[0] for w in big]
    mixer_blocks = [conv_w.reshape(n_layers * 3, dc_shard), pool_w.reshape(n_layers * n_groups * cg_shard, cg)]
    groups = {}

    def start(tag, blocks, after):
        send_sems, recv_sems, blocks, lands, token = _gather_start(blocks, me, after, f"weights_start_{tag}")
        groups[tag] = (send_sems, recv_sems, blocks, lands)
        return token

    token = start("first", [first_layer[0], *mixer_blocks, *first_layer[1:]], xs)
    if n_layers > 1:
        others = [_to_bf16(w, 1, n_layers, "cast_other_layers", deps=(token,)) for w in big]
        token = start("rest", [others[k][l - 1] for l in range(1, n_layers) for k in range(n_kinds)], token)
    mixer_place = ("first", 1)

    def place(l, k):
        return ("first", 0 if k == 0 else k + len(mixer_blocks)) if l == 0 else ("rest", n_kinds * (l - 1) + k)

    def passed_on(where, n, after, name):
        tag, i = where
        lands = groups[tag][3]
        send_sems, recv_sems, lands[i:i + n] = _gather_pass_on(groups[tag][1], i, lands[i:i + n], after, f"weights_pass_{name}")
        return send_sems, recv_sems

    def gathered(where, n, passed_sems, after, name):
        tag, i = where
        first_send, first_recv, blocks, lands = groups[tag]
        lands[i:i + n] = _gather_finish((first_send, first_recv), i, passed_sems, blocks[i:i + n], lands[i:i + n], after,
                                        f"weights_finish_{name}")
        return lands[i:i + n]

    def landing(where):
        return groups[where[0]][3][where[1]]

    saved, weights = [], []
    xc = xs
    after = token
    for l in range(n_layers):
        if l == 0:
            h1, x0b = _rmsnorm(xc, norm_mix[l], "norm_mix", deps=(after,))
            sems = passed_on(place(l, 0), 1, h1, f"{l}_0")
        else:
            sems = passed_on(place(l, 0), 1, after, f"{l}_0")
            h1, x0b = _rmsnorm(xc, norm_mix[l], "norm_mix", deps=(landing(place(l, 0)),))
        win, = gathered(place(l, 0), 1, sems, h1, f"{l}_0")
        proj = _mm_nn(h1, win, out_dtype=BF16, name="in_proj")
        if l == 0:
            mixer_sems = passed_on(mixer_place, len(mixer_blocks), proj, "mixer")
        sems = passed_on(place(l, 1), 1, proj, f"{l}_1")
        if l == 0:
            conv_g, pool_g = gathered(mixer_place, len(mixer_blocks), mixer_sems, proj, "mixer")
            conv_full = conv_g.reshape(N_DEV, n_layers, 3, dc_shard).transpose(1, 2, 0, 3).reshape(n_layers, 3, dc)
            pool_full = pool_g.reshape(N_DEV, n_layers, n_groups, cg_shard, cg).transpose(1, 2, 0, 3, 4)
            pool_full = pool_full.reshape(n_layers, n_groups, cg, cg).astype(BF16)
        y = _mixer_fwd(proj, conv_full[l], pool_full[l], pool_scale[l], "mixer_fwd", deps=(landing(place(l, 1)),))
        wout = gathered(place(l, 1), 1, sems, y, f"{l}_1")[0].reshape(1, d, d)
        x1 = _mm_nn(y, wout, out_dtype=F32, res=xc, name="out_proj")
        sems = passed_on(place(l, 2), 1, x1, f"{l}_2")
        h2, x1b = _rmsnorm(x1, norm_mlp[l], "norm_mlp", deps=(landing(place(l, 2)),))
        wup, = gathered(place(l, 2), 1, sems, h2, f"{l}_2")
        if l < 2:
            a, s = _mm_nn(h2, wup, out_dtype=BF16, with_relu2=True, name="mlp_up")
            sems = passed_on(place(l, 3), 1, a, f"{l}_3")
        else:
            sems = passed_on(place(l, 3), 1, h2, f"{l}_3")
            a, s = _mm_nn(h2, wup, out_dtype=BF16, with_relu2=True, name="mlp_up", deps=(landing(place(l, 3)),))
        wdown = gathered(place(l, 3), 1, sems, a, f"{l}_3")[0].reshape(1, f_shard * N_DEV, d)
        x2 = _mm_nn(s, wdown, out_dtype=F32, res=x1, name="mlp_down")
        weights.append((win, wout, wup, wdown))
        saved.append((x0b, h1, proj, y, x1b, h2, a, s))
        xc = after = x2

    loss_part, dxb, g_norm_final = _loss_head(xc, norm_final, target, "loss_head")
    loss = lax.psum(loss_part[0, 0], MESH_AXES)

    in_flight = {}
    received = {}

    def push(l, arrs, keys, tag):
        send_sems, recv_sems, thru, lands, token = _exchange_start(arrs, f"grads_start_{tag}_{l}")
        in_flight.setdefault(l, []).append((send_sems, recv_sems, thru, lands, keys, tag))
        return token

    def land(l, after):
        for send_sems, recv_sems, thru, lands, keys, tag in in_flight.pop(l):
            thru, lands = _exchange_wait(send_sems, recv_sems, thru, lands, after, f"grads_wait_{tag}_{l}")
            for key, own, got in zip(keys, thru, lands):
                received[key, l] = (own, got)

    g_conv, g_scale, g_mix, g_mlp = ([None] * n_layers for _ in range(4))
    for l in reversed(range(n_layers)):
        win, wout, wup, wdown = weights[l]
        x0b, h1, proj, y, x1b, h2, a, s = saved[l]
        gw_down = _mm_tn(s, dxb, n_blocks=1, name="mlp_down_dw")
        tok_down = push(l, [gw_down.reshape(N_DEV, f_shard, d)], ["w_down"], "down")
        da = _mm_nt(dxb, wdown, out_dtype=BF16, relu2_grad_of=a, name="mlp_down_dx", deps=(tok_down,))
        gw_up = _mm_tn(h2, da, n_blocks=N_DEV, name="mlp_up_dw")
        tok_up = push(l, [gw_up], ["w_up"], "up")
        dh2 = _mm_nt(da, wup, out_dtype=BF16, name="mlp_up_dx", deps=(tok_up,))
        dx1b, g_mlp[l] = _rmsnorm_bwd(dh2, x1b, norm_mlp[l], dxb, BF16, "norm_mlp_bwd")
        dy = _mm_nt(dx1b, wout, out_dtype=BF16, name="out_proj_dx")
        gw_out = _mm_tn(y, dx1b, n_blocks=1, name="out_proj_dw")
        tok_out = push(l, [gw_out.reshape(N_DEV, d // N_DEV, d)], ["w_out"], "out")
        dproj, g_conv[l], gw_pool, g_scale[l] = _mixer_bwd(
            proj, dy, conv_full[l], pool_full[l], pool_scale[l], "mixer_bwd", deps=(tok_out,))
        gw_in = _mm_tn(h1, dproj, n_blocks=N_DEV, name="in_proj_dw")
        gw_pool = gw_pool.reshape(n_groups, N_DEV, cg_shard, cg).transpose(1, 0, 2, 3)
        tok_in = push(l, [gw_in, gw_pool.reshape(N_DEV, n_groups * cg_shard, cg).astype(BF16)], ["w_in", "pool_w"], "in")
        dh1 = _mm_nt(dproj, win, out_dtype=BF16, name="in_proj_dx", deps=(tok_in,))
        dxb, g_mix[l] = _rmsnorm_bwd(dh1, x0b, norm_mix[l], dx1b, F32 if l == 0 else BF16, "norm_mix_bwd")
        if l + 1 < n_layers:
            land(l + 1, dxb)
    land(0, dxb)
    grad_x = dxb.reshape(x.shape)

    small = _all_gather(
        [jnp.stack(g_conv).reshape(n_layers * 3, dc), jnp.concatenate(g_scale, axis=0),
         jnp.concatenate(g_mix, axis=0), jnp.concatenate(g_mlp, axis=0), g_norm_final], "gather_small_grads")
    conv_parts = lax.dynamic_slice_in_dim(small[0], me * dc_shard, dc_shard, axis=2)

    def update(partials, w, m, v, name, mine=()):
        shape = w.shape
        rc = (shape[0], -1, shape[-1]) if w.ndim > 2 else (1, *shape) if w.ndim == 2 else (1, 1, *shape)
        outs = _adamw(partials, w.reshape(rc), m.reshape(rc), v.reshape(rc), me, name, mine=mine)
        return [o.reshape(shape) for o in outs]

    def exchanged(key, w, m, v):
        return update([received[key, l][1] for l in range(n_layers)], w, m, v, f"adamw_{key}",
                      mine=[received[key, l][0] for l in range(n_layers)])

    results = {
        "w_in": exchanged("w_in", w_in, m_w_in, v_w_in),
        "conv_w": update([conv_parts], conv_w.reshape(n_layers * 3, dc_shard), m_conv_w.reshape(n_layers * 3, dc_shard),
                         v_conv_w.reshape(n_layers * 3, dc_shard), "adamw_conv_w"),
        "pool_w": exchanged("pool_w", pool_w, m_pool_w, v_pool_w),
        "pool_scale": update([small[1]], pool_scale, m_pool_scale, v_pool_scale, "adamw_pool_scale"),
        "w_out": exchanged("w_out", w_out, m_w_out, v_w_out),
        "norm_mix": update([small[2]], norm_mix, m_norm_mix, v_norm_mix, "adamw_norm_mix"),
        "norm_mlp": update([small[3]], norm_mlp, m_norm_mlp, v_norm_mlp, "adamw_norm_mlp"),
        "w_up": exchanged("w_up", w_up, m_w_up, v_w_up),
        "w_down": exchanged("w_down", w_down, m_w_down, v_w_down),
        "norm_final": update([small[4]], norm_final, m_norm_final, v_norm_final, "adamw_norm_final"),
    }
    results["conv_w"] = [o.reshape(conv_w.shape) for o in results["conv_w"]]
    order = ("w_in", "conv_w", "pool_w", "pool_scale", "w_out", "norm_mix", "norm_mlp", "w_up", "w_down", "norm_final")
    return (loss, grad_x, *[results[k][0] for k in order], *[results[k][1] for k in order],
            *[results[k][2] for k in order], *[results[k][3] for k in order])
```

```python
import jax
import jax.numpy as jnp
from jax import lax
from jax.experimental import pallas as pl
from jax.experimental.pallas import tpu as pltpu

F32 = jnp.float32
BF16 = jnp.bfloat16

N_DEV = 8
MESH_AXES = ("x", "y", "c")
NORM_EPS = 1e-6
POOL_WINDOWS = (2, 4, 8, 16)
HALO_ROWS = 16

ADAM_LR = 0.001
ADAM_B1 = 0.9
ADAM_B2 = 0.999
ADAM_EPS = 1e-08
ADAM_WD = 0.01
ADAM_STEP = 10

VMEM_BYTES_V7X = 64 * 1024 * 1024
VMEM_LIMIT = (VMEM_BYTES_V7X * 3) // 4

MM_TILE = 1024
MM_TILE_K = 2048
ROW_TILE = 512
MIXER_TILE = 512
ADAM_BLOCK_ELEMS = 64 * 1024
CAST_BLOCK_ELEMS = 512 * 1024


def _params(*semantics):
    return pltpu.CompilerParams(dimension_semantics=semantics, vmem_limit_bytes=VMEM_LIMIT)


def _tile(dim, pref):
    t = min(dim, pref)
    assert dim % t == 0, (dim, pref)
    return t


def _mesh_position():
    return lax.axis_index("x"), lax.axis_index("y"), lax.axis_index("c")


def _slot(p):
    return 4 * p[0] + 2 * p[1] + p[2]


def _all_gather(arrs, name):
    n = len(arrs)

    def body(*refs):
        ins, outs = refs[:n], refs[n:2 * n]
        send_sems, recv_sems, local_sems = refs[2 * n:]
        x, y, c = _mesh_position()
        me, sibling = (x, y, c), (x, y, 1 - c)
        chips = [(1 - x, y), (x, 1 - y), (1 - x, 1 - y)]

        def copy(a, k, block, to, src=None):
            dst = outs[a].at[_slot(block)]
            return pltpu.make_async_remote_copy(
                src_ref=dst if src is None else src, dst_ref=dst,
                send_sem=send_sems.at[a, k], recv_sem=recv_sems.at[a, k],
                device_id=to, device_id_type=pl.DeviceIdType.MESH)

        mine = [pltpu.make_async_copy(ins[a], outs[a].at[_slot(me)], local_sems.at[a]) for a in range(n)]
        for cp in mine:
            cp.start()
        first = []
        for a in range(n):
            first.append(copy(a, 0, me, sibling, src=ins[a]))
            first += [copy(a, 1 + j, me, (*chip, c), src=ins[a]) for j, chip in enumerate(chips)]
        for cp in first:
            cp.start()
        passed = []
        for j, chip in enumerate(chips):
            for a in range(n):
                copy(a, 1 + j, (*chip, c), me).wait_recv()
                cp = copy(a, 4 + j, (*chip, c), sibling)
                cp.start()
                passed.append(cp)
        for a in range(n):
            copy(a, 0, sibling, me).wait_recv()
            for j, chip in enumerate(chips):
                copy(a, 4 + j, (*chip, 1 - c), me).wait_recv()
        for cp in first + passed:
            cp.wait_send()
        for cp in mine:
            cp.wait()

    any_spec = pl.BlockSpec(memory_space=pl.ANY)
    return pl.pallas_call(
        body, name=name,
        out_shape=[jax.ShapeDtypeStruct((N_DEV, *a.shape), a.dtype) for a in arrs],
        in_specs=[any_spec] * n, out_specs=[any_spec] * n,
        scratch_shapes=[pltpu.SemaphoreType.DMA((n, 7)), pltpu.SemaphoreType.DMA((n, 7)),
                        pltpu.SemaphoreType.DMA((n,))],
    )(*arrs)


def _peer(k):
    x, y, c = _mesh_position()
    return (1 - x if k & 4 else x, 1 - y if k & 2 else y, 1 - c if k & 1 else c)


_HBM = pl.BlockSpec(memory_space=pltpu.HBM)
_SEM = pl.BlockSpec(memory_space=pltpu.SEMAPHORE)
_EFFECT = pltpu.SideEffectType.DATAFLOW_SIDE_EFFECTING


def _exchange_copy(g_ref, land_ref, send_sems, recv_sems, a, k):
    return pltpu.make_async_remote_copy(
        src_ref=g_ref.at[_slot(_peer(k))], dst_ref=land_ref.at[k - 1],
        send_sem=send_sems.at[a * (N_DEV - 1) + k - 1], recv_sem=recv_sems.at[a * (N_DEV - 1) + k - 1],
        device_id=_peer(k), device_id_type=pl.DeviceIdType.MESH)


def _exchange_start(arrs, name):
    n = len(arrs)

    def body(*refs):
        g_refs, land_refs = refs[:n], refs[n:2 * n]
        send_sems, recv_sems = refs[2 * n:2 * n + 2]
        token = refs[-1]
        for k in range(1, N_DEV):
            for a in range(n):
                _exchange_copy(g_refs[a], land_refs[a], send_sems, recv_sems, a, k).start()
        token[...] = jnp.zeros_like(token)

    lands = [lax.empty((N_DEV - 1, *g.shape[1:]), g.dtype) for g in arrs]
    outs = pl.pallas_call(
        body, name=name,
        out_shape=(pltpu.SemaphoreType.DMA((n * (N_DEV - 1),)), pltpu.SemaphoreType.DMA((n * (N_DEV - 1),)),
                   *[pltpu.HBM(g.shape, g.dtype) for g in arrs], *[pltpu.HBM(z.shape, z.dtype) for z in lands],
                   jax.ShapeDtypeStruct((8, 128), F32)),
        in_specs=[_HBM] * (2 * n),
        out_specs=(_SEM, _SEM, *[_HBM] * (2 * n), pl.BlockSpec(memory_space=pltpu.VMEM)),
        input_output_aliases={i: 2 + i for i in range(2 * n)},
        compiler_params=pltpu.CompilerParams(has_side_effects=_EFFECT),
    )(*[pltpu.with_memory_space_constraint(g, pltpu.HBM) for g in arrs],
      *[pltpu.with_memory_space_constraint(z, pltpu.HBM) for z in lands])
    return outs[0], outs[1], list(outs[2:2 + n]), list(outs[2 + n:2 + 2 * n]), outs[-1]


def _exchange_wait(send_sems, recv_sems, arrs, lands, after, name):
    n = len(arrs)

    def body(*refs):
        g_refs, land_refs = refs[:n], refs[n:2 * n]
        send_sems_ref, recv_sems_ref = refs[2 * n:2 * n + 2]
        for k in range(1, N_DEV):
            for a in range(n):
                cp = _exchange_copy(g_refs[a], land_refs[a], send_sems_ref, recv_sems_ref, a, k)
                cp.wait_send()
                cp.wait_recv()

    outs = pl.pallas_call(
        body, name=name,
        out_shape=(*[pltpu.HBM(g.shape, g.dtype) for g in arrs], *[pltpu.HBM(z.shape, z.dtype) for z in lands]),
        in_specs=[_HBM] * (2 * n) + [_SEM, _SEM, pl.BlockSpec(memory_space=pl.ANY)],
        out_specs=[_HBM] * (2 * n),
        input_output_aliases={i: i for i in range(2 * n)},
        compiler_params=pltpu.CompilerParams(has_side_effects=_EFFECT),
    )(*arrs, *lands, send_sems, recv_sems, after)
    return list(outs[:n]), list(outs[n:])


N_FIRST = 4
N_PASSED = 3


def _other_chips():
    x, y, _ = _mesh_position()
    return [(1 - x, y), (x, 1 - y), (1 - x, 1 - y)]


def _gather_copy(src_ref, land_ref, block, to, send_sem, recv_sem):
    rows = land_ref.at[_slot(block)]
    return pltpu.make_async_remote_copy(
        src_ref=rows if src_ref is None else src_ref, dst_ref=rows, send_sem=send_sem, recv_sem=recv_sem,
        device_id=to, device_id_type=pl.DeviceIdType.MESH)


def _gather_start(arrs, me_slot, after, name):
    n = len(arrs)

    def body(*refs):
        src_refs, land_refs = refs[:n], refs[n:2 * n]
        send_sems, recv_sems = refs[2 * n + 1:2 * n + 3]
        token = refs[-1]
        x, y, c = _mesh_position()
        targets = [(x, y, 1 - c)] + [(*chip, c) for chip in _other_chips()]
        for a in range(n):
            for k, to in enumerate(targets):
                _gather_copy(src_refs[a], land_refs[a], (x, y, c), to,
                             send_sems.at[a * N_FIRST + k], recv_sems.at[a * N_FIRST + k]).start()
        token[...] = jnp.zeros_like(token)

    lands = [lax.dynamic_update_slice_in_dim(lax.empty((N_DEV, *a.shape), a.dtype), a[None], me_slot, axis=0)
             for a in arrs]
    outs = pl.pallas_call(
        body, name=name,
        out_shape=(pltpu.SemaphoreType.DMA((n * N_FIRST,)), pltpu.SemaphoreType.DMA((n * N_FIRST,)),
                   *[pltpu.HBM(a.shape, a.dtype) for a in arrs], *[pltpu.HBM(z.shape, z.dtype) for z in lands],
                   jax.ShapeDtypeStruct((8, 128), F32)),
        in_specs=[_HBM] * (2 * n) + [pl.BlockSpec(memory_space=pl.ANY)],
        out_specs=(_SEM, _SEM, *[_HBM] * (2 * n), pl.BlockSpec(memory_space=pltpu.VMEM)),
        input_output_aliases={i: 2 + i for i in range(2 * n)},
        compiler_params=pltpu.CompilerParams(has_side_effects=_EFFECT),
    )(*[pltpu.with_memory_space_constraint(a, pltpu.HBM) for a in arrs],
      *[pltpu.with_memory_space_constraint(z, pltpu.HBM) for z in lands], after)
    return outs[0], outs[1], list(outs[2:2 + n]), list(outs[2 + n:2 + 2 * n]), outs[-1]


def _gather_pass_on(first_recv_sems, base, lands, after, name):
    n = len(lands)

    def body(*refs):
        land_refs = refs[:n]
        first_recv = refs[n]
        send_sems, recv_sems = refs[n + 2:n + 4]
        x, y, c = _mesh_position()
        sibling = (x, y, 1 - c)
        for j, chip in enumerate(_other_chips()):
            for a in range(n):
                _gather_copy(None, land_refs[a], (*chip, c), sibling,
                             send_sems.at[a * N_PASSED + j], first_recv.at[(base + a) * N_FIRST + 1 + j]).wait_recv()
                _gather_copy(None, land_refs[a], (*chip, c), sibling,
                             send_sems.at[a * N_PASSED + j], recv_sems.at[a * N_PASSED + j]).start()

    outs = pl.pallas_call(
        body, name=name,
        out_shape=(pltpu.SemaphoreType.DMA((n * N_PASSED,)), pltpu.SemaphoreType.DMA((n * N_PASSED,)),
                   *[pltpu.HBM(z.shape, z.dtype) for z in lands]),
        in_specs=[_HBM] * n + [_SEM, pl.BlockSpec(memory_space=pl.ANY)],
        out_specs=(_SEM, _SEM, *[_HBM] * n),
        input_output_aliases={i: 2 + i for i in range(n)},
        compiler_params=pltpu.CompilerParams(has_side_effects=_EFFECT),
    )(*lands, first_recv_sems, after)
    return outs[0], outs[1], list(outs[2:])


def _gather_finish(first_sems, base, passed_sems, arrs, lands, after, name):
    n = len(lands)

    def body(*refs):
        src_refs, land_refs = refs[:n], refs[n:2 * n]
        first_send, first_recv, passed_send, passed_recv = refs[2 * n:2 * n + 4]
        x, y, c = _mesh_position()
        sibling = (x, y, 1 - c)
        chips = _other_chips()
        for a in range(n):
            _gather_copy(src_refs[a], land_refs[a], sibling, sibling,
                         first_send.at[(base + a) * N_FIRST], first_recv.at[(base + a) * N_FIRST]).wait_recv()
            for j, chip in enumerate(chips):
                _gather_copy(None, land_refs[a], (*chip, 1 - c), sibling,
                             passed_send.at[a * N_PASSED + j], passed_recv.at[a * N_PASSED + j]).wait_recv()
        for a in range(n):
            for k in range(N_FIRST):
                _gather_copy(src_refs[a], land_refs[a], (x, y, c), sibling,
                             first_send.at[(base + a) * N_FIRST + k], first_recv.at[(base + a) * N_FIRST + k]).wait_send()
            for j, chip in enumerate(chips):
                _gather_copy(None, land_refs[a], (*chip, c), sibling,
                             passed_send.at[a * N_PASSED + j], passed_recv.at[a * N_PASSED + j]).wait_send()

    outs = pl.pallas_call(
        body, name=name,
        out_shape=(*[pltpu.HBM(a.shape, a.dtype) for a in arrs], *[pltpu.HBM(z.shape, z.dtype) for z in lands]),
        in_specs=[_HBM] * (2 * n) + [_SEM] * 4 + [pl.BlockSpec(memory_space=pl.ANY)],
        out_specs=[_HBM] * (2 * n),
        input_output_aliases={i: i for i in range(2 * n)},
        compiler_params=pltpu.CompilerParams(has_side_effects=_EFFECT),
    )(*arrs, *lands, *first_sems, *passed_sems, after)
    return list(outs[n:])


def _run_after(deps, body, in_specs, operands):
    n_deps = len(deps)
    if n_deps == 0:
        return body, in_specs, operands

    def body_behind(*refs):
        body(*refs[n_deps:])

    return body_behind, [pl.BlockSpec(memory_space=pl.ANY)] * n_deps + list(in_specs), list(deps) + list(operands)


def _zero_at_first(acc_ref, k, nk):
    if nk > 1:
        @pl.when(k == 0)
        def _():
            acc_ref[...] = jnp.zeros_like(acc_ref)


def _accumulate(acc_ref, product, k, nk, finish):
    if nk == 1:
        finish(product())
        return

    @pl.when(k < nk - 1)
    def _():
        acc_ref[...] += product()

    @pl.when(k == nk - 1)
    def _():
        finish(acc_ref[...] + product())


def _blocks_per_tile(n_blocks, width, pref):
    if width >= pref:
        return 1
    per = min(n_blocks, pref // width)
    assert n_blocks % per == 0
    return per


def _mm_nn(a, b3, *, out_dtype, name, res=None, with_relu2=False, deps=()):
    m, kdim = a.shape
    nb, kb_, nw = b3.shape
    assert kb_ == kdim
    per_tile = _blocks_per_tile(nb, nw, MM_TILE)
    tw = _tile(nw, MM_TILE)
    per_block = nw // tw
    tn = per_tile * tw
    tm, tk = _tile(m, MM_TILE), _tile(kdim, MM_TILE_K)
    nk = kdim // tk
    n_in = 2 + (res is not None)

    def body(*refs):
        a_ref, b_ref = refs[:2]
        r_ref = refs[2] if res is not None else None
        o_ref = refs[n_in]
        s_ref = refs[n_in + 1] if with_relu2 else None
        acc_ref = refs[-1] if nk > 1 else None
        k = pl.program_id(2)
        _zero_at_first(acc_ref, k, nk)
        for b in range(per_tile):
            cols = slice(b * tw, (b + 1) * tw)

            def product(b=b):
                return jnp.dot(a_ref[...], b_ref[b], preferred_element_type=F32)

            def finish(total, cols=cols):
                if r_ref is not None:
                    total = total + r_ref[:, cols]
                o_ref[:, cols] = total.astype(out_dtype)
                if s_ref is not None:
                    r = jnp.maximum(total, 0.0)
                    s_ref[:, cols] = (r * r).astype(BF16)

            _accumulate(acc_ref.at[:, cols] if nk > 1 else None, product, k, nk, finish)

    if per_tile > 1:
        b_spec = pl.BlockSpec((per_tile, tk, nw), lambda i, j, k: (j, k, 0))
    else:
        b_spec = pl.BlockSpec((1, tk, tw), lambda i, j, k: (j // per_block, k, j % per_block))
    in_specs = [pl.BlockSpec((tm, tk), lambda i, j, k: (i, k)), b_spec]
    operands = [a, b3]
    if res is not None:
        in_specs.append(pl.BlockSpec((tm, tn), lambda i, j, k: (i, j)))
        operands.append(res)
    body, in_specs, operands = _run_after(deps, body, in_specs, operands)
    o_spec = pl.BlockSpec((tm, tn), lambda i, j, k: (i, j))
    out_shape = [jax.ShapeDtypeStruct((m, nb * nw), out_dtype)]
    if with_relu2:
        out_shape.append(jax.ShapeDtypeStruct((m, nb * nw), BF16))
    outs = pl.pallas_call(
        body, name=name, grid=(m // tm, (nb * nw) // tn, nk),
        out_shape=out_shape, in_specs=in_specs, out_specs=[o_spec] * len(out_shape),
        scratch_shapes=[pltpu.VMEM((tm, tn), F32)] if nk > 1 else [],
        compiler_params=_params("parallel", "parallel", "arbitrary"),
    )(*operands)
    return outs if with_relu2 else outs[0]


def _mm_nt(a, b3, *, out_dtype, name, relu2_grad_of=None, deps=()):
    m, kdim = a.shape
    kb, n, kw = b3.shape
    assert kb * kw == kdim
    per_step = _blocks_per_tile(kb, kw, MM_TILE_K)
    tw = _tile(kw, MM_TILE_K)
    per_block = kw // tw
    tk = per_step * tw
    tm, tn = _tile(m, MM_TILE), _tile(n, MM_TILE)
    nk = kdim // tk

    def body(*refs):
        if relu2_grad_of is None:
            a_ref, b_ref, o_ref = refs[:3]
            g_ref = None
        else:
            a_ref, b_ref, g_ref, o_ref = refs[:4]
        acc_ref = refs[-1] if nk > 1 else None
        k = pl.program_id(2)
        _zero_at_first(acc_ref, k, nk)
        def product():
            p = None
            for b in range(per_step):
                part = lax.dot_general(a_ref[:, b * tw:(b + 1) * tw], b_ref[b], (((1,), (1,)), ((), ())),
                                       preferred_element_type=F32)
                p = part if p is None else p + part
            return p

        def finish(total):
            if g_ref is not None:
                total = total * (2.0 * jnp.maximum(g_ref[...].astype(F32), 0.0))
            o_ref[...] = total.astype(out_dtype)

        _accumulate(acc_ref, product, k, nk, finish)

    if per_step > 1:
        b_spec = pl.BlockSpec((per_step, tn, kw), lambda i, j, k: (k, j, 0))
    else:
        b_spec = pl.BlockSpec((1, tn, tw), lambda i, j, k: (k // per_block, j, k % per_block))
    in_specs = [pl.BlockSpec((tm, tk), lambda i, j, k: (i, k)), b_spec]
    operands = [a, b3]
    if relu2_grad_of is not None:
        in_specs.append(pl.BlockSpec((tm, tn), lambda i, j, k: (i, j)))
        operands.append(relu2_grad_of)
    body, in_specs, operands = _run_after(deps, body, in_specs, operands)
    return pl.pallas_call(
        body, name=name, grid=(m // tm, n // tn, nk),
        out_shape=jax.ShapeDtypeStruct((m, n), out_dtype),
        in_specs=in_specs, out_specs=pl.BlockSpec((tm, tn), lambda i, j, k: (i, j)),
        scratch_shapes=[pltpu.VMEM((tm, tn), F32)] if nk > 1 else [],
        compiler_params=_params("parallel", "parallel", "arbitrary"),
    )(*operands)


def _mm_tn(a, b, *, n_blocks, name, deps=()):
    t, m = a.shape
    t2, n = b.shape
    assert t == t2 and n % n_blocks == 0
    nw = n // n_blocks
    per_tile = _blocks_per_tile(n_blocks, nw, MM_TILE)
    tw = _tile(nw, MM_TILE)
    per_block = nw // tw
    tn = per_tile * tw
    tm, tk = _tile(m, MM_TILE), _tile(t, MM_TILE_K)
    nk = t // tk

    def body(a_ref, b_ref, o_ref, *scratch):
        acc_ref = scratch[0] if nk > 1 else None
        k = pl.program_id(2)
        _zero_at_first(acc_ref, k, nk)
        def product():
            return lax.dot_general(a_ref[...], b_ref[...], (((0,), (0,)), ((), ())), preferred_element_type=F32)

        def finish(total):
            for b in range(per_tile):
                o_ref[b] = total[:, b * tw:(b + 1) * tw].astype(BF16)

        _accumulate(acc_ref, product, k, nk, finish)

    if per_tile > 1:
        o_spec = pl.BlockSpec((per_tile, tm, nw), lambda i, j, k: (j, i, 0))
    else:
        o_spec = pl.BlockSpec((1, tm, tw), lambda i, j, k: (j // per_block, i, j % per_block))
    in_specs = [pl.BlockSpec((tk, tm), lambda i, j, k: (k, i)), pl.BlockSpec((tk, tn), lambda i, j, k: (k, j))]
    body, in_specs, operands = _run_after(deps, body, in_specs, [a, b])
    return pl.pallas_call(
        body, name=name, grid=(m // tm, n // tn, nk),
        out_shape=jax.ShapeDtypeStruct((n_blocks, m, nw), BF16),
        in_specs=in_specs, out_specs=o_spec,
        scratch_shapes=[pltpu.VMEM((tm, tn), F32)] if nk > 1 else [],
        compiler_params=_params("parallel", "parallel", "arbitrary"),
    )(*operands)


def _normalise(x):
    r = lax.rsqrt(jnp.mean(x * x, axis=-1, keepdims=True) + NORM_EPS)
    return x * r, r


def _rmsnorm_backward(dh, xhat, r, gain):
    dxhat = dh * gain
    return r * (dxhat - xhat * jnp.mean(dxhat * xhat, axis=-1, keepdims=True))


def _rmsnorm(x, gain, name, deps=()):
    t, d = x.shape
    tr = _tile(t, ROW_TILE)

    def body(x_ref, g_ref, o_ref, xb_ref):
        xv = x_ref[...]
        xhat, _ = _normalise(xv)
        o_ref[...] = (xhat * g_ref[...]).astype(BF16)
        xb_ref[...] = xv.astype(BF16)

    row = pl.BlockSpec((tr, d), lambda i: (i, 0))
    in_specs = [row, pl.BlockSpec((1, d), lambda i: (0, 0))]
    body, in_specs, operands = _run_after(deps, body, in_specs, [x, gain.reshape(1, d)])
    return pl.pallas_call(
        body, name=name, grid=(t // tr,),
        out_shape=[jax.ShapeDtypeStruct((t, d), BF16)] * 2,
        in_specs=in_specs, out_specs=[row, row],
        compiler_params=_params("parallel"),
    )(*operands)


def _rmsnorm_bwd(dh, x, gain, dres, out_dtype, name):
    t, d = x.shape
    tr = _tile(t, ROW_TILE)

    def body(dh_ref, x_ref, g_ref, r_ref, dx_ref, dg_ref):
        xhat, r = _normalise(x_ref[...].astype(F32))
        dh_v = dh_ref[...].astype(F32)
        dx = r_ref[...].astype(F32) + _rmsnorm_backward(dh_v, xhat, r, g_ref[...])
        dx_ref[...] = dx.astype(out_dtype)
        part = jnp.sum(dh_v * xhat, axis=0, keepdims=True)

        @pl.when(pl.program_id(0) == 0)
        def _():
            dg_ref[...] = part

        @pl.when(pl.program_id(0) > 0)
        def _():
            dg_ref[...] += part

    row = pl.BlockSpec((tr, d), lambda i: (i, 0))
    vec = pl.BlockSpec((1, d), lambda i: (0, 0))
    return pl.pallas_call(
        body, name=name, grid=(t // tr,),
        out_shape=[jax.ShapeDtypeStruct((t, d), out_dtype), jax.ShapeDtypeStruct((1, d), F32)],
        in_specs=[row, row, vec, row], out_specs=[row, vec],
        compiler_params=_params("arbitrary"),
    )(dh, x, gain.reshape(1, d), dres)


def _loss_head(x, gain, target, name):
    t, d = x.shape
    tr = _tile(t, ROW_TILE)
    steps = t // tr

    def body(x_ref, g_ref, t_ref, loss_ref, dxb_ref, dg_ref, sq_ref):
        i = pl.program_id(0)
        xhat, r = _normalise(x_ref[...])
        gain_v = g_ref[...]
        diff = xhat * gain_v - t_ref[...]
        dy = diff / float(d)
        dxb_ref[...] = _rmsnorm_backward(dy, xhat, r, gain_v).astype(BF16)
        dg_part = jnp.sum(dy * xhat, axis=0, keepdims=True)
        sq_part = jnp.sum(diff * diff, axis=0, keepdims=True)

        @pl.when(i == 0)
        def _():
            dg_ref[...] = dg_part
            sq_ref[...] = sq_part

        @pl.when(i > 0)
        def _():
            dg_ref[...] += dg_part
            sq_ref[...] += sq_part

        @pl.when(i == steps - 1)
        def _():
            loss_ref[...] = (0.5 / float(d)) * jnp.sum(sq_ref[...], axis=1, keepdims=True)

    row = pl.BlockSpec((tr, d), lambda i: (i, 0))
    vec = pl.BlockSpec((1, d), lambda i: (0, 0))
    return pl.pallas_call(
        body, name=name, grid=(steps,),
        out_shape=[jax.ShapeDtypeStruct((1, 1), F32), jax.ShapeDtypeStruct((t, d), BF16),
                   jax.ShapeDtypeStruct((1, d), F32)],
        in_specs=[row, vec, row],
        out_specs=[pl.BlockSpec((1, 1), lambda i: (0, 0)), row, vec],
        scratch_shapes=[pltpu.VMEM((1, d), F32)],
        compiler_params=_params("arbitrary"),
    )(x, gain.reshape(1, d), target)


def _shift_down(ext, s):
    return pltpu.roll(ext, s, 0)


def _shift_up(ext, s):
    return pltpu.roll(ext, ext.shape[0] - s, 0)


def _window_sum(ext, w, shift):
    s = 1
    while s < w:
        ext = ext + shift(ext, s)
        s *= 2
    return ext


def _window_count(tile_index, rows, cols, w):
    t = tile_index * rows + lax.broadcasted_iota(jnp.int32, (rows, cols), 0)
    return jnp.minimum(t + 1, w).astype(F32)


def _mixer_sizes(proj, conv_w):
    t, e = proj.shape
    dc = conv_w.shape[1]
    dp = e - 3 * dc
    cg = dp // len(POOL_WINDOWS)
    tt = _tile(t, MIXER_TILE)
    assert tt % HALO_ROWS == 0 and tt >= HALO_ROWS
    cw = _tile(dc, cg)
    return t, e, dc, dp, cg, tt, cw


def _mixer_fwd(proj, conv_w, pool_w, pool_scale, name, deps=()):
    t, e, dc, dp, cg, tt, cw = _mixer_sizes(proj, conv_w)
    per_halo = tt // HALO_ROWS

    def body(cur_ref, prev_ref, cw_ref, pw_ref, ps_ref, y_ref):
        i = pl.program_id(0)
        first = i == 0

        def cur(lo, width):
            return cur_ref[:, lo:lo + width].astype(F32)

        def prev(lo, width):
            return jnp.where(first, 0.0, prev_ref[:, lo:lo + width].astype(F32))

        for lo in range(0, dc, cw):
            u = cur(dc + lo, cw) * cur(2 * dc + lo, cw)
            ext = jnp.concatenate([prev(dc + lo, cw) * prev(2 * dc + lo, cw), u], axis=0)
            u1 = _shift_down(ext, 1)[HALO_ROWS:]
            u2 = _shift_down(ext, 2)[HALO_ROWS:]
            conv = cw_ref[0:1, lo:lo + cw] * u2 + cw_ref[1:2, lo:lo + cw] * u1 + cw_ref[2:3, lo:lo + cw] * u
            y_ref[:, lo:lo + cw] = (cur(lo, cw) * conv).astype(BF16)

        for g, w in enumerate(POOL_WINDOWS):
            lo = 3 * dc + g * cg
            v = cur(lo, cg)
            ext = jnp.concatenate([prev(lo, cg), v], axis=0)
            mean = _window_sum(ext, w, _shift_down)[HALO_ROWS:] / _window_count(i, tt, cg, w)
            z = jnp.dot((mean - v).astype(BF16), pw_ref[g], preferred_element_type=F32)
            y_ref[:, dc + g * cg:dc + (g + 1) * cg] = (z * ps_ref[0:1, g * cg:(g + 1) * cg]).astype(BF16)

    in_specs = [pl.BlockSpec((tt, e), lambda i: (i, 0)),
                pl.BlockSpec((HALO_ROWS, e), lambda i: (jnp.maximum(i * per_halo - 1, 0), 0)),
                pl.BlockSpec((3, dc), lambda i: (0, 0)),
                pl.BlockSpec((len(POOL_WINDOWS), cg, cg), lambda i: (0, 0, 0)),
                pl.BlockSpec((1, dp), lambda i: (0, 0))]
    body, in_specs, operands = _run_after(
        deps, body, in_specs, [proj, proj, conv_w, pool_w, pool_scale.reshape(1, dp)])
    return pl.pallas_call(
        body, name=name, grid=(t // tt,),
        out_shape=jax.ShapeDtypeStruct((t, dc + dp), BF16),
        in_specs=in_specs,
        out_specs=pl.BlockSpec((tt, dc + dp), lambda i: (i, 0)),
        compiler_params=_params("parallel"),
    )(*operands)


def _mixer_bwd(proj, dy, conv_w, pool_w, pool_scale, name, deps=()):
    t, e, dc, dp, cg, tt, cw = _mixer_sizes(proj, conv_w)
    per_halo = tt // HALO_ROWS
    steps = t // tt
    n_groups = len(POOL_WINDOWS)

    def body(cur_ref, prev_ref, next_ref, dy_ref, dyn_ref, cw_ref, pw_ref, ps_ref,
             dp_ref, dcw_ref, dpw_ref, dps_ref):
        i = pl.program_id(0)
        first = i == 0
        last = i == steps - 1

        @pl.when(first)
        def _():
            dcw_ref[...] = jnp.zeros_like(dcw_ref)
            dpw_ref[...] = jnp.zeros_like(dpw_ref)
            dps_ref[...] = jnp.zeros_like(dps_ref)

        def cur(lo, width):
            return cur_ref[:, lo:lo + width].astype(F32)

        def prev(lo, width):
            return jnp.where(first, 0.0, prev_ref[:, lo:lo + width].astype(F32))

        def nxt(ref, lo, width):
            return jnp.where(last, 0.0, ref[:, lo:lo + width].astype(F32))

        def colsum(v):
            return jnp.sum(v, axis=0, keepdims=True)

        for lo in range(0, dc, cw):
            cols = slice(lo, lo + cw)
            b, c, xt = cur(lo, cw), cur(dc + lo, cw), cur(2 * dc + lo, cw)
            u = c * xt
            ext = jnp.concatenate([prev(dc + lo, cw) * prev(2 * dc + lo, cw), u], axis=0)
            u1 = _shift_down(ext, 1)[HALO_ROWS:]
            u2 = _shift_down(ext, 2)[HALO_ROWS:]
            w0, w1, w2 = cw_ref[0:1, cols], cw_ref[1:2, cols], cw_ref[2:3, cols]
            dyc = dy_ref[:, cols].astype(F32)
            dp_ref[:, cols] = (dyc * (w0 * u2 + w1 * u1 + w2 * u)).astype(BF16)
            dconv = dyc * b
            dcw_ref[0:1, cols] += colsum(dconv * u2)
            dcw_ref[1:2, cols] += colsum(dconv * u1)
            dcw_ref[2:3, cols] += colsum(dconv * u)
            dext = jnp.concatenate([dconv, nxt(dyn_ref, lo, cw) * nxt(next_ref, lo, cw)], axis=0)
            du = w2 * dconv + w1 * _shift_up(dext, 1)[:tt] + w0 * _shift_up(dext, 2)[:tt]
            dp_ref[:, dc + lo:dc + lo + cw] = (du * xt).astype(BF16)
            dp_ref[:, 2 * dc + lo:2 * dc + lo + cw] = (du * c).astype(BF16)

        for g, w in enumerate(POOL_WINDOWS):
            lo = 3 * dc + g * cg
            ycols = slice(dc + g * cg, dc + (g + 1) * cg)
            pcols = slice(g * cg, (g + 1) * cg)
            v = cur(lo, cg)
            ext = jnp.concatenate([prev(lo, cg), v], axis=0)
            count = _window_count(i, tt, cg, w)
            d = ((_window_sum(ext, w, _shift_down)[HALO_ROWS:] / count) - v).astype(BF16)
            pw = pw_ref[g]
            scale = ps_ref[0:1, pcols]
            dyp = dy_ref[:, ycols].astype(F32)
            z = jnp.dot(d, pw, preferred_element_type=F32)
            dps_ref[0:1, pcols] += colsum(dyp * z)
            dz = (dyp * scale).astype(BF16)
            dpw_ref[g] += lax.dot_general(d, dz, (((0,), (0,)), ((), ())), preferred_element_type=F32)
            dd = lax.dot_general(dz, pw, (((1,), (1,)), ((), ())), preferred_element_type=F32)
            dzn = (nxt(dyn_ref, dc + g * cg, cg) * scale).astype(BF16)
            ddn = lax.dot_general(dzn, pw, (((1,), (1,)), ((), ())), preferred_element_type=F32)
            qext = jnp.concatenate([dd / count, ddn / float(w)], axis=0)
            dp_ref[:, lo:lo + cg] = (_window_sum(qext, w, _shift_up)[:tt] - dd).astype(BF16)

    cur_spec = lambda width: pl.BlockSpec((tt, width), lambda i: (i, 0))
    prev_spec = pl.BlockSpec((HALO_ROWS, e), lambda i: (jnp.maximum(i * per_halo - 1, 0), 0))
    next_spec = lambda width: pl.BlockSpec(
        (HALO_ROWS, width), lambda i: (jnp.minimum((i + 1) * per_halo, t // HALO_ROWS - 1), 0))
    in_specs = [cur_spec(e), prev_spec, next_spec(e), cur_spec(dc + dp), next_spec(dc + dp),
                pl.BlockSpec((3, dc), lambda i: (0, 0)),
                pl.BlockSpec((n_groups, cg, cg), lambda i: (0, 0, 0)),
                pl.BlockSpec((1, dp), lambda i: (0, 0))]
    body, in_specs, operands = _run_after(
        deps, body, in_specs, [proj, proj, proj, dy, dy, conv_w, pool_w, pool_scale.reshape(1, dp)])
    return pl.pallas_call(
        body, name=name, grid=(steps,),
        out_shape=[jax.ShapeDtypeStruct((t, e), BF16), jax.ShapeDtypeStruct((3, dc), F32),
                   jax.ShapeDtypeStruct((n_groups, cg, cg), F32), jax.ShapeDtypeStruct((1, dp), F32)],
        in_specs=in_specs,
        out_specs=[cur_spec(e), pl.BlockSpec((3, dc), lambda i: (0, 0)),
                   pl.BlockSpec((n_groups, cg, cg), lambda i: (0, 0, 0)),
                   pl.BlockSpec((1, dp), lambda i: (0, 0))],
        compiler_params=_params("arbitrary"),
    )(*operands)


def _adamw(partials, w, m, v, me_slot, name, mine=()):
    n_layers, r, c = w.shape
    assert len(partials) == n_layers and len(mine) in (0, n_layers)
    n_mine = len(mine)
    tr = r if r * c <= ADAM_BLOCK_ELEMS else _tile(r, max(16, ADAM_BLOCK_ELEMS // c))

    def body(me_ref, *refs):
        own_refs, p_refs = refs[:n_mine], refs[n_mine:n_mine + n_layers]
        w_ref, m_ref, v_ref, g_out, d_out, m_out, v_out = refs[n_mine + n_layers:]
        for l in range(n_layers):
            g = own_refs[l][0].astype(F32) if n_mine else None
            for s in range(p_refs[l].shape[0]):
                part = p_refs[l][s].astype(F32)
                g = part if g is None else g + part
            m_new = ADAM_B1 * m_ref[l] + (1.0 - ADAM_B1) * g
            v_new = ADAM_B2 * v_ref[l] + (1.0 - ADAM_B2) * (g * g)
            m_hat = m_new / (1.0 - ADAM_B1 ** ADAM_STEP)
            v_hat = v_new / (1.0 - ADAM_B2 ** ADAM_STEP)
            g_out[l] = g
            d_out[l] = -ADAM_LR * (m_hat / (jnp.sqrt(v_hat) + ADAM_EPS) + ADAM_WD * w_ref[l])
            m_out[l] = m_new
            v_out[l] = v_new

    own_spec = pl.BlockSpec((1, tr, c), lambda i, me_ref: (me_ref[0], i, 0))
    p_specs = [pl.BlockSpec((p.shape[0], tr, c), lambda i, me_ref: (0, i, 0)) for p in partials]
    w_spec = pl.BlockSpec((n_layers, tr, c), lambda i, me_ref: (0, i, 0))
    return pl.pallas_call(
        body, name=name,
        grid_spec=pltpu.PrefetchScalarGridSpec(
            num_scalar_prefetch=1, grid=(r // tr,),
            in_specs=[own_spec] * n_mine + p_specs + [w_spec] * 3, out_specs=[w_spec] * 4),
        out_shape=[jax.ShapeDtypeStruct(w.shape, F32)] * 4,
        compiler_params=_params("parallel"),
    )(jnp.reshape(me_slot, (1,)).astype(jnp.int32), *mine, *partials, w, m, v)


def _to_bf16(w, first, last, name, deps=()):
    _, r, c = w.shape
    tr = r if r * c <= CAST_BLOCK_ELEMS else _tile(r, max(16, CAST_BLOCK_ELEMS // c))

    def body(w_ref, o_ref):
        o_ref[...] = w_ref[...].astype(BF16)

    in_specs = [pl.BlockSpec((1, tr, c), lambda l, i: (first + l, i, 0))]
    body, in_specs, operands = _run_after(deps, body, in_specs, [w])
    return pl.pallas_call(
        body, name=name, grid=(last - first, r // tr),
        out_shape=jax.ShapeDtypeStruct((last - first, r, c), BF16),
        in_specs=in_specs, out_specs=pl.BlockSpec((1, tr, c), lambda l, i: (l, i, 0)),
        compiler_params=_params("parallel", "parallel"),
    )(*operands)


def kernel(x, w_in, conv_w, pool_w, pool_scale, w_out, norm_mix, norm_mlp, w_up, w_down, norm_final, loss_target, m_w_in, m_conv_w, m_pool_w, m_pool_scale, m_w_out, m_norm_mix, m_norm_mlp, m_w_up, m_w_down, m_norm_final, v_w_in, v_conv_w, v_pool_w, v_pool_scale, v_w_out, v_norm_mix, v_norm_mlp, v_w_up, v_w_down, v_norm_final):
    n_layers, d, e_shard = w_in.shape
    t = x.shape[1]
    n_groups, cg_shard, cg = pool_w.shape[1:]
    dc_shard = conv_w.shape[2]
    dc, dp = dc_shard * N_DEV, n_groups * cg
    f_shard = w_up.shape[2]
    xs = x.reshape(t, d)
    target = loss_target.reshape(t, d)

    me = _slot(_mesh_position())
    big = (w_in, w_out, w_up, w_down)
    n_kinds = len(big)
    first_layer = [_to_bf16(w, 0, 1, "cast_first_layer")[0] for w in big]
    mixer_blocks = [conv_w.reshape(n_layers * 3, dc_shard), pool_w.reshape(n_layers * n_groups * cg_shard, cg)]
    groups = {}

    def start(tag, blocks, after):
        send_sems, recv_sems, blocks, lands, token = _gather_start(blocks, me, after, f"weights_start_{tag}")
        groups[tag] = (send_sems, recv_sems, blocks, lands)
        return token

    token = start("first", [first_layer[0], *mixer_blocks, *first_layer[1:]], xs)
    if n_layers > 1:
        others = [_to_bf16(w, 1, n_layers, "cast_other_layers", deps=(token,)) for w in big]
        token = start("rest", [others[k][l - 1] for l in range(1, n_layers) for k in range(n_kinds)], token)
    mixer_place = ("first", 1)

    def place(l, k):
        return ("first", 0 if k == 0 else k + len(mixer_blocks)) if l == 0 else ("rest", n_kinds * (l - 1) + k)

    def passed_on(where, n, after, name):
        tag, i = where
        lands = groups[tag][3]
        send_sems, recv_sems, lands[i:i + n] = _gather_pass_on(groups[tag][1], i, lands[i:i + n], after, f"weights_pass_{name}")
        return send_sems, recv_sems

    def gathered(where, n, passed_sems, after, name):
        tag, i = where
        first_send, first_recv, blocks, lands = groups[tag]
        lands[i:i + n] = _gather_finish((first_send, first_recv), i, passed_sems, blocks[i:i + n], lands[i:i + n], after,
                                        f"weights_finish_{name}")
        return lands[i:i + n]

    def landing(where):
        return groups[where[0]][3][where[1]]

    saved, weights = [], []
    xc = xs
    after = token
    for l in range(n_layers):
        sems = passed_on(place(l, 0), 1, after, f"{l}_0")
        h1, x0b = _rmsnorm(xc, norm_mix[l], "norm_mix", deps=(landing(place(l, 0)),))
        win, = gathered(place(l, 0), 1, sems, h1, f"{l}_0")
        proj = _mm_nn(h1, win, out_dtype=BF16, name="in_proj")
        if l == 0:
            mixer_sems = passed_on(mixer_place, len(mixer_blocks), proj, "mixer")
        sems = passed_on(place(l, 1), 1, proj, f"{l}_1")
        if l == 0:
            conv_g, pool_g = gathered(mixer_place, len(mixer_blocks), mixer_sems, proj, "mixer")
            conv_full = conv_g.reshape(N_DEV, n_layers, 3, dc_shard).transpose(1, 2, 0, 3).reshape(n_layers, 3, dc)
            pool_full = pool_g.reshape(N_DEV, n_layers, n_groups, cg_shard, cg).transpose(1, 2, 0, 3, 4)
            pool_full = pool_full.reshape(n_layers, n_groups, cg, cg).astype(BF16)
        y = _mixer_fwd(proj, conv_full[l], pool_full[l], pool_scale[l], "mixer_fwd", deps=(landing(place(l, 1)),))
        wout = gathered(place(l, 1), 1, sems, y, f"{l}_1")[0].reshape(1, d, d)
        x1 = _mm_nn(y, wout, out_dtype=F32, res=xc, name="out_proj")
        sems = passed_on(place(l, 2), 1, x1, f"{l}_2")
        h2, x1b = _rmsnorm(x1, norm_mlp[l], "norm_mlp", deps=(landing(place(l, 2)),))
        wup, = gathered(place(l, 2), 1, sems, h2, f"{l}_2")
        if l < 2:
            a, s = _mm_nn(h2, wup, out_dtype=BF16, with_relu2=True, name="mlp_up")
            sems = passed_on(place(l, 3), 1, a, f"{l}_3")
        else:
            sems = passed_on(place(l, 3), 1, h2, f"{l}_3")
            a, s = _mm_nn(h2, wup, out_dtype=BF16, with_relu2=True, name="mlp_up", deps=(landing(place(l, 3)),))
        wdown = gathered(place(l, 3), 1, sems, a, f"{l}_3")[0].reshape(1, f_shard * N_DEV, d)
        x2 = _mm_nn(s, wdown, out_dtype=F32, res=x1, name="mlp_down")
        weights.append((win, wout, wup, wdown))
        saved.append((x0b, h1, proj, y, x1b, h2, a, s))
        xc = after = x2

    loss_part, dxb, g_norm_final = _loss_head(xc, norm_final, target, "loss_head")
    loss = lax.psum(loss_part[0, 0], MESH_AXES)

    in_flight = {}
    received = {}

    def push(l, arrs, keys, tag):
        send_sems, recv_sems, thru, lands, token = _exchange_start(arrs, f"grads_start_{tag}_{l}")
        in_flight.setdefault(l, []).append((send_sems, recv_sems, thru, lands, keys, tag))
        return token

    def land(l, after):
        for send_sems, recv_sems, thru, lands, keys, tag in in_flight.pop(l):
            thru, lands = _exchange_wait(send_sems, recv_sems, thru, lands, after, f"grads_wait_{tag}_{l}")
            for key, own, got in zip(keys, thru, lands):
                received[key, l] = (own, got)

    g_conv, g_scale, g_mix, g_mlp = ([None] * n_layers for _ in range(4))
    for l in reversed(range(n_layers)):
        win, wout, wup, wdown = weights[l]
        x0b, h1, proj, y, x1b, h2, a, s = saved[l]
        gw_down = _mm_tn(s, dxb, n_blocks=1, name="mlp_down_dw")
        tok_down = push(l, [gw_down.reshape(N_DEV, f_shard, d)], ["w_down"], "down")
        da = _mm_nt(dxb, wdown, out_dtype=BF16, relu2_grad_of=a, name="mlp_down_dx", deps=(tok_down,))
        gw_up = _mm_tn(h2, da, n_blocks=N_DEV, name="mlp_up_dw")
        tok_up = push(l, [gw_up], ["w_up"], "up")
        dh2 = _mm_nt(da, wup, out_dtype=BF16, name="mlp_up_dx", deps=(tok_up,))
        dx1b, g_mlp[l] = _rmsnorm_bwd(dh2, x1b, norm_mlp[l], dxb, BF16, "norm_mlp_bwd")
        dy = _mm_nt(dx1b, wout, out_dtype=BF16, name="out_proj_dx")
        gw_out = _mm_tn(y, dx1b, n_blocks=1, name="out_proj_dw")
        tok_out = push(l, [gw_out.reshape(N_DEV, d // N_DEV, d)], ["w_out"], "out")
        dproj, g_conv[l], gw_pool, g_scale[l] = _mixer_bwd(
            proj, dy, conv_full[l], pool_full[l], pool_scale[l], "mixer_bwd", deps=(tok_out,))
        gw_in = _mm_tn(h1, dproj, n_blocks=N_DEV, name="in_proj_dw")
        gw_pool = gw_pool.reshape(n_groups, N_DEV, cg_shard, cg).transpose(1, 0, 2, 3)
        tok_in = push(l, [gw_in, gw_pool.reshape(N_DEV, n_groups * cg_shard, cg).astype(BF16)], ["w_in", "pool_w"], "in")
        dh1 = _mm_nt(dproj, win, out_dtype=BF16, name="in_proj_dx", deps=(tok_in,))
        dxb, g_mix[l] = _rmsnorm_bwd(dh1, x0b, norm_mix[l], dx1b, F32 if l == 0 else BF16, "norm_mix_bwd")
        if l + 1 < n_layers:
            land(l + 1, dxb)
    land(0, dxb)
    grad_x = dxb.reshape(x.shape)

    small = _all_gather(
        [jnp.stack(g_conv).reshape(n_layers * 3, dc), jnp.concatenate(g_scale, axis=0),
         jnp.concatenate(g_mix, axis=0), jnp.concatenate(g_mlp, axis=0), g_norm_final], "gather_small_grads")
    conv_parts = lax.dynamic_slice_in_dim(small[0], me * dc_shard, dc_shard, axis=2)

    def update(partials, w, m, v, name, mine=()):
        shape = w.shape
        rc = (shape[0], -1, shape[-1]) if w.ndim > 2 else (1, *shape) if w.ndim == 2 else (1, 1, *shape)
        outs = _adamw(partials, w.reshape(rc), m.reshape(rc), v.reshape(rc), me, name, mine=mine)
        return [o.reshape(shape) for o in outs]

    def exchanged(key, w, m, v):
        return update([received[key, l][1] for l in range(n_layers)], w, m, v, f"adamw_{key}",
                      mine=[received[key, l][0] for l in range(n_layers)])

    results = {
        "w_in": exchanged("w_in", w_in, m_w_in, v_w_in),
        "conv_w": update([conv_parts], conv_w.reshape(n_layers * 3, dc_shard), m_conv_w.reshape(n_layers * 3, dc_shard),
                         v_conv_w.reshape(n_layers * 3, dc_shard), "adamw_conv_w"),
        "pool_w": exchanged("pool_w", pool_w, m_pool_w, v_pool_w),
        "pool_scale": update([small[1]], pool_scale, m_pool_scale, v_pool_scale, "adamw_pool_scale"),
        "w_out": exchanged("w_out", w_out, m_w_out, v_w_out),
        "norm_mix": update([small[2]], norm_mix, m_norm_mix, v_norm_mix, "adamw_norm_mix"),
        "norm_mlp": update([small[3]], norm_mlp, m_norm_mlp, v_norm_mlp, "adamw_norm_mlp"),
        "w_up": exchanged("w_up", w_up, m_w_up, v_w_up),
        "w_down": exchanged("w_down", w_down, m_w_down, v_w_down),
        "norm_final": update([small[4]], norm_final, m_norm_final, v_norm_final, "adamw_norm_final"),
    }
    results["conv_w"] = [o.reshape(conv_w.shape) for o in results["conv_w"]]
    order = ("w_in", "conv_w", "pool_w", "pool_scale", "w_out", "norm_mix", "norm_mlp", "w_up", "w_down", "norm_final")
    return (loss, grad_x, *[results[k][0] for k in order], *[results[k][1] for k in order],
            *[results[k][2] for k in order], *[results[k][3] for k in order])
```

```python
import jax
import jax.numpy as jnp
from jax import lax
from jax.experimental import pallas as pl
from jax.experimental.pallas import tpu as pltpu

F32 = jnp.float32
BF16 = jnp.bfloat16

N_DEV = 8
MESH_AXES = ("x", "y", "c")
NORM_EPS = 1e-6
POOL_WINDOWS = (2, 4, 8, 16)
HALO_ROWS = 16

ADAM_LR = 0.001
ADAM_B1 = 0.9
ADAM_B2 = 0.999
ADAM_EPS = 1e-08
ADAM_WD = 0.01
ADAM_STEP = 10

VMEM_BYTES_V7X = 64 * 1024 * 1024
VMEM_LIMIT = (VMEM_BYTES_V7X * 3) // 4

MM_TILE = 1024
MM_TILE_K = 2048
ROW_TILE = 512
MIXER_TILE = 512
ADAM_BLOCK_ELEMS = 64 * 1024
CAST_BLOCK_ELEMS = 512 * 1024


def _params(*semantics):
    return pltpu.CompilerParams(dimension_semantics=semantics, vmem_limit_bytes=VMEM_LIMIT)


def _tile(dim, pref):
    t = min(dim, pref)
    assert dim % t == 0, (dim, pref)
    return t


def _mesh_position():
    return lax.axis_index("x"), lax.axis_index("y"), lax.axis_index("c")


def _slot(p):
    return 4 * p[0] + 2 * p[1] + p[2]


def _all_gather(arrs, name):
    n = len(arrs)

    def body(*refs):
        ins, outs = refs[:n], refs[n:2 * n]
        send_sems, recv_sems, local_sems = refs[2 * n:]
        x, y, c = _mesh_position()
        me, sibling = (x, y, c), (x, y, 1 - c)
        chips = [(1 - x, y), (x, 1 - y), (1 - x, 1 - y)]

        def copy(a, k, block, to, src=None):
            dst = outs[a].at[_slot(block)]
            return pltpu.make_async_remote_copy(
                src_ref=dst if src is None else src, dst_ref=dst,
                send_sem=send_sems.at[a, k], recv_sem=recv_sems.at[a, k],
                device_id=to, device_id_type=pl.DeviceIdType.MESH)

        mine = [pltpu.make_async_copy(ins[a], outs[a].at[_slot(me)], local_sems.at[a]) for a in range(n)]
        for cp in mine:
            cp.start()
        first = []
        for a in range(n):
            first.append(copy(a, 0, me, sibling, src=ins[a]))
            first += [copy(a, 1 + j, me, (*chip, c), src=ins[a]) for j, chip in enumerate(chips)]
        for cp in first:
            cp.start()
        passed = []
        for j, chip in enumerate(chips):
            for a in range(n):
                copy(a, 1 + j, (*chip, c), me).wait_recv()
                cp = copy(a, 4 + j, (*chip, c), sibling)
                cp.start()
                passed.append(cp)
        for a in range(n):
            copy(a, 0, sibling, me).wait_recv()
            for j, chip in enumerate(chips):
                copy(a, 4 + j, (*chip, 1 - c), me).wait_recv()
        for cp in first + passed:
            cp.wait_send()
        for cp in mine:
            cp.wait()

    any_spec = pl.BlockSpec(memory_space=pl.ANY)
    return pl.pallas_call(
        body, name=name,
        out_shape=[jax.ShapeDtypeStruct((N_DEV, *a.shape), a.dtype) for a in arrs],
        in_specs=[any_spec] * n, out_specs=[any_spec] * n,
        scratch_shapes=[pltpu.SemaphoreType.DMA((n, 7)), pltpu.SemaphoreType.DMA((n, 7)),
                        pltpu.SemaphoreType.DMA((n,))],
    )(*arrs)


def _peer(k):
    x, y, c = _mesh_position()
    return (1 - x if k & 4 else x, 1 - y if k & 2 else y, 1 - c if k & 1 else c)


_HBM = pl.BlockSpec(memory_space=pltpu.HBM)
_SEM = pl.BlockSpec(memory_space=pltpu.SEMAPHORE)
_EFFECT = pltpu.SideEffectType.DATAFLOW_SIDE_EFFECTING


def _exchange_copy(g_ref, land_ref, send_sems, recv_sems, a, k):
    return pltpu.make_async_remote_copy(
        src_ref=g_ref.at[_slot(_peer(k))], dst_ref=land_ref.at[k - 1],
        send_sem=send_sems.at[a * (N_DEV - 1) + k - 1], recv_sem=recv_sems.at[a * (N_DEV - 1) + k - 1],
        device_id=_peer(k), device_id_type=pl.DeviceIdType.MESH)


def _exchange_start(arrs, name):
    n = len(arrs)

    def body(*refs):
        g_refs, land_refs = refs[:n], refs[n:2 * n]
        send_sems, recv_sems = refs[2 * n:2 * n + 2]
        token = refs[-1]
        for k in range(1, N_DEV):
            for a in range(n):
                _exchange_copy(g_refs[a], land_refs[a], send_sems, recv_sems, a, k).start()
        token[...] = jnp.zeros_like(token)

    lands = [lax.empty((N_DEV - 1, *g.shape[1:]), g.dtype) for g in arrs]
    outs = pl.pallas_call(
        body, name=name,
        out_shape=(pltpu.SemaphoreType.DMA((n * (N_DEV - 1),)), pltpu.SemaphoreType.DMA((n * (N_DEV - 1),)),
                   *[pltpu.HBM(g.shape, g.dtype) for g in arrs], *[pltpu.HBM(z.shape, z.dtype) for z in lands],
                   jax.ShapeDtypeStruct((8, 128), F32)),
        in_specs=[_HBM] * (2 * n),
        out_specs=(_SEM, _SEM, *[_HBM] * (2 * n), pl.BlockSpec(memory_space=pltpu.VMEM)),
        input_output_aliases={i: 2 + i for i in range(2 * n)},
        compiler_params=pltpu.CompilerParams(has_side_effects=_EFFECT),
    )(*[pltpu.with_memory_space_constraint(g, pltpu.HBM) for g in arrs],
      *[pltpu.with_memory_space_constraint(z, pltpu.HBM) for z in lands])
    return outs[0], outs[1], list(outs[2:2 + n]), list(outs[2 + n:2 + 2 * n]), outs[-1]


def _exchange_wait(send_sems, recv_sems, arrs, lands, after, name):
    n = len(arrs)

    def body(*refs):
        g_refs, land_refs = refs[:n], refs[n:2 * n]
        send_sems_ref, recv_sems_ref = refs[2 * n:2 * n + 2]
        for k in range(1, N_DEV):
            for a in range(n):
                cp = _exchange_copy(g_refs[a], land_refs[a], send_sems_ref, recv_sems_ref, a, k)
                cp.wait_send()
                cp.wait_recv()

    outs = pl.pallas_call(
        body, name=name,
        out_shape=(*[pltpu.HBM(g.shape, g.dtype) for g in arrs], *[pltpu.HBM(z.shape, z.dtype) for z in lands]),
        in_specs=[_HBM] * (2 * n) + [_SEM, _SEM, pl.BlockSpec(memory_space=pl.ANY)],
        out_specs=[_HBM] * (2 * n),
        input_output_aliases={i: i for i in range(2 * n)},
        compiler_params=pltpu.CompilerParams(has_side_effects=_EFFECT),
    )(*arrs, *lands, send_sems, recv_sems, after)
    return list(outs[:n]), list(outs[n:])


N_FIRST = 4
N_PASSED = 3


def _other_chips():
    x, y, _ = _mesh_position()
    return [(1 - x, y), (x, 1 - y), (1 - x, 1 - y)]


def _gather_copy(src_ref, land_ref, block, to, send_sem, recv_sem):
    rows = land_ref.at[_slot(block)]
    return pltpu.make_async_remote_copy(
        src_ref=rows if src_ref is None else src_ref, dst_ref=rows, send_sem=send_sem, recv_sem=recv_sem,
        device_id=to, device_id_type=pl.DeviceIdType.MESH)


def _gather_start(arrs, me_slot, after, name):
    n = len(arrs)

    def body(*refs):
        src_refs, land_refs = refs[:n], refs[n:2 * n]
        send_sems, recv_sems = refs[2 * n + 1:2 * n + 3]
        token = refs[-1]
        x, y, c = _mesh_position()
        targets = [(x, y, 1 - c)] + [(*chip, c) for chip in _other_chips()]
        for a in range(n):
            for k, to in enumerate(targets):
                _gather_copy(src_refs[a], land_refs[a], (x, y, c), to,
                             send_sems.at[a * N_FIRST + k], recv_sems.at[a * N_FIRST + k]).start()
        token[...] = jnp.zeros_like(token)

    lands = [lax.dynamic_update_slice_in_dim(lax.empty((N_DEV, *a.shape), a.dtype), a[None], me_slot, axis=0)
             for a in arrs]
    outs = pl.pallas_call(
        body, name=name,
        out_shape=(pltpu.SemaphoreType.DMA((n * N_FIRST,)), pltpu.SemaphoreType.DMA((n * N_FIRST,)),
                   *[pltpu.HBM(a.shape, a.dtype) for a in arrs], *[pltpu.HBM(z.shape, z.dtype) for z in lands],
                   jax.ShapeDtypeStruct((8, 128), F32)),
        in_specs=[_HBM] * (2 * n) + [pl.BlockSpec(memory_space=pl.ANY)],
        out_specs=(_SEM, _SEM, *[_HBM] * (2 * n), pl.BlockSpec(memory_space=pltpu.VMEM)),
        input_output_aliases={i: 2 + i for i in range(2 * n)},
        compiler_params=pltpu.CompilerParams(has_side_effects=_EFFECT),
    )(*[pltpu.with_memory_space_constraint(a, pltpu.HBM) for a in arrs],
      *[pltpu.with_memory_space_constraint(z, pltpu.HBM) for z in lands], after)
    return outs[0], outs[1], list(outs[2:2 + n]), list(outs[2 + n:2 + 2 * n]), outs[-1]


def _gather_pass_on(first_recv_sems, base, lands, after, name):
    n = len(lands)

    def body(*refs):
        land_refs = refs[:n]
        first_recv = refs[n]
        send_sems, recv_sems = refs[n + 2:n + 4]
        x, y, c = _mesh_position()
        sibling = (x, y, 1 - c)
        for j, chip in enumerate(_other_chips()):
            for a in range(n):
                _gather_copy(None, land_refs[a], (*chip, c), sibling,
                             send_sems.at[a * N_PASSED + j], first_recv.at[(base + a) * N_FIRST + 1 + j]).wait_recv()
                _gather_copy(None, land_refs[a], (*chip, c), sibling,
                             send_sems.at[a * N_PASSED + j], recv_sems.at[a * N_PASSED + j]).start()

    outs = pl.pallas_call(
        body, name=name,
        out_shape=(pltpu.SemaphoreType.DMA((n * N_PASSED,)), pltpu.SemaphoreType.DMA((n * N_PASSED,)),
                   *[pltpu.HBM(z.shape, z.dtype) for z in lands]),
        in_specs=[_HBM] * n + [_SEM, pl.BlockSpec(memory_space=pl.ANY)],
        out_specs=(_SEM, _SEM, *[_HBM] * n),
        input_output_aliases={i: 2 + i for i in range(n)},
        compiler_params=pltpu.CompilerParams(has_side_effects=_EFFECT),
    )(*lands, first_recv_sems, after)
    return outs[0], outs[1], list(outs[2:])


def _gather_finish(first_sems, base, passed_sems, arrs, lands, after, name):
    n = len(lands)

    def body(*refs):
        src_refs, land_refs = refs[:n], refs[n:2 * n]
        first_send, first_recv, passed_send, passed_recv = refs[2 * n:2 * n + 4]
        x, y, c = _mesh_position()
        sibling = (x, y, 1 - c)
        chips = _other_chips()
        for a in range(n):
            _gather_copy(src_refs[a], land_refs[a], sibling, sibling,
                         first_send.at[(base + a) * N_FIRST], first_recv.at[(base + a) * N_FIRST]).wait_recv()
            for j, chip in enumerate(chips):
                _gather_copy(None, land_refs[a], (*chip, 1 - c), sibling,
                             passed_send.at[a * N_PASSED + j], passed_recv.at[a * N_PASSED + j]).wait_recv()
        for a in range(n):
            for k in range(N_FIRST):
                _gather_copy(src_refs[a], land_refs[a], (x, y, c), sibling,
                             first_send.at[(base + a) * N_FIRST + k], first_recv.at[(base + a) * N_FIRST + k]).wait_send()
            for j, chip in enumerate(chips):
                _gather_copy(None, land_refs[a], (*chip, c), sibling,
                             passed_send.at[a * N_PASSED + j], passed_recv.at[a * N_PASSED + j]).wait_send()

    outs = pl.pallas_call(
        body, name=name,
        out_shape=(*[pltpu.HBM(a.shape, a.dtype) for a in arrs], *[pltpu.HBM(z.shape, z.dtype) for z in lands]),
        in_specs=[_HBM] * (2 * n) + [_SEM] * 4 + [pl.BlockSpec(memory_space=pl.ANY)],
        out_specs=[_HBM] * (2 * n),
        input_output_aliases={i: i for i in range(2 * n)},
        compiler_params=pltpu.CompilerParams(has_side_effects=_EFFECT),
    )(*arrs, *lands, *first_sems, *passed_sems, after)
    return list(outs[n:])


def _run_after(deps, body, in_specs, operands):
    n_deps = len(deps)
    if n_deps == 0:
        return body, in_specs, operands

    def body_behind(*refs):
        body(*refs[n_deps:])

    return body_behind, [pl.BlockSpec(memory_space=pl.ANY)] * n_deps + list(in_specs), list(deps) + list(operands)


def _zero_at_first(acc_ref, k, nk):
    if nk > 1:
        @pl.when(k == 0)
        def _():
            acc_ref[...] = jnp.zeros_like(acc_ref)


def _accumulate(acc_ref, product, k, nk, finish):
    if nk == 1:
        finish(product())
        return

    @pl.when(k < nk - 1)
    def _():
        acc_ref[...] += product()

    @pl.when(k == nk - 1)
    def _():
        finish(acc_ref[...] + product())


def _blocks_per_tile(n_blocks, width, pref):
    if width >= pref:
        return 1
    per = min(n_blocks, pref // width)
    assert n_blocks % per == 0
    return per


def _mm_nn(a, b3, *, out_dtype, name, res=None, with_relu2=False, deps=()):
    m, kdim = a.shape
    nb, kb_, nw = b3.shape
    assert kb_ == kdim
    per_tile = _blocks_per_tile(nb, nw, MM_TILE)
    tw = _tile(nw, MM_TILE)
    per_block = nw // tw
    tn = per_tile * tw
    tm, tk = _tile(m, MM_TILE), _tile(kdim, MM_TILE_K)
    nk = kdim // tk
    n_in = 2 + (res is not None)

    def body(*refs):
        a_ref, b_ref = refs[:2]
        r_ref = refs[2] if res is not None else None
        o_ref = refs[n_in]
        s_ref = refs[n_in + 1] if with_relu2 else None
        acc_ref = refs[-1] if nk > 1 else None
        k = pl.program_id(2)
        _zero_at_first(acc_ref, k, nk)
        for b in range(per_tile):
            cols = slice(b * tw, (b + 1) * tw)

            def product(b=b):
                return jnp.dot(a_ref[...], b_ref[b], preferred_element_type=F32)

            def finish(total, cols=cols):
                if r_ref is not None:
                    total = total + r_ref[:, cols]
                o_ref[:, cols] = total.astype(out_dtype)
                if s_ref is not None:
                    r = jnp.maximum(total, 0.0)
                    s_ref[:, cols] = (r * r).astype(BF16)

            _accumulate(acc_ref.at[:, cols] if nk > 1 else None, product, k, nk, finish)

    if per_tile > 1:
        b_spec = pl.BlockSpec((per_tile, tk, nw), lambda i, j, k: (j, k, 0))
    else:
        b_spec = pl.BlockSpec((1, tk, tw), lambda i, j, k: (j // per_block, k, j % per_block))
    in_specs = [pl.BlockSpec((tm, tk), lambda i, j, k: (i, k)), b_spec]
    operands = [a, b3]
    if res is not None:
        in_specs.append(pl.BlockSpec((tm, tn), lambda i, j, k: (i, j)))
        operands.append(res)
    body, in_specs, operands = _run_after(deps, body, in_specs, operands)
    o_spec = pl.BlockSpec((tm, tn), lambda i, j, k: (i, j))
    out_shape = [jax.ShapeDtypeStruct((m, nb * nw), out_dtype)]
    if with_relu2:
        out_shape.append(jax.ShapeDtypeStruct((m, nb * nw), BF16))
    outs = pl.pallas_call(
        body, name=name, grid=(m // tm, (nb * nw) // tn, nk),
        out_shape=out_shape, in_specs=in_specs, out_specs=[o_spec] * len(out_shape),
        scratch_shapes=[pltpu.VMEM((tm, tn), F32)] if nk > 1 else [],
        compiler_params=_params("parallel", "parallel", "arbitrary"),
    )(*operands)
    return outs if with_relu2 else outs[0]


def _mm_nt(a, b3, *, out_dtype, name, relu2_grad_of=None, deps=()):
    m, kdim = a.shape
    kb, n, kw = b3.shape
    assert kb * kw == kdim
    per_step = _blocks_per_tile(kb, kw, MM_TILE_K)
    tw = _tile(kw, MM_TILE_K)
    per_block = kw // tw
    tk = per_step * tw
    tm, tn = _tile(m, MM_TILE), _tile(n, MM_TILE)
    nk = kdim // tk

    def body(*refs):
        if relu2_grad_of is None:
            a_ref, b_ref, o_ref = refs[:3]
            g_ref = None
        else:
            a_ref, b_ref, g_ref, o_ref = refs[:4]
        acc_ref = refs[-1] if nk > 1 else None
        k = pl.program_id(2)
        _zero_at_first(acc_ref, k, nk)
        def product():
            p = None
            for b in range(per_step):
                part = lax.dot_general(a_ref[:, b * tw:(b + 1) * tw], b_ref[b], (((1,), (1,)), ((), ())),
                                       preferred_element_type=F32)
                p = part if p is None else p + part
            return p

        def finish(total):
            if g_ref is not None:
                total = total * (2.0 * jnp.maximum(g_ref[...].astype(F32), 0.0))
            o_ref[...] = total.astype(out_dtype)

        _accumulate(acc_ref, product, k, nk, finish)

    if per_step > 1:
        b_spec = pl.BlockSpec((per_step, tn, kw), lambda i, j, k: (k, j, 0))
    else:
        b_spec = pl.BlockSpec((1, tn, tw), lambda i, j, k: (k // per_block, j, k % per_block))
    in_specs = [pl.BlockSpec((tm, tk), lambda i, j, k: (i, k)), b_spec]
    operands = [a, b3]
    if relu2_grad_of is not None:
        in_specs.append(pl.BlockSpec((tm, tn), lambda i, j, k: (i, j)))
        operands.append(relu2_grad_of)
    body, in_specs, operands = _run_after(deps, body, in_specs, operands)
    return pl.pallas_call(
        body, name=name, grid=(m // tm, n // tn, nk),
        out_shape=jax.ShapeDtypeStruct((m, n), out_dtype),
        in_specs=in_specs, out_specs=pl.BlockSpec((tm, tn), lambda i, j, k: (i, j)),
        scratch_shapes=[pltpu.VMEM((tm, tn), F32)] if nk > 1 else [],
        compiler_params=_params("parallel", "parallel", "arbitrary"),
    )(*operands)


def _mm_tn(a, b, *, n_blocks, name, deps=()):
    t, m = a.shape
    t2, n = b.shape
    assert t == t2 and n % n_blocks == 0
    nw = n // n_blocks
    per_tile = _blocks_per_tile(n_blocks, nw, MM_TILE)
    tw = _tile(nw, MM_TILE)
    per_block = nw // tw
    tn = per_tile * tw
    tm, tk = _tile(m, MM_TILE), _tile(t, MM_TILE_K)
    nk = t // tk

    def body(a_ref, b_ref, o_ref, *scratch):
        acc_ref = scratch[0] if nk > 1 else None
        k = pl.program_id(2)
        _zero_at_first(acc_ref, k, nk)
        def product():
            return lax.dot_general(a_ref[...], b_ref[...], (((0,), (0,)), ((), ())), preferred_element_type=F32)

        def finish(total):
            for b in range(per_tile):
                o_ref[b] = total[:, b * tw:(b + 1) * tw].astype(BF16)

        _accumulate(acc_ref, product, k, nk, finish)

    if per_tile > 1:
        o_spec = pl.BlockSpec((per_tile, tm, nw), lambda i, j, k: (j, i, 0))
    else:
        o_spec = pl.BlockSpec((1, tm, tw), lambda i, j, k: (j // per_block, i, j % per_block))
    in_specs = [pl.BlockSpec((tk, tm), lambda i, j, k: (k, i)), pl.BlockSpec((tk, tn), lambda i, j, k: (k, j))]
    body, in_specs, operands = _run_after(deps, body, in_specs, [a, b])
    return pl.pallas_call(
        body, name=name, grid=(m // tm, n // tn, nk),
        out_shape=jax.ShapeDtypeStruct((n_blocks, m, nw), BF16),
        in_specs=in_specs, out_specs=o_spec,
        scratch_shapes=[pltpu.VMEM((tm, tn), F32)] if nk > 1 else [],
        compiler_params=_params("parallel", "parallel", "arbitrary"),
    )(*operands)


def _normalise(x):
    r = lax.rsqrt(jnp.mean(x * x, axis=-1, keepdims=True) + NORM_EPS)
    return x * r, r


def _rmsnorm_backward(dh, xhat, r, gain):
    dxhat = dh * gain
    return r * (dxhat - xhat * jnp.mean(dxhat * xhat, axis=-1, keepdims=True))


def _rmsnorm(x, gain, name, deps=()):
    t, d = x.shape
    tr = _tile(t, ROW_TILE)

    def body(x_ref, g_ref, o_ref, xb_ref):
        xv = x_ref[...]
        xhat, _ = _normalise(xv)
        o_ref[...] = (xhat * g_ref[...]).astype(BF16)
        xb_ref[...] = xv.astype(BF16)

    row = pl.BlockSpec((tr, d), lambda i: (i, 0))
    in_specs = [row, pl.BlockSpec((1, d), lambda i: (0, 0))]
    body, in_specs, operands = _run_after(deps, body, in_specs, [x, gain.reshape(1, d)])
    return pl.pallas_call(
        body, name=name, grid=(t // tr,),
        out_shape=[jax.ShapeDtypeStruct((t, d), BF16)] * 2,
        in_specs=in_specs, out_specs=[row, row],
        compiler_params=_params("parallel"),
    )(*operands)


def _rmsnorm_bwd(dh, x, gain, dres, out_dtype, name):
    t, d = x.shape
    tr = _tile(t, ROW_TILE)

    def body(dh_ref, x_ref, g_ref, r_ref, dx_ref, dg_ref):
        xhat, r = _normalise(x_ref[...].astype(F32))
        dh_v = dh_ref[...].astype(F32)
        dx = r_ref[...].astype(F32) + _rmsnorm_backward(dh_v, xhat, r, g_ref[...])
        dx_ref[...] = dx.astype(out_dtype)
        part = jnp.sum(dh_v * xhat, axis=0, keepdims=True)

        @pl.when(pl.program_id(0) == 0)
        def _():
            dg_ref[...] = part

        @pl.when(pl.program_id(0) > 0)
        def _():
            dg_ref[...] += part

    row = pl.BlockSpec((tr, d), lambda i: (i, 0))
    vec = pl.BlockSpec((1, d), lambda i: (0, 0))
    return pl.pallas_call(
        body, name=name, grid=(t // tr,),
        out_shape=[jax.ShapeDtypeStruct((t, d), out_dtype), jax.ShapeDtypeStruct((1, d), F32)],
        in_specs=[row, row, vec, row], out_specs=[row, vec],
        compiler_params=_params("arbitrary"),
    )(dh, x, gain.reshape(1, d), dres)


def _loss_head(x, gain, target, name):
    t, d = x.shape
    tr = _tile(t, ROW_TILE)
    steps = t // tr

    def body(x_ref, g_ref, t_ref, loss_ref, dxb_ref, dg_ref, sq_ref):
        i = pl.program_id(0)
        xhat, r = _normalise(x_ref[...])
        gain_v = g_ref[...]
        diff = xhat * gain_v - t_ref[...]
        dy = diff / float(d)
        dxb_ref[...] = _rmsnorm_backward(dy, xhat, r, gain_v).astype(BF16)
        dg_part = jnp.sum(dy * xhat, axis=0, keepdims=True)
        sq_part = jnp.sum(diff * diff, axis=0, keepdims=True)

        @pl.when(i == 0)
        def _():
            dg_ref[...] = dg_part
            sq_ref[...] = sq_part

        @pl.when(i > 0)
        def _():
            dg_ref[...] += dg_part
            sq_ref[...] += sq_part

        @pl.when(i == steps - 1)
        def _():
            loss_ref[...] = (0.5 / float(d)) * jnp.sum(sq_ref[...], axis=1, keepdims=True)

    row = pl.BlockSpec((tr, d), lambda i: (i, 0))
    vec = pl.BlockSpec((1, d), lambda i: (0, 0))
    return pl.pallas_call(
        body, name=name, grid=(steps,),
        out_shape=[jax.ShapeDtypeStruct((1, 1), F32), jax.ShapeDtypeStruct((t, d), BF16),
                   jax.ShapeDtypeStruct((1, d), F32)],
        in_specs=[row, vec, row],
        out_specs=[pl.BlockSpec((1, 1), lambda i: (0, 0)), row, vec],
        scratch_shapes=[pltpu.VMEM((1, d), F32)],
        compiler_params=_params("arbitrary"),
    )(x, gain.reshape(1, d), target)


def _shift_down(ext, s):
    return pltpu.roll(ext, s, 0)


def _shift_up(ext, s):
    return pltpu.roll(ext, ext.shape[0] - s, 0)


def _window_sum(ext, w, shift):
    s = 1
    while s < w:
        ext = ext + shift(ext, s)
        s *= 2
    return ext


def _window_count(tile_index, rows, cols, w):
    t = tile_index * rows + lax.broadcasted_iota(jnp.int32, (rows, cols), 0)
    return jnp.minimum(t + 1, w).astype(F32)


def _mixer_sizes(proj, conv_w):
    t, e = proj.shape
    dc = conv_w.shape[1]
    dp = e - 3 * dc
    cg = dp // len(POOL_WINDOWS)
    tt = _tile(t, MIXER_TILE)
    assert tt % HALO_ROWS == 0 and tt >= HALO_ROWS
    cw = _tile(dc, cg)
    return t, e, dc, dp, cg, tt, cw


def _mixer_fwd(proj, conv_w, pool_w, pool_scale, name, deps=()):
    t, e, dc, dp, cg, tt, cw = _mixer_sizes(proj, conv_w)
    per_halo = tt // HALO_ROWS

    def body(cur_ref, prev_ref, cw_ref, pw_ref, ps_ref, y_ref):
        i = pl.program_id(0)
        first = i == 0

        def cur(lo, width):
            return cur_ref[:, lo:lo + width].astype(F32)

        def prev(lo, width):
            return jnp.where(first, 0.0, prev_ref[:, lo:lo + width].astype(F32))

        for lo in range(0, dc, cw):
            u = cur(dc + lo, cw) * cur(2 * dc + lo, cw)
            ext = jnp.concatenate([prev(dc + lo, cw) * prev(2 * dc + lo, cw), u], axis=0)
            u1 = _shift_down(ext, 1)[HALO_ROWS:]
            u2 = _shift_down(ext, 2)[HALO_ROWS:]
            conv = cw_ref[0:1, lo:lo + cw] * u2 + cw_ref[1:2, lo:lo + cw] * u1 + cw_ref[2:3, lo:lo + cw] * u
            y_ref[:, lo:lo + cw] = (cur(lo, cw) * conv).astype(BF16)

        for g, w in enumerate(POOL_WINDOWS):
            lo = 3 * dc + g * cg
            v = cur(lo, cg)
            ext = jnp.concatenate([prev(lo, cg), v], axis=0)
            mean = _window_sum(ext, w, _shift_down)[HALO_ROWS:] / _window_count(i, tt, cg, w)
            z = jnp.dot((mean - v).astype(BF16), pw_ref[g], preferred_element_type=F32)
            y_ref[:, dc + g * cg:dc + (g + 1) * cg] = (z * ps_ref[0:1, g * cg:(g + 1) * cg]).astype(BF16)

    in_specs = [pl.BlockSpec((tt, e), lambda i: (i, 0)),
                pl.BlockSpec((HALO_ROWS, e), lambda i: (jnp.maximum(i * per_halo - 1, 0), 0)),
                pl.BlockSpec((3, dc), lambda i: (0, 0)),
                pl.BlockSpec((len(POOL_WINDOWS), cg, cg), lambda i: (0, 0, 0)),
                pl.BlockSpec((1, dp), lambda i: (0, 0))]
    body, in_specs, operands = _run_after(
        deps, body, in_specs, [proj, proj, conv_w, pool_w, pool_scale.reshape(1, dp)])
    return pl.pallas_call(
        body, name=name, grid=(t // tt,),
        out_shape=jax.ShapeDtypeStruct((t, dc + dp), BF16),
        in_specs=in_specs,
        out_specs=pl.BlockSpec((tt, dc + dp), lambda i: (i, 0)),
        compiler_params=_params("parallel"),
    )(*operands)


def _mixer_bwd(proj, dy, conv_w, pool_w, pool_scale, name, deps=()):
    t, e, dc, dp, cg, tt, cw = _mixer_sizes(proj, conv_w)
    per_halo = tt // HALO_ROWS
    steps = t // tt
    n_groups = len(POOL_WINDOWS)

    def body(cur_ref, prev_ref, next_ref, dy_ref, dyn_ref, cw_ref, pw_ref, ps_ref,
             dp_ref, dcw_ref, dpw_ref, dps_ref):
        i = pl.program_id(0)
        first = i == 0
        last = i == steps - 1

        @pl.when(first)
        def _():
            dcw_ref[...] = jnp.zeros_like(dcw_ref)
            dpw_ref[...] = jnp.zeros_like(dpw_ref)
            dps_ref[...] = jnp.zeros_like(dps_ref)

        def cur(lo, width):
            return cur_ref[:, lo:lo + width].astype(F32)

        def prev(lo, width):
            return jnp.where(first, 0.0, prev_ref[:, lo:lo + width].astype(F32))

        def nxt(ref, lo, width):
            return jnp.where(last, 0.0, ref[:, lo:lo + width].astype(F32))

        def colsum(v):
            return jnp.sum(v, axis=0, keepdims=True)

        for lo in range(0, dc, cw):
            cols = slice(lo, lo + cw)
            b, c, xt = cur(lo, cw), cur(dc + lo, cw), cur(2 * dc + lo, cw)
            u = c * xt
            ext = jnp.concatenate([prev(dc + lo, cw) * prev(2 * dc + lo, cw), u], axis=0)
            u1 = _shift_down(ext, 1)[HALO_ROWS:]
            u2 = _shift_down(ext, 2)[HALO_ROWS:]
            w0, w1, w2 = cw_ref[0:1, cols], cw_ref[1:2, cols], cw_ref[2:3, cols]
            dyc = dy_ref[:, cols].astype(F32)
            dp_ref[:, cols] = (dyc * (w0 * u2 + w1 * u1 + w2 * u)).astype(BF16)
            dconv = dyc * b
            dcw_ref[0:1, cols] += colsum(dconv * u2)
            dcw_ref[1:2, cols] += colsum(dconv * u1)
            dcw_ref[2:3, cols] += colsum(dconv * u)
            dext = jnp.concatenate([dconv, nxt(dyn_ref, lo, cw) * nxt(next_ref, lo, cw)], axis=0)
            du = w2 * dconv + w1 * _shift_up(dext, 1)[:tt] + w0 * _shift_up(dext, 2)[:tt]
            dp_ref[:, dc + lo:dc + lo + cw] = (du * xt).astype(BF16)
            dp_ref[:, 2 * dc + lo:2 * dc + lo + cw] = (du * c).astype(BF16)

        for g, w in enumerate(POOL_WINDOWS):
            lo = 3 * dc + g * cg
            ycols = slice(dc + g * cg, dc + (g + 1) * cg)
            pcols = slice(g * cg, (g + 1) * cg)
            v = cur(lo, cg)
            ext = jnp.concatenate([prev(lo, cg), v], axis=0)
            count = _window_count(i, tt, cg, w)
            d = ((_window_sum(ext, w, _shift_down)[HALO_ROWS:] / count) - v).astype(BF16)
            pw = pw_ref[g]
            scale = ps_ref[0:1, pcols]
            dyp = dy_ref[:, ycols].astype(F32)
            z = jnp.dot(d, pw, preferred_element_type=F32)
            dps_ref[0:1, pcols] += colsum(dyp * z)
            dz = (dyp * scale).astype(BF16)
            dpw_ref[g] += lax.dot_general(d, dz, (((0,), (0,)), ((), ())), preferred_element_type=F32)
            dd = lax.dot_general(dz, pw, (((1,), (1,)), ((), ())), preferred_element_type=F32)
            dzn = (nxt(dyn_ref, dc + g * cg, cg) * scale).astype(BF16)
            ddn = lax.dot_general(dzn, pw, (((1,), (1,)), ((), ())), preferred_element_type=F32)
            qext = jnp.concatenate([dd / count, ddn / float(w)], axis=0)
            dp_ref[:, lo:lo + cg] = (_window_sum(qext, w, _shift_up)[:tt] - dd).astype(BF16)

    cur_spec = lambda width: pl.BlockSpec((tt, width), lambda i: (i, 0))
    prev_spec = pl.BlockSpec((HALO_ROWS, e), lambda i: (jnp.maximum(i * per_halo - 1, 0), 0))
    next_spec = lambda width: pl.BlockSpec(
        (HALO_ROWS, width), lambda i: (jnp.minimum((i + 1) * per_halo, t // HALO_ROWS - 1), 0))
    in_specs = [cur_spec(e), prev_spec, next_spec(e), cur_spec(dc + dp), next_spec(dc + dp),
                pl.BlockSpec((3, dc), lambda i: (0, 0)),
                pl.BlockSpec((n_groups, cg, cg), lambda i: (0, 0, 0)),
                pl.BlockSpec((1, dp), lambda i: (0, 0))]
    body, in_specs, operands = _run_after(
        deps, body, in_specs, [proj, proj, proj, dy, dy, conv_w, pool_w, pool_scale.reshape(1, dp)])
    return pl.pallas_call(
        body, name=name, grid=(steps,),
        out_shape=[jax.ShapeDtypeStruct((t, e), BF16), jax.ShapeDtypeStruct((3, dc), F32),
                   jax.ShapeDtypeStruct((n_groups, cg, cg), F32), jax.ShapeDtypeStruct((1, dp), F32)],
        in_specs=in_specs,
        out_specs=[cur_spec(e), pl.BlockSpec((3, dc), lambda i: (0, 0)),
                   pl.BlockSpec((n_groups, cg, cg), lambda i: (0, 0, 0)),
                   pl.BlockSpec((1, dp), lambda i: (0, 0))],
        compiler_params=_params("arbitrary"),
    )(*operands)


def _adamw(partials, w, m, v, me_slot, name, mine=()):
    n_layers, r, c = w.shape
    assert len(partials) == n_layers and len(mine) in (0, n_layers)
    n_mine = len(mine)
    tr = r if r * c <= ADAM_BLOCK_ELEMS else _tile(r, max(16, ADAM_BLOCK_ELEMS // c))

    def body(me_ref, *refs):
        own_refs, p_refs = refs[:n_mine], refs[n_mine:n_mine + n_layers]
        w_ref, m_ref, v_ref, g_out, d_out, m_out, v_out = refs[n_mine + n_layers:]
        for l in range(n_layers):
            g = own_refs[l][0].astype(F32) if n_mine else None
            for s in range(p_refs[l].shape[0]):
                part = p_refs[l][s].astype(F32)
                g = part if g is None else g + part
            m_new = ADAM_B1 * m_ref[l] + (1.0 - ADAM_B1) * g
            v_new = ADAM_B2 * v_ref[l] + (1.0 - ADAM_B2) * (g * g)
            m_hat = m_new / (1.0 - ADAM_B1 ** ADAM_STEP)
            v_hat = v_new / (1.0 - ADAM_B2 ** ADAM_STEP)
            g_out[l] = g
            d_out[l] = -ADAM_LR * (m_hat / (jnp.sqrt(v_hat) + ADAM_EPS) + ADAM_WD * w_ref[l])
            m_out[l] = m_new
            v_out[l] = v_new

    own_spec = pl.BlockSpec((1, tr, c), lambda i, me_ref: (me_ref[0], i, 0))
    p_specs = [pl.BlockSpec((p.shape[0], tr, c), lambda i, me_ref: (0, i, 0)) for p in partials]
    w_spec = pl.BlockSpec((n_layers, tr, c), lambda i, me_ref: (0, i, 0))
    return pl.pallas_call(
        body, name=name,
        grid_spec=pltpu.PrefetchScalarGridSpec(
            num_scalar_prefetch=1, grid=(r // tr,),
            in_specs=[own_spec] * n_mine + p_specs + [w_spec] * 3, out_specs=[w_spec] * 4),
        out_shape=[jax.ShapeDtypeStruct(w.shape, F32)] * 4,
        compiler_params=_params("parallel"),
    )(jnp.reshape(me_slot, (1,)).astype(jnp.int32), *mine, *partials, w, m, v)


def _to_bf16(w, first, last, name, deps=()):
    _, r, c = w.shape
    tr = r if r * c <= CAST_BLOCK_ELEMS else _tile(r, max(16, CAST_BLOCK_ELEMS // c))

    def body(w_ref, o_ref):
        o_ref[...] = w_ref[...].astype(BF16)

    in_specs = [pl.BlockSpec((1, tr, c), lambda l, i: (first + l, i, 0))]
    body, in_specs, operands = _run_after(deps, body, in_specs, [w])
    return pl.pallas_call(
        body, name=name, grid=(last - first, r // tr),
        out_shape=jax.ShapeDtypeStruct((last - first, r, c), BF16),
        in_specs=in_specs, out_specs=pl.BlockSpec((1, tr, c), lambda l, i: (l, i, 0)),
        compiler_params=_params("parallel", "parallel"),
    )(*operands)


def kernel(x, w_in, conv_w, pool_w, pool_scale, w_out, norm_mix, norm_mlp, w_up, w_down, norm_final, loss_target, m_w_in, m_conv_w, m_pool_w, m_pool_scale, m_w_out, m_norm_mix, m_norm_mlp, m_w_up, m_w_down, m_norm_final, v_w_in, v_conv_w, v_pool_w, v_pool_scale, v_w_out, v_norm_mix, v_norm_mlp, v_w_up, v_w_down, v_norm_final):
    n_layers, d, e_shard = w_in.shape
    t = x.shape[1]
    n_groups, cg_shard, cg = pool_w.shape[1:]
    dc_shard = conv_w.shape[2]
    dc, dp = dc_shard * N_DEV, n_groups * cg
    f_shard = w_up.shape[2]
    xs = x.reshape(t, d)
    target = loss_target.reshape(t, d)

    me = _slot(_mesh_position())
    big = (w_in, w_out, w_up, w_down)
    n_kinds = len(big)
    first_layer = [_to_bf16(w, 0, 1, "cast_first_layer")[0] for w in big]
    mixer_blocks = [conv_w.reshape(n_layers * 3, dc_shard), pool_w.reshape(n_layers * n_groups * cg_shard, cg)]
    groups = {}

    def start(tag, blocks, after):
        send_sems, recv_sems, blocks, lands, token = _gather_start(blocks, me, after, f"weights_start_{tag}")
        groups[tag] = (send_sems, recv_sems, blocks, lands)
        return token

    token = start("first", [first_layer[0], *mixer_blocks, *first_layer[1:]], xs)
    mixer_place = ("first", 1)

    def place(l, k):
        return ("first", 0 if k == 0 else k + len(mixer_blocks)) if l == 0 else ("rest", n_kinds * (l - 1) + k)

    def passed_on(where, n, after, name):
        tag, i = where
        lands = groups[tag][3]
        send_sems, recv_sems, lands[i:i + n] = _gather_pass_on(groups[tag][1], i, lands[i:i + n], after, f"weights_pass_{name}")
        return send_sems, recv_sems

    def gathered(where, n, passed_sems, after, name):
        tag, i = where
        first_send, first_recv, blocks, lands = groups[tag]
        lands[i:i + n] = _gather_finish((first_send, first_recv), i, passed_sems, blocks[i:i + n], lands[i:i + n], after,
                                        f"weights_finish_{name}")
        return lands[i:i + n]

    def landing(where):
        return groups[where[0]][3][where[1]]

    saved, weights = [], []
    xc = xs
    after = token
    for l in range(n_layers):
        if l == 0:
            h1, x0b = _rmsnorm(xc, norm_mix[l], "norm_mix", deps=(after,))
            sems = passed_on(place(l, 0), 1, h1, f"{l}_0")
        else:
            sems = passed_on(place(l, 0), 1, after, f"{l}_0")
            h1, x0b = _rmsnorm(xc, norm_mix[l], "norm_mix", deps=(landing(place(l, 0)),))
        win, = gathered(place(l, 0), 1, sems, h1, f"{l}_0")
        proj = _mm_nn(h1, win, out_dtype=BF16, name="in_proj")
        if l == 0 and n_layers > 1:
            others = [_to_bf16(w, 1, n_layers, "cast_other_layers", deps=(proj,)) for w in big]
            token = start("rest", [others[k][j - 1] for j in range(1, n_layers) for k in range(n_kinds)], proj)
        if l == 0:
            mixer_sems = passed_on(mixer_place, len(mixer_blocks), token, "mixer")
        sems = passed_on(place(l, 1), 1, proj, f"{l}_1")
        if l == 0:
            conv_g, pool_g = gathered(mixer_place, len(mixer_blocks), mixer_sems, proj, "mixer")
            conv_full = conv_g.reshape(N_DEV, n_layers, 3, dc_shard).transpose(1, 2, 0, 3).reshape(n_layers, 3, dc)
            pool_full = pool_g.reshape(N_DEV, n_layers, n_groups, cg_shard, cg).transpose(1, 2, 0, 3, 4)
            pool_full = pool_full.reshape(n_layers, n_groups, cg, cg).astype(BF16)
        y = _mixer_fwd(proj, conv_full[l], pool_full[l], pool_scale[l], "mixer_fwd", deps=(landing(place(l, 1)),))
        wout = gathered(place(l, 1), 1, sems, y, f"{l}_1")[0].reshape(1, d, d)
        x1 = _mm_nn(y, wout, out_dtype=F32, res=xc, name="out_proj")
        sems = passed_on(place(l, 2), 1, x1, f"{l}_2")
        h2, x1b = _rmsnorm(x1, norm_mlp[l], "norm_mlp", deps=(landing(place(l, 2)),))
        wup, = gathered(place(l, 2), 1, sems, h2, f"{l}_2")
        if l < 2:
            a, s = _mm_nn(h2, wup, out_dtype=BF16, with_relu2=True, name="mlp_up")
            sems = passed_on(place(l, 3), 1, a, f"{l}_3")
        else:
            sems = passed_on(place(l, 3), 1, h2, f"{l}_3")
            a, s = _mm_nn(h2, wup, out_dtype=BF16, with_relu2=True, name="mlp_up", deps=(landing(place(l, 3)),))
        wdown = gathered(place(l, 3), 1, sems, a, f"{l}_3")[0].reshape(1, f_shard * N_DEV, d)
        x2 = _mm_nn(s, wdown, out_dtype=F32, res=x1, name="mlp_down")
        weights.append((win, wout, wup, wdown))
        saved.append((x0b, h1, proj, y, x1b, h2, a, s))
        xc = after = x2

    loss_part, dxb, g_norm_final = _loss_head(xc, norm_final, target, "loss_head")
    loss = lax.psum(loss_part[0, 0], MESH_AXES)

    in_flight = {}
    received = {}

    def push(l, arrs, keys, tag):
        send_sems, recv_sems, thru, lands, token = _exchange_start(arrs, f"grads_start_{tag}_{l}")
        in_flight.setdefault(l, []).append((send_sems, recv_sems, thru, lands, keys, tag))
        return token

    def land(l, after):
        for send_sems, recv_sems, thru, lands, keys, tag in in_flight.pop(l):
            thru, lands = _exchange_wait(send_sems, recv_sems, thru, lands, after, f"grads_wait_{tag}_{l}")
            for key, own, got in zip(keys, thru, lands):
                received[key, l] = (own, got)

    g_conv, g_scale, g_mix, g_mlp = ([None] * n_layers for _ in range(4))
    for l in reversed(range(n_layers)):
        win, wout, wup, wdown = weights[l]
        x0b, h1, proj, y, x1b, h2, a, s = saved[l]
        gw_down = _mm_tn(s, dxb, n_blocks=1, name="mlp_down_dw")
        tok_down = push(l, [gw_down.reshape(N_DEV, f_shard, d)], ["w_down"], "down")
        da = _mm_nt(dxb, wdown, out_dtype=BF16, relu2_grad_of=a, name="mlp_down_dx", deps=(tok_down,))
        gw_up = _mm_tn(h2, da, n_blocks=N_DEV, name="mlp_up_dw")
        tok_up = push(l, [gw_up], ["w_up"], "up")
        dh2 = _mm_nt(da, wup, out_dtype=BF16, name="mlp_up_dx", deps=(tok_up,))
        dx1b, g_mlp[l] = _rmsnorm_bwd(dh2, x1b, norm_mlp[l], dxb, BF16, "norm_mlp_bwd")
        dy = _mm_nt(dx1b, wout, out_dtype=BF16, name="out_proj_dx")
        gw_out = _mm_tn(y, dx1b, n_blocks=1, name="out_proj_dw")
        tok_out = push(l, [gw_out.reshape(N_DEV, d // N_DEV, d)], ["w_out"], "out")
        dproj, g_conv[l], gw_pool, g_scale[l] = _mixer_bwd(
            proj, dy, conv_full[l], pool_full[l], pool_scale[l], "mixer_bwd", deps=(tok_out,))
        gw_in = _mm_tn(h1, dproj, n_blocks=N_DEV, name="in_proj_dw")
        gw_pool = gw_pool.reshape(n_groups, N_DEV, cg_shard, cg).transpose(1, 0, 2, 3)
        tok_in = push(l, [gw_in, gw_pool.reshape(N_DEV, n_groups * cg_shard, cg).astype(BF16)], ["w_in", "pool_w"], "in")
        dh1 = _mm_nt(dproj, win, out_dtype=BF16, name="in_proj_dx", deps=(tok_in,))
        dxb, g_mix[l] = _rmsnorm_bwd(dh1, x0b, norm_mix[l], dx1b, F32 if l == 0 else BF16, "norm_mix_bwd")
        if l + 1 < n_layers:
            land(l + 1, dxb)
    land(0, dxb)
    grad_x = dxb.reshape(x.shape)

    small = _all_gather(
        [jnp.stack(g_conv).reshape(n_layers * 3, dc), jnp.concatenate(g_scale, axis=0),
         jnp.concatenate(g_mix, axis=0), jnp.concatenate(g_mlp, axis=0), g_norm_final], "gather_small_grads")
    conv_parts = lax.dynamic_slice_in_dim(small[0], me * dc_shard, dc_shard, axis=2)

    def update(partials, w, m, v, name, mine=()):
        shape = w.shape
        rc = (shape[0], -1, shape[-1]) if w.ndim > 2 else (1, *shape) if w.ndim == 2 else (1, 1, *shape)
        outs = _adamw(partials, w.reshape(rc), m.reshape(rc), v.reshape(rc), me, name, mine=mine)
        return [o.reshape(shape) for o in outs]

    def exchanged(key, w, m, v):
        return update([received[key, l][1] for l in range(n_layers)], w, m, v, f"adamw_{key}",
                      mine=[received[key, l][0] for l in range(n_layers)])

    results = {
        "w_in": exchanged("w_in", w_in, m_w_in, v_w_in),
        "conv_w": update([conv_parts], conv_w.reshape(n_layers * 3, dc_shard), m_conv_w.reshape(n_layers * 3, dc_shard),
                         v_conv_w.reshape(n_layers * 3, dc_shard), "adamw_conv_w"),
        "pool_w": exchanged("pool_w", pool_w, m_pool_w, v_pool_w),
        "pool_scale": update([small[1]], pool_scale, m_pool_scale, v_pool_scale, "adamw_pool_scale"),
        "w_out": exchanged("w_out", w_out, m_w_out, v_w_out),
        "norm_mix": update([small[2]], norm_mix, m_norm_mix, v_norm_mix, "adamw_norm_mix"),
        "norm_mlp": update([small[3]], norm_mlp, m_norm_mlp, v_norm_mlp, "adamw_norm_mlp"),
        "w_up": exchanged("w_up", w_up, m_w_up, v_w_up),
        "w_down": exchanged("w_down", w_down, m_w_down, v_w_down),
        "norm_final": update([small[4]], norm_final, m_norm_final, v_norm_final, "adamw_norm_final"),
    }
    results["conv_w"] = [o.reshape(conv_w.shape) for o in results["conv_w"]]
    order = ("w_in", "conv_w", "pool_w", "pool_scale", "w_out", "norm_mix", "norm_mlp", "w_up", "w_down", "norm_final")
    return (loss, grad_x, *[results[k][0] for k in order], *[results[k][1] for k in order],
            *[results[k][2] for k in order], *[results[k][3] for k in order])
```
